```python
import math
import jax
import jax.numpy as jnp
from jax import lax
import numpy as np

D_MODEL = 2048
BATCH = 32
SEQ = 256
DEPTH = 2
DEC_BATCH = 2
DEC_SEQ = 4096
PAST_LEN = 512

GRID_W = 64
N_EVEN = (DEPTH + 1) // 2
N_ODD = DEPTH // 2
NORM_EPS = 1e-6

A_HEAD_DIM = 64
A_WIDTH = D_MODEL // 2
A_HEADS = A_WIDTH // A_HEAD_DIM
A_DECAY_RANK = 64
A_ICLR_RANK = 64
A_GATE_RANK = 128
A_COLS = 3 * A_WIDTH + 2 * A_DECAY_RANK + 2 * A_ICLR_RANK + A_GATE_RANK
A_GN_EPS = 64e-5

B_HEAD_DIM = 128
B_WIDTH = D_MODEL // 2
B_Q_HEADS = B_WIDTH // B_HEAD_DIM
B_KV_HEADS = B_Q_HEADS // 4
B_GROUP = B_Q_HEADS // B_KV_HEADS
B_KV_WIDTH = B_KV_HEADS * B_HEAD_DIM
B_COLS = B_WIDTH + 2 * B_KV_WIDTH
ROPE_THETA = 10000.0
Q_BLOCK = 128

C_WIDTH = D_MODEL
HYENA_ORDER = 2
FILT_EMB = 33
FILT_BANDS = (FILT_EMB - 1) // 2
FILT_HIDDEN = 64
DECAY_TARGET = 1e-2
FAST_DECAY_PCT = 0.3
SLOW_DECAY_PCT = 1.5

PEER_HEADS = 8
PEER_NKEYS = 128
PEER_EXPERTS = PEER_NKEYS * PEER_NKEYS
PEER_QDIM = 256
PEER_TOPK = 16
PEER_CHUNK = 128

kernel_name = "hybrid_diffusion_rwkv7_gqa_hyena_peer_step"

F32 = jnp.float32


def _rmsnorm(x, g):
    xf = x.astype(F32)
    y = xf * lax.rsqrt(jnp.mean(xf * xf, axis=-1, keepdims=True) + NORM_EPS)
    return y * g.astype(F32)


def _short_conv3(x, w):
    xp = jnp.pad(x, ((0, 0), (1, 1), (0, 0)))
    return xp[:, :-2] * w[0] + xp[:, 1:-1] * w[1] + xp[:, 2:] * w[2]


def _axial_rope(x):
    L = x.shape[1]
    n_rows = L // GRID_W
    row = jnp.repeat(jnp.arange(n_rows, dtype=F32), GRID_W)
    col = jnp.tile(jnp.arange(GRID_W, dtype=F32), n_rows)
    half = x.shape[-1] // 2
    nf = half // 2
    inv = ROPE_THETA ** (-jnp.arange(nf, dtype=F32) / nf)
    shape = (1, L) + (1,) * (x.ndim - 3) + (nf,)
    xf = x.astype(F32)

    def rot(xh, pos):
        ang = (pos[:, None] * inv[None, :]).reshape(shape)
        cs, sn = jnp.cos(ang), jnp.sin(ang)
        x1, x2 = xh[..., :nf], xh[..., nf:]
        return jnp.concatenate([x1 * cs - x2 * sn, x1 * sn + x2 * cs], axis=-1)

    return jnp.concatenate([rot(xf[..., :half], row), rot(xf[..., half:], col)], axis=-1)


def _attend(q, k, v):
    Bn, Lq, Hkv, G, Dh = q.shape
    nb = Lq // Q_BLOCK
    qb = jnp.moveaxis(q.astype(F32).reshape(Bn, nb, Q_BLOCK, Hkv, G, Dh), 1, 0)
    kf, vf = k.astype(F32), v.astype(F32)
    scale = Dh ** -0.5

    def one(qblk):
        s = jnp.einsum('bqhgd,bkhd->bhgqk', qblk, kf) * scale
        p = jax.nn.softmax(s, axis=-1)
        return jnp.einsum('bhgqk,bkhd->bqhgd', p, vf)

    o = lax.map(one, qb)
    return jnp.moveaxis(o, 0, 1).reshape(Bn, Lq, Hkv * G * Dh)


def _bidir_delta_scan(s0, r, w, kk, kt, a, v):
    def both(t):
        return jnp.stack([t, jnp.flip(t, 1)], axis=2)

    def rev1(t):
        return jnp.stack([t[:, :, 0], jnp.flip(t[:, :, 1], 1)], axis=2)

    xs = tuple(jnp.moveaxis(t.astype(F32), 1, 0)
               for t in (both(r), rev1(w), both(kk), rev1(kt), rev1(a), both(v)))

    def step(S, inp):
        r_t, w_t, kk_t, kt_t, a_t, v_t = inp
        sa = jnp.einsum('bdhvk,bdhk->bdhv', S, kk_t)
        S = (S * w_t[..., None, :] - sa[..., None] * (kk_t * a_t)[..., None, :]
             + v_t[..., None] * kt_t[..., None, :])
        return S, jnp.einsum('bdhvk,bdhk->bdhv', S, r_t)

    s_fin, ys = lax.scan(step, s0.astype(F32), xs)
    ys = jnp.moveaxis(ys, 0, 1)
    return ys[:, :, 0] + jnp.flip(ys[:, :, 1], 1), s_fin


def _rwkv7_mixer(za, s0, conv, w0, wu, a0, au, gu, k_k, k_a, r_k, ln_w, ln_b):
    Bn, L, _ = za.shape
    W, Rw, Ra = A_WIDTH, A_DECAY_RANK, A_ICLR_RANK
    za = _short_conv3(za.astype(F32), conv)
    r = za[..., :W]
    k = za[..., W:2 * W]
    v = za[..., 2 * W:3 * W]
    o = 3 * W
    wd = za[..., o:o + 2 * Rw].reshape(Bn, L, 2, Rw)
    o += 2 * Rw
    ad = za[..., o:o + 2 * Ra].reshape(Bn, L, 2, Ra)
    o += 2 * Ra
    gd = za[..., o:]
    w_log = -jax.nn.softplus(-(w0 + jnp.einsum('bldr,drc->bldc', jnp.tanh(wd), wu))) - 0.5
    w = jnp.exp(-jnp.exp(w_log))
    a = jax.nn.sigmoid(a0 + jnp.einsum('bldr,drc->bldc', ad, au))
    g = jax.nn.sigmoid(gd) @ gu
    hs = (Bn, L, A_HEADS, A_HEAD_DIM)
    dh = (Bn, L, 2, A_HEADS, A_HEAD_DIM)
    kk = (k * k_k).reshape(hs)
    kk = kk * lax.rsqrt(jnp.sum(kk * kk, axis=-1, keepdims=True) + 1e-12)
    kt = k[:, :, None, :] * (1.0 + (a - 1.0) * k_a)
    r_h, k_h, v_h = r.reshape(hs), k.reshape(hs), v.reshape(hs)
    y, s_fin = _bidir_delta_scan(s0, r_h, w.reshape(dh), kk, kt.reshape(dh), a.reshape(dh), v_h)
    mu = jnp.mean(y, axis=-1, keepdims=True)
    var = jnp.mean(jnp.square(y - mu), axis=-1, keepdims=True)
    yn = ((y - mu) * lax.rsqrt(var + A_GN_EPS)).reshape(Bn, L, W) * ln_w + ln_b
    bonus = (jnp.sum(r_h * k_h * r_k, axis=-1, keepdims=True) * v_h).reshape(Bn, L, W)
    return (yn + bonus) * g, s_fin


def _even_mixer(h, P, j, latent, cache_k, cache_v, state_a):
    Bn, L, _ = h.shape
    z = h @ P['even_w_in'][j]
    za, zb = z[..., :A_COLS], z[..., A_COLS:]
    q = _rmsnorm(zb[..., :B_WIDTH].reshape(Bn, L, B_KV_HEADS, B_GROUP, B_HEAD_DIM), P['even_b_qnorm'][j])
    k = _rmsnorm(zb[..., B_WIDTH:B_WIDTH + B_KV_WIDTH].reshape(Bn, L, B_KV_HEADS, B_HEAD_DIM), P['even_b_knorm'][j])
    v = zb[..., B_WIDTH + B_KV_WIDTH:].reshape(Bn, L, B_KV_HEADS, B_HEAD_DIM).astype(F32)
    if latent:
        q = _axial_rope(q)
        keys = jnp.concatenate([_axial_rope(k), cache_k[:, j].astype(F32)], axis=1)
        vals = jnp.concatenate([v, cache_v[:, j].astype(F32)], axis=1)
        s0 = state_a[:, j]
    else:
        keys, vals = k, v
        s0 = jnp.zeros((Bn, 2, A_HEADS, A_HEAD_DIM, A_HEAD_DIM), F32)
    y_b = _attend(q, keys, vals)
    y_a, s_fin = _rwkv7_mixer(za, s0, P['even_a_conv'][j], P['even_a_w0'][j], P['even_a_wu'][j],
                              P['even_a_a0'][j], P['even_a_au'][j], P['even_a_gu'][j], P['even_a_kk'][j],
                              P['even_a_ka'][j], P['even_a_rk'][j], P['even_a_ln_w'][j], P['even_a_ln_b'][j])
    y = jnp.concatenate([y_a, y_b], axis=-1) @ P['even_w_out'][j]
    return y, k, v, s_fin


def _hyena_filters(L, fw1, fb1, freq, fw2, fb2, fw3):
    t = jnp.linspace(0.0, 1.0, L, dtype=F32)[:, None]
    wpos = 2.0 * math.pi * jnp.arange(L, dtype=F32)[:, None] / L
    f = jnp.linspace(1e-4, FILT_BANDS - 1, FILT_BANDS, dtype=F32)[None, :]
    z = jnp.concatenate([t, jnp.cos(f * wpos), -jnp.sin(f * wpos)], axis=-1)
    fr = freq.astype(F32)
    hdn = jnp.sin(fr * (z @ fw1 + fb1))
    hdn = jnp.sin(fr * (hdn @ fw2 + fb2))
    filt = (hdn @ fw3).reshape(L, HYENA_ORDER, 2, C_WIDTH)
    deltas = jnp.abs(jnp.linspace(math.log(DECAY_TARGET) / SLOW_DECAY_PCT,
                                  math.log(DECAY_TARGET) / FAST_DECAY_PCT, C_WIDTH, dtype=F32))
    filt = filt * jnp.exp(-t * deltas[None, :])[:, None, None, :]
    return filt * lax.rsqrt(jnp.sum(filt * filt, axis=(0, 2), keepdims=True) + 1e-12)


def _bidir_long_conv(u, h_fwd, h_bwd):
    L = u.shape[1]
    f = jnp.concatenate([h_fwd, jnp.zeros((1, h_fwd.shape[1]), F32), jnp.flip(h_bwd[1:], 0)], axis=0)
    U = jnp.fft.rfft(u.astype(F32), n=2 * L, axis=1)
    Fh = jnp.fft.rfft(f, n=2 * L, axis=0)
    return jnp.fft.irfft(U * Fh[None], n=2 * L, axis=1)[:, :L]


def _hyena_mixer(h, w_in, conv, conv_b, fw1, fb1, freq, fw2, fb2, fw3, bias, w_out):
    L = h.shape[1]
    u = _short_conv3(h @ w_in, conv) + conv_b
    x1, x2, v = u[..., :C_WIDTH], u[..., C_WIDTH:2 * C_WIDTH], u[..., 2 * C_WIDTH:]
    filt = _hyena_filters(L, fw1, fb1, freq, fw2, fb2, fw3)
    z = v
    for n, gate in enumerate((x1, x2)):
        z = gate * (_bidir_long_conv(z, filt[:, n, 0], filt[:, n, 1]) + bias[n] * z)
    return z @ w_out


def _peer(h, wq, sub_keys, u_tab, v_tab):
    Bn, L, D = h.shape
    T = Bn * L
    x = h.reshape(T, D)
    q = (x @ wq).astype(F32).reshape(T, PEER_HEADS, 2, PEER_QDIM // 2)
    s = jnp.einsum('thcd,hcnd->thcn', q, sub_keys.astype(F32))
    s1, i1 = lax.top_k(s[:, :, 0], PEER_TOPK)
    s2, i2 = lax.top_k(s[:, :, 1], PEER_TOPK)
    cand = (s1[..., :, None] + s2[..., None, :]).reshape(T, PEER_HEADS, PEER_TOPK * PEER_TOPK)
    top, ci = lax.top_k(cand, PEER_TOPK)
    e = (jnp.take_along_axis(i1, ci // PEER_TOPK, axis=-1) * PEER_NKEYS
         + jnp.take_along_axis(i2, ci % PEER_TOPK, axis=-1))
    gate = jax.nn.softmax(top, axis=-1)
    nc = T // PEER_CHUNK

    def chunk(args):
        xc, ec, gc = args
        u = jnp.take(u_tab, ec, axis=0)
        act = jax.nn.gelu(jnp.einsum('chkd,cd->chk', u, xc), approximate=False)
        vv = jnp.take(v_tab, ec, axis=0)
        return jnp.einsum('chk,chkd->cd', gc * act, vv)

    out = lax.map(chunk, (x.reshape(nc, PEER_CHUNK, D),
                          e.reshape(nc, PEER_CHUNK, PEER_HEADS, PEER_TOPK),
                          gate.reshape(nc, PEER_CHUNK, PEER_HEADS, PEER_TOPK)))
    return out.reshape(Bn, L, D)


def _trunk(x, cond, P, latent, cache_k, cache_v, state_a):
    new_k, new_v, new_s = [], [], []
    for layer in range(DEPTH):
        j = layer // 2
        m = jax.nn.silu(cond.astype(F32)) @ P['mod_w'][layer] + P['mod_b'][layer]
        if m.ndim == 2:
            m = m[:, None, :]
        sh1, sc1, gt1, sh2, sc2, gt2 = jnp.split(m, 6, axis=-1)
        hm = _rmsnorm(x, P['norm1'][layer]) * (1.0 + sc1) + sh1
        if layer % 2 == 0:
            mix, k_c, v_c, s_c = _even_mixer(hm, P, j, latent, cache_k, cache_v, state_a)
            if not latent:
                new_k.append(k_c)
                new_v.append(v_c)
                new_s.append(s_c)
        else:
            mix = _hyena_mixer(hm, P['odd_w_in'][j], P['odd_c_conv'][j], P['odd_c_conv_b'][j],
                               P['odd_c_fw1'][j], P['odd_c_fb1'][j], P['odd_c_freq'][j],
                               P['odd_c_fw2'][j], P['odd_c_fb2'][j], P['odd_c_fw3'][j],
                               P['odd_c_bias'][j], P['odd_w_out'][j])
        x = x + gt1 * mix
        hm = _rmsnorm(x, P['norm2'][layer]) * (1.0 + sc2) + sh2
        x = x + gt2 * _peer(hm, P['peer_wq'][layer], P['peer_keys'][layer], P['peer_u'][layer], P['peer_v'][layer])
    return x, new_k, new_v, new_s


def setup_inputs(seed: int = 0) -> dict:
    key = jax.random.key(seed)
    ks = iter(jax.random.split(key, 48))

    def nrm(shape, scale):
        return jax.random.normal(next(ks), shape, F32) * scale

    centre = jnp.array([0.0, 1.0, 0.0], F32)[:, None]
    D = D_MODEL
    return {
        'x_prompt': nrm((BATCH, SEQ, D), 1.0),
        'x_sample': nrm((DEC_BATCH, DEC_SEQ, D), 1.0),
        'cache_b_k': nrm((DEC_BATCH, N_EVEN, PAST_LEN, B_KV_HEADS, B_HEAD_DIM), 1.0),
        'cache_b_v': nrm((DEC_BATCH, N_EVEN, PAST_LEN, B_KV_HEADS, B_HEAD_DIM), 1.0),
        'state_a': nrm((DEC_BATCH, N_EVEN, 2, A_HEADS, A_HEAD_DIM, A_HEAD_DIM), 0.3),
        'c': nrm((DEC_BATCH, D), 1.0),
        'c_ctx': nrm((D,), 1.0),
        'mod_w': nrm((DEPTH, D, 6 * D), 0.5 * D ** -0.5),
        'mod_b': nrm((DEPTH, 6 * D), 0.02),
        'norm1': 1.0 + nrm((DEPTH, D), 0.02),
        'norm2': 1.0 + nrm((DEPTH, D), 0.02),
        'even_w_in': nrm((N_EVEN, D, A_COLS + B_COLS), D ** -0.5),
        'even_a_conv': nrm((N_EVEN, 3, A_COLS), 0.2) + centre,
        'even_a_w0': nrm((N_EVEN, 2, A_WIDTH), 1.0) - 2.0,
        'even_a_wu': nrm((N_EVEN, 2, A_DECAY_RANK, A_WIDTH), 0.1),
        'even_a_a0': nrm((N_EVEN, 2, A_WIDTH), 0.5),
        'even_a_au': nrm((N_EVEN, 2, A_ICLR_RANK, A_WIDTH), 0.1),
        'even_a_gu': nrm((N_EVEN, A_GATE_RANK, A_WIDTH), A_GATE_RANK ** -0.5),
        'even_a_kk': 1.0 + nrm((N_EVEN, A_WIDTH), 0.1),
        'even_a_ka': 1.0 + nrm((N_EVEN, A_WIDTH), 0.1),
        'even_a_rk': nrm((N_EVEN, A_HEADS, A_HEAD_DIM), 0.1),
        'even_a_ln_w': 1.0 + nrm((N_EVEN, A_WIDTH), 0.02),
        'even_a_ln_b': nrm((N_EVEN, A_WIDTH), 0.02),
        'even_b_qnorm': 1.0 + nrm((N_EVEN, B_HEAD_DIM), 0.02),
        'even_b_knorm': 1.0 + nrm((N_EVEN, B_HEAD_DIM), 0.02),
        'even_w_out': nrm((N_EVEN, A_WIDTH + B_WIDTH, D), (A_WIDTH + B_WIDTH) ** -0.5),
        'odd_w_in': nrm((N_ODD, D, 3 * C_WIDTH), D ** -0.5),
        'odd_c_conv': nrm((N_ODD, 3, 3 * C_WIDTH), 0.2) + centre,
        'odd_c_conv_b': nrm((N_ODD, 3 * C_WIDTH), 0.02),
        'odd_c_fw1': nrm((N_ODD, FILT_EMB, FILT_HIDDEN), FILT_EMB ** -0.5),
        'odd_c_fb1': nrm((N_ODD, FILT_HIDDEN), 0.1),
        'odd_c_freq': 1.0 + nrm((N_ODD, FILT_HIDDEN), 0.1),
        'odd_c_fw2': nrm((N_ODD, FILT_HIDDEN, FILT_HIDDEN), FILT_HIDDEN ** -0.5),
        'odd_c_fb2': nrm((N_ODD, FILT_HIDDEN), 0.1),
        'odd_c_fw3': nrm((N_ODD, FILT_HIDDEN, HYENA_ORDER * 2 * C_WIDTH), FILT_HIDDEN ** -0.5),
        'odd_c_bias': nrm((N_ODD, HYENA_ORDER, C_WIDTH), 0.5),
        'odd_w_out': nrm((N_ODD, C_WIDTH, D), C_WIDTH ** -0.5),
        'peer_wq': nrm((DEPTH, D, PEER_HEADS * PEER_QDIM), D ** -0.5),
        'peer_keys': nrm((DEPTH, PEER_HEADS, 2, PEER_NKEYS, PEER_QDIM // 2), (PEER_QDIM // 2) ** -0.5),
        'peer_u': nrm((DEPTH, PEER_EXPERTS, D), D ** -0.5),
        'peer_v': nrm((DEPTH, PEER_EXPERTS, D), PEER_HEADS ** -0.5),
    }


def reference(x_prompt, x_sample, cache_b_k, cache_b_v, state_a, c, c_ctx, mod_w, mod_b, norm1, norm2,
              even_w_in, even_a_conv, even_a_w0, even_a_wu, even_a_a0, even_a_au, even_a_gu, even_a_kk,
              even_a_ka, even_a_rk, even_a_ln_w, even_a_ln_b, even_b_qnorm, even_b_knorm, even_w_out,
              odd_w_in, odd_c_conv, odd_c_conv_b, odd_c_fw1, odd_c_fb1, odd_c_freq, odd_c_fw2, odd_c_fb2,
              odd_c_fw3, odd_c_bias, odd_w_out, peer_wq, peer_keys, peer_u, peer_v):
    P = {
        'mod_w': mod_w, 'mod_b': mod_b, 'norm1': norm1, 'norm2': norm2,
        'even_w_in': even_w_in, 'even_a_conv': even_a_conv, 'even_a_w0': even_a_w0, 'even_a_wu': even_a_wu,
        'even_a_a0': even_a_a0, 'even_a_au': even_a_au, 'even_a_gu': even_a_gu, 'even_a_kk': even_a_kk,
        'even_a_ka': even_a_ka, 'even_a_rk': even_a_rk, 'even_a_ln_w': even_a_ln_w, 'even_a_ln_b': even_a_ln_b,
        'even_b_qnorm': even_b_qnorm, 'even_b_knorm': even_b_knorm, 'even_w_out': even_w_out,
        'odd_w_in': odd_w_in, 'odd_c_conv': odd_c_conv, 'odd_c_conv_b': odd_c_conv_b, 'odd_c_fw1': odd_c_fw1,
        'odd_c_fb1': odd_c_fb1, 'odd_c_freq': odd_c_freq, 'odd_c_fw2': odd_c_fw2, 'odd_c_fb2': odd_c_fb2,
        'odd_c_fw3': odd_c_fw3, 'odd_c_bias': odd_c_bias, 'odd_w_out': odd_w_out,
        'peer_wq': peer_wq, 'peer_keys': peer_keys, 'peer_u': peer_u, 'peer_v': peer_v,
    }
    y_prompt, nk, nv, ns = _trunk(x_prompt, c_ctx, P, False, None, None, None)
    y_sample, _, _, _ = _trunk(x_sample, c, P, True, cache_b_k, cache_b_v, state_a)
    new_cache_b_k = jnp.stack(nk, axis=1)
    new_cache_b_v = jnp.stack(nv, axis=1)
    new_state_a = jnp.stack(ns, axis=1)
    return (y_prompt, y_sample, new_cache_b_k, new_cache_b_v, new_state_a)
```

```python
import functools
import math

import numpy as np
import jax
import jax.numpy as jnp
from jax import lax
from jax.experimental import pallas as pl
from jax.experimental.pallas import tpu as pltpu

F32 = jnp.float32
BF16 = jnp.bfloat16
I32 = jnp.int32

NORM_EPS = 1e-6
A_HEAD_DIM = 64
A_GN_EPS = 64e-5
B_GROUP = 4
GRID_W = 64
ROPE_THETA = 10000.0
HYENA_ORDER = 2
DECAY_TARGET = 1e-2
FAST_DECAY_PCT = 0.3
SLOW_DECAY_PCT = 1.5
PEER_TOPK = 16
LANES = 128
VMEM_LIMIT = 56 * 1024 * 1024


def _cparams(sem):
    return pltpu.CompilerParams(dimension_semantics=sem, vmem_limit_bytes=VMEM_LIMIT)


def _row_tile(*lengths, cap=512):
    t = cap
    while any(n % t for n in lengths):
        t //= 2
    return t


def _mod_kernel(c_ref, w_ref, b_ref, o_ref):
    c = c_ref[...]
    s = (c * jax.nn.sigmoid(c)).astype(BF16)
    o_ref[0] = jnp.dot(s, w_ref[0].astype(BF16), preferred_element_type=F32) + b_ref[0]


def _mod_table(cond8, mod_w, mod_b):
    depth, d, n = mod_w.shape
    tn = _row_tile(n, cap=1024)
    return pl.pallas_call(
        _mod_kernel,
        grid=(depth, n // tn),
        in_specs=[pl.BlockSpec((8, d), lambda l, j: (0, 0)),
                  pl.BlockSpec((1, d, tn), lambda l, j: (l, 0, j)),
                  pl.BlockSpec((1, 1, tn), lambda l, j: (l, 0, j))],
        out_specs=pl.BlockSpec((1, 8, tn), lambda l, j: (l, 0, j)),
        out_shape=jax.ShapeDtypeStruct((depth, 8, n), F32),
        compiler_params=_cparams(("parallel", "parallel")),
        name="mod_table",
    )(cond8, mod_w, mod_b.reshape(depth, 1, n))


def _nm_matmul_kernel(seg_ref, x_ref, g_ref, sc_ref, sh_ref, w_ref, o_ref, h_ref, h_scr):
    del seg_ref

    @pl.when(pl.program_id(1) == 0)
    def _():
        x = x_ref[...]
        y = x * lax.rsqrt(jnp.mean(x * x, axis=-1, keepdims=True) + NORM_EPS) * g_ref[...]
        h = (y * (1.0 + sc_ref[0]) + sh_ref[0]).astype(BF16)
        h_scr[...] = h
        h_ref[...] = h

    o_ref[...] = jnp.dot(h_scr[...], w_ref[...], preferred_element_type=F32).astype(o_ref.dtype)


def _nm_matmul(x, g, mod3, sc_idx, sh_idx, seg, w_bf16, tm, out_dtype=F32):
    t, d = x.shape
    n = w_bf16.shape[1]
    tn = _row_tile(n, cap=512)
    grid_spec = pltpu.PrefetchScalarGridSpec(
        num_scalar_prefetch=1,
        grid=(t // tm, n // tn),
        in_specs=[pl.BlockSpec((tm, d), lambda i, j, s: (i, 0)),
                  pl.BlockSpec((1, d), lambda i, j, s: (0, 0)),
                  pl.BlockSpec((1, 1, d), lambda i, j, s: (s[i] * 6 + sc_idx, 0, 0)),
                  pl.BlockSpec((1, 1, d), lambda i, j, s: (s[i] * 6 + sh_idx, 0, 0)),
                  pl.BlockSpec((d, tn), lambda i, j, s: (0, j))],
        out_specs=[pl.BlockSpec((tm, tn), lambda i, j, s: (i, j)),
                   pl.BlockSpec((tm, d), lambda i, j, s: (i, 0))],
        scratch_shapes=[pltpu.VMEM((tm, d), BF16)])
    return pl.pallas_call(
        _nm_matmul_kernel,
        grid_spec=grid_spec,
        out_shape=[jax.ShapeDtypeStruct((t, n), out_dtype), jax.ShapeDtypeStruct((t, d), BF16)],
        compiler_params=_cparams(("parallel", "arbitrary")),
        name="norm_mod_matmul",
    )(seg, x, g.reshape(1, d), mod3, mod3, w_bf16)


def _res_matmul_kernel(seg_ref, a_ref, w_ref, r_ref, gt_ref, o_ref):
    del seg_ref
    mm = jnp.dot(a_ref[...], w_ref[...], preferred_element_type=F32)
    o_ref[...] = r_ref[...] + gt_ref[0] * mm


def _res_matmul(a_bf16, w_bf16, res, mod3, gt_idx, seg, tm):
    t, k = a_bf16.shape
    n = w_bf16.shape[1]
    tn = _row_tile(n, cap=512)
    grid_spec = pltpu.PrefetchScalarGridSpec(
        num_scalar_prefetch=1,
        grid=(t // tm, n // tn),
        in_specs=[pl.BlockSpec((tm, k), lambda i, j, s: (i, 0)),
                  pl.BlockSpec((k, tn), lambda i, j, s: (0, j)),
                  pl.BlockSpec((tm, tn), lambda i, j, s: (i, j)),
                  pl.BlockSpec((1, 1, tn), lambda i, j, s: (s[i] * 6 + gt_idx, 0, j))],
        out_specs=pl.BlockSpec((tm, tn), lambda i, j, s: (i, j)))
    return pl.pallas_call(
        _res_matmul_kernel,
        grid_spec=grid_spec,
        out_shape=jax.ShapeDtypeStruct((t, n), F32),
        compiler_params=_cparams(("parallel", "parallel")),
        name="res_matmul",
    )(seg, a_bf16, w_bf16, res, mod3)


def _scan_kernel(r_ref, kk_ref, w_ref, b_ref, kt_ref, v_ref, s0_ref, y_ref, sfin_ref, s_scr, *, tb_steps, nk):
    tb = pl.program_id(1)

    @pl.when(tb == 0)
    def _():
        s_scr[...] = s0_ref[...]

    def tree(parts):
        while len(parts) > 1:
            parts = [parts[i] + parts[i + 1] for i in range(0, len(parts), 2)]
        return parts[0]

    nacc = 4

    def step(t, carry):
        accs = [None] * nacc
        for k in range(nk):
            p = s_scr[k] * kk_ref[t, pl.ds(k, 1), :]
            accs[k % nacc] = p if accs[k % nacc] is None else accs[k % nacc] + p
        sa = tree(accs)
        v = v_ref[t]
        yacc = [None] * nacc
        for k in range(nk):
            s_new = (s_scr[k] * w_ref[t, pl.ds(k, 1), :] - sa * b_ref[t, pl.ds(k, 1), :]
                     + v * kt_ref[t, pl.ds(k, 1), :])
            s_scr[k] = s_new
            p = s_new * r_ref[t, pl.ds(k, 1), :]
            yacc[k % nacc] = p if yacc[k % nacc] is None else yacc[k % nacc] + p
        y_ref[t] = tree(yacc)
        return carry

    lax.fori_loop(0, tb_steps, step, 0)

    @pl.when(tb == pl.num_programs(1) - 1)
    def _():
        sfin_ref[...] = s_scr[...]


def _scan(r, kk, w, b, kt, v, s0):
    seq, nk, g = r.shape
    vs = v.shape[1]
    tb_steps = _row_tile(seq, cap=16)
    row = pl.BlockSpec((tb_steps, nk, LANES), lambda gi, ti: (ti, 0, gi))
    val = pl.BlockSpec((tb_steps, vs, LANES), lambda gi, ti: (ti, 0, gi))
    st = pl.BlockSpec((nk, vs, LANES), lambda gi, ti: (0, 0, gi))
    return pl.pallas_call(
        functools.partial(_scan_kernel, tb_steps=tb_steps, nk=nk),
        grid=(g // LANES, seq // tb_steps),
        in_specs=[row, row, row, row, row, val, st],
        out_specs=[val, st],
        out_shape=[jax.ShapeDtypeStruct((seq, vs, g), F32), jax.ShapeDtypeStruct((nk, vs, g), F32)],
        scratch_shapes=[pltpu.VMEM((nk, vs, LANES), F32)],
        compiler_params=_cparams(("parallel", "arbitrary")),
        name="rwkv7_scan",
    )(r, kk, w, b, kt, v, s0)


def _attn_kernel(q_ref, k_ref, v_ref, o_ref, *, dh):
    k = k_ref[0]
    v = v_ref[0]
    for g in range(B_GROUP):
        q = q_ref[0, :, g * dh:(g + 1) * dh]
        s = lax.dot_general(q, k, (((1,), (1,)), ((), ())), preferred_element_type=F32)
        m = jnp.max(s, axis=-1, keepdims=True)
        p = jnp.exp(s - m)
        den = jnp.sum(p, axis=-1, keepdims=True)
        o = jnp.dot(p.astype(BF16), v, preferred_element_type=F32) / den
        o_ref[0, :, g * dh:(g + 1) * dh] = o.astype(o_ref.dtype)


def _attend(q, k, v, dh):
    bn, lq, qw = q.shape
    lk = k.shape[1]
    kvh = k.shape[2] // dh
    gw = B_GROUP * dh
    tq = _row_tile(lq, cap=256)
    return pl.pallas_call(
        functools.partial(_attn_kernel, dh=dh),
        grid=(bn, kvh, lq // tq),
        in_specs=[pl.BlockSpec((1, tq, gw), lambda b, h, i: (b, i, h)),
                  pl.BlockSpec((1, lk, dh), lambda b, h, i: (b, 0, h)),
                  pl.BlockSpec((1, lk, dh), lambda b, h, i: (b, 0, h))],
        out_specs=pl.BlockSpec((1, tq, gw), lambda b, h, i: (b, i, h)),
        out_shape=jax.ShapeDtypeStruct((bn, lq, qw), BF16),
        compiler_params=_cparams(("parallel", "parallel", "parallel")),
        name="attention",
    )(q, k, v)


G_GROUP = 16


def _peer_kernel(seg_ref, xb_ref, i1_ref, i2_ref, g_ref, u_ref, v_ref, r_ref, gt_ref, o_ref,
                 gs_scr, gtmp_scr, acc_scr, *, tm, nkeys):
    del seg_ref
    e = pl.program_id(1)
    ne = pl.num_programs(1)
    per = u_ref.shape[0] // nkeys

    @pl.when(e == 0)
    def _():
        acc_scr[...] = jnp.zeros_like(acc_scr)
        iota = lax.broadcasted_iota(I32, (nkeys, nkeys), 0)

        def build(grp, carry):
            base = pl.multiple_of(grp * G_GROUP, G_GROUP)
            for tt in range(G_GROUP):
                t = base + tt
                a_t = jnp.where(iota == i1_ref[pl.ds(t, 1), :], 1.0, 0.0).astype(BF16)
                b_t = jnp.where(iota == i2_ref[pl.ds(t, 1), :], g_ref[pl.ds(t, 1), :], 0.0).astype(BF16)
                gtmp_scr[tt * nkeys:(tt + 1) * nkeys, :] = lax.dot_general(
                    a_t, b_t, (((1,), (1,)), ((), ())), preferred_element_type=F32)
            for n1 in range(nkeys):
                rows = gtmp_scr[pl.ds(n1, G_GROUP, stride=nkeys), :]
                gs_scr[pl.ds(pl.multiple_of(n1 * tm + base, G_GROUP), G_GROUP), :] = rows.astype(BF16)
            return carry

        lax.fori_loop(0, tm // G_GROUP, build, 0)

    h = lax.dot_general(xb_ref[...], u_ref[...], (((1,), (1,)), ((), ())), preferred_element_type=F32)
    for c in range(per):
        gt = gs_scr[pl.ds(pl.multiple_of((e * per + c) * tm, tm), tm), :].astype(F32)
        hc = h[:, c * nkeys:(c + 1) * nkeys]
        act = 0.5 * hc * (1.0 + lax.erf(hc * (1.0 / math.sqrt(2.0)))) * gt
        acc_scr[...] += jnp.dot(act.astype(BF16), v_ref[c * nkeys:(c + 1) * nkeys, :],
                                preferred_element_type=F32)

    @pl.when(e == ne - 1)
    def _():
        o_ref[...] = r_ref[...] + gt_ref[0] * acc_scr[...]


def _peer_experts(xb, i1, i2, gate, u_bf16, v_bf16, res, mod3, gt_idx, seg, tm, nkeys):
    t, d = xb.shape
    ne = u_bf16.shape[0]
    te = 2 * nkeys
    nj = i1.shape[1]
    grid_spec = pltpu.PrefetchScalarGridSpec(
        num_scalar_prefetch=1,
        grid=(t // tm, ne // te),
        in_specs=[pl.BlockSpec((tm, d), lambda i, e, s: (i, 0)),
                  pl.BlockSpec((tm, nj), lambda i, e, s: (i, 0)),
                  pl.BlockSpec((tm, nj), lambda i, e, s: (i, 0)),
                  pl.BlockSpec((tm, nj), lambda i, e, s: (i, 0)),
                  pl.BlockSpec((te, d), lambda i, e, s: (e, 0)),
                  pl.BlockSpec((te, d), lambda i, e, s: (e, 0)),
                  pl.BlockSpec((tm, d), lambda i, e, s: (i, 0)),
                  pl.BlockSpec((1, 1, d), lambda i, e, s: (s[i] * 6 + gt_idx, 0, 0))],
        out_specs=pl.BlockSpec((tm, d), lambda i, e, s: (i, 0)),
        scratch_shapes=[pltpu.VMEM((nkeys * tm, nkeys), BF16),
                        pltpu.VMEM((G_GROUP * nkeys, nkeys), F32),
                        pltpu.VMEM((tm, d), F32)])
    return pl.pallas_call(
        functools.partial(_peer_kernel, tm=tm, nkeys=nkeys),
        grid_spec=grid_spec,
        out_shape=jax.ShapeDtypeStruct((t, d), F32),
        compiler_params=_cparams(("parallel", "arbitrary")),
        name="peer_experts",
    )(seg, xb, i1, i2, gate, u_bf16, v_bf16, res, mod3)


def _short_conv3(x, w):
    xp = jnp.pad(x, ((0, 0), (1, 1), (0, 0)))
    return xp[:, :-2] * w[0] + xp[:, 1:-1] * w[1] + xp[:, 2:] * w[2]


def _axial_rope(x):
    seq = x.shape[1]
    n_rows = seq // GRID_W
    row = jnp.repeat(jnp.arange(n_rows, dtype=F32), GRID_W)
    col = jnp.tile(jnp.arange(GRID_W, dtype=F32), n_rows)
    half = x.shape[-1] // 2
    nf = half // 2
    inv = ROPE_THETA ** (-jnp.arange(nf, dtype=F32) / nf)
    shape = (1, seq) + (1,) * (x.ndim - 3) + (nf,)

    def rot(xh, pos):
        ang = (pos[:, None] * inv[None, :]).reshape(shape)
        cs, sn = jnp.cos(ang), jnp.sin(ang)
        x1, x2 = xh[..., :nf], xh[..., nf:]
        return jnp.concatenate([x1 * cs - x2 * sn, x1 * sn + x2 * cs], axis=-1)

    return jnp.concatenate([rot(x[..., :half], row), rot(x[..., half:], col)], axis=-1)


def _rms(x, g):
    return x * lax.rsqrt(jnp.mean(x * x, axis=-1, keepdims=True) + NORM_EPS) * g


def _bdot(a, b):
    return jnp.dot(a.astype(BF16), b.astype(BF16), preferred_element_type=F32)


def _rwkv_pre(za, P):
    bn, seq, _ = za.shape
    w_ = P['a_w0'].shape[-1]
    rw, ra = P['a_wu'].shape[1], P['a_au'].shape[1]
    heads = w_ // A_HEAD_DIM
    za = _short_conv3(za, P['a_conv'])
    r, k, v = za[..., :w_], za[..., w_:2 * w_], za[..., 2 * w_:3 * w_]
    o = 3 * w_
    wd = za[..., o:o + 2 * rw].reshape(bn, seq, 2, rw)
    o += 2 * rw
    ad = za[..., o:o + 2 * ra].reshape(bn, seq, 2, ra)
    o += 2 * ra
    gd = za[..., o:]
    lw = jnp.stack([_bdot(jnp.tanh(wd[:, :, d_]), P['a_wu'][d_]) for d_ in range(2)], axis=2)
    w_log = -jax.nn.softplus(-(P['a_w0'] + lw)) - 0.5
    w = jnp.exp(-jnp.exp(w_log))
    la = jnp.stack([_bdot(ad[:, :, d_], P['a_au'][d_]) for d_ in range(2)], axis=2)
    a = jax.nn.sigmoid(P['a_a0'] + la)
    g = _bdot(jax.nn.sigmoid(gd), P['a_gu'])
    hs = (bn, seq, heads, A_HEAD_DIM)
    kk = (k * P['a_kk']).reshape(hs)
    kk = (kk * lax.rsqrt(jnp.sum(kk * kk, axis=-1, keepdims=True) + 1e-12)).reshape(bn, seq, w_)
    kt = k[:, :, None, :] * (1.0 + (a - 1.0) * P['a_ka'])
    b = kk[:, :, None, :] * a
    bonus = (jnp.sum((r * k).reshape(hs) * P['a_rk'], axis=-1, keepdims=True) * v.reshape(hs)).reshape(bn, seq, w_)
    return r, kk, v, w, b, kt, g, bonus


def _to_chain_rows(x_dir, vh):
    two, bn, seq, heads, n = x_dir.shape
    y = jnp.transpose(x_dir, (2, 4, 0, 1, 3))
    y = jnp.broadcast_to(y[:, :, None], (seq, n, vh, two, bn, heads))
    return y.reshape(seq, n, vh * two * bn * heads)


def _rwkv_scan_pass(r, kk, v, w, b, kt, s0):
    bn, seq, w_ = r.shape
    heads = w_ // A_HEAD_DIM
    n = A_HEAD_DIM
    chains = 2 * bn * heads
    vh = max(1, LANES // chains)
    vs = n // vh
    hs = (bn, seq, heads, n)

    def both(t):
        t = t.reshape(hs)
        return jnp.stack([t, jnp.flip(t, 1)], axis=0)

    def rev1(t):
        t = t.reshape(bn, seq, 2, heads, n)
        return jnp.stack([t[:, :, 0], jnp.flip(t[:, :, 1], 1)], axis=0)

    rr = _to_chain_rows(both(r), vh)
    kr = _to_chain_rows(both(kk), vh)
    wr = _to_chain_rows(rev1(w), vh)
    br = _to_chain_rows(rev1(b), vh)
    ktr = _to_chain_rows(rev1(kt), vh)
    v2 = both(v).reshape(2, bn, seq, heads, vh, vs)
    vr = jnp.transpose(v2, (2, 5, 4, 0, 1, 3)).reshape(seq, vs, vh * chains)
    if s0 is None:
        s0r = jnp.zeros((n, vs, vh * chains), F32)
    else:
        s0r = jnp.transpose(s0.reshape(bn, 2, heads, vh, vs, n), (5, 4, 3, 1, 0, 2)).reshape(n, vs, vh * chains)
    y, sfin = _scan(rr, kr, wr, br, ktr, vr, s0r)
    y = jnp.transpose(y.reshape(seq, vs, vh, 2, bn, heads), (3, 4, 0, 5, 2, 1)).reshape(2, bn, seq, w_)
    y = y[0] + jnp.flip(y[1], 1)
    sfin = jnp.transpose(sfin.reshape(n, vs, vh, 2, bn, heads), (4, 3, 5, 2, 1, 0)).reshape(bn, 2, heads, n, n)
    return y, sfin


def _rwkv_post(y, bonus, g, P):
    bn, seq, w_ = y.shape
    hs = (bn, seq, w_ // A_HEAD_DIM, A_HEAD_DIM)
    y = y.reshape(hs)
    mu = jnp.mean(y, axis=-1, keepdims=True)
    var = jnp.mean(jnp.square(y - mu), axis=-1, keepdims=True)
    yn = ((y - mu) * lax.rsqrt(var + A_GN_EPS)).reshape(bn, seq, w_) * P['a_ln_w'] + P['a_ln_b']
    return (yn + bonus) * g


def _hyena_filters(seq, fw1, fb1, freq, fw2, fb2, fw3, c_width):
    emb = fw1.shape[0]
    bands = (emb - 1) // 2
    t = jnp.linspace(0.0, 1.0, seq, dtype=F32)[:, None]
    wpos = 2.0 * math.pi * jnp.arange(seq, dtype=F32)[:, None] / seq
    f = jnp.linspace(1e-4, bands - 1, bands, dtype=F32)[None, :]
    z = jnp.concatenate([t, jnp.cos(f * wpos), -jnp.sin(f * wpos)], axis=-1)
    hdn = jnp.sin(freq * (_bdot(z, fw1) + fb1))
    hdn = jnp.sin(freq * (_bdot(hdn, fw2) + fb2))
    filt = _bdot(hdn, fw3).reshape(seq, HYENA_ORDER, 2, c_width)
    deltas = jnp.abs(jnp.linspace(math.log(DECAY_TARGET) / SLOW_DECAY_PCT,
                                  math.log(DECAY_TARGET) / FAST_DECAY_PCT, c_width, dtype=F32))
    filt = filt * jnp.exp(-t * deltas[None, :])[:, None, None, :]
    return filt * lax.rsqrt(jnp.sum(filt * filt, axis=(0, 2), keepdims=True) + 1e-12)


def _bidir_long_conv(u, h_fwd, h_bwd):
    seq = u.shape[1]
    f = jnp.concatenate([h_fwd, jnp.zeros((1, h_fwd.shape[1]), F32), jnp.flip(h_bwd[1:], 0)], axis=0)
    uf = jnp.fft.rfft(u, n=2 * seq, axis=1)
    ff = jnp.fft.rfft(f, n=2 * seq, axis=0)
    return jnp.fft.irfft(uf * ff[None], n=2 * seq, axis=1)[:, :seq]


def _hyena_core(u_pre, P, c_width):
    seq = u_pre.shape[1]
    u = _short_conv3(u_pre, P['c_conv']) + P['c_conv_b']
    x1, x2, v = u[..., :c_width], u[..., c_width:2 * c_width], u[..., 2 * c_width:]
    filt = _hyena_filters(seq, P['c_fw1'], P['c_fb1'], P['c_freq'], P['c_fw2'], P['c_fb2'], P['c_fw3'], c_width)
    z = v
    for n_, gate in enumerate((x1, x2)):
        z = gate * (_bidir_long_conv(z, filt[:, n_, 0], filt[:, n_, 1]) + P['c_bias'][n_] * z)
    return z


def _peer_route(q, sub_keys):
    t = q.shape[0]
    heads, _, nkeys, dq = sub_keys.shape
    q = q.reshape(t, heads, 2, dq)
    s = jnp.einsum('thcd,hcnd->thcn', q.astype(BF16), sub_keys.astype(BF16), preferred_element_type=F32)
    s1, i1 = lax.top_k(s[:, :, 0], PEER_TOPK)
    s2, i2 = lax.top_k(s[:, :, 1], PEER_TOPK)
    cand = (s1[..., :, None] + s2[..., None, :]).reshape(t, heads, PEER_TOPK * PEER_TOPK)
    top, ci = lax.top_k(cand, PEER_TOPK)
    n1 = jnp.take_along_axis(i1, ci // PEER_TOPK, axis=-1)
    n2 = jnp.take_along_axis(i2, ci % PEER_TOPK, axis=-1)
    gate = jax.nn.softmax(top, axis=-1)
    hk = heads * PEER_TOPK
    return n1.reshape(t, hk).astype(I32), n2.reshape(t, hk).astype(I32), gate.reshape(t, hk)


def kernel(x_prompt, x_sample, cache_b_k, cache_b_v, state_a, c, c_ctx, mod_w, mod_b, norm1, norm2,
           even_w_in, even_a_conv, even_a_w0, even_a_wu, even_a_a0, even_a_au, even_a_gu, even_a_kk,
           even_a_ka, even_a_rk, even_a_ln_w, even_a_ln_b, even_b_qnorm, even_b_knorm, even_w_out,
           odd_w_in, odd_c_conv, odd_c_conv_b, odd_c_fw1, odd_c_fb1, odd_c_freq, odd_c_fw2, odd_c_fb2,
           odd_c_fw3, odd_c_bias, odd_w_out, peer_wq, peer_keys, peer_u, peer_v):
    bp, sp, d = x_prompt.shape
    bs, ss, _ = x_sample.shape
    depth = mod_w.shape[0]
    tp, ts = bp * sp, bs * ss
    t_all = tp + ts
    a_width = even_a_w0.shape[-1]
    a_cols = even_a_conv.shape[-1]
    dh = even_b_qnorm.shape[-1]
    b_width = d // 2
    kv_width = b_width // B_GROUP
    kvh = kv_width // dh
    c_width = odd_c_bias.shape[-1]
    nkeys = peer_keys.shape[3]
    assert bs + 1 <= 8 and nkeys == LANES and peer_keys.shape[1] * PEER_TOPK == LANES

    tm = _row_tile(tp, ss, cap=512)
    seg = jnp.concatenate([jnp.zeros((tp // tm,), I32),
                           1 + jnp.arange(ts // tm, dtype=I32) // (ss // tm)])
    tm_peer = _row_tile(tp, ss, cap=256)
    seg_peer = jnp.concatenate([jnp.zeros((tp // tm_peer,), I32),
                                1 + jnp.arange(ts // tm_peer, dtype=I32) // (ss // tm_peer)])

    cond8 = jnp.zeros((8, d), F32).at[0].set(c_ctx).at[1:1 + bs].set(c)
    mods = _mod_table(cond8, mod_w, mod_b)

    x = jnp.concatenate([x_prompt.reshape(tp, d), x_sample.reshape(ts, d)], axis=0)
    new_k, new_v, new_s = [], [], []
    for layer in range(depth):
        j = layer // 2
        mod3 = mods[layer].reshape(8 * 6, 1, d)
        if layer % 2 == 0:
            PA = dict(a_conv=even_a_conv[j], a_w0=even_a_w0[j], a_wu=even_a_wu[j], a_a0=even_a_a0[j],
                      a_au=even_a_au[j], a_gu=even_a_gu[j], a_kk=even_a_kk[j], a_ka=even_a_ka[j],
                      a_rk=even_a_rk[j], a_ln_w=even_a_ln_w[j], a_ln_b=even_a_ln_b[j])
            z, _ = _nm_matmul(x, norm1[layer], mod3, 1, 0, seg, even_w_in[j].astype(BF16), tm)
            outs = []
            for (z_p, bn, seq, latent) in ((z[:tp], bp, sp, False), (z[tp:], bs, ss, True)):
                z_p = z_p.reshape(bn, seq, -1)
                za, zb = z_p[..., :a_cols], z_p[..., a_cols:]
                q = _rms(zb[..., :b_width].reshape(bn, seq, kvh, B_GROUP, dh), even_b_qnorm[j])
                k = _rms(zb[..., b_width:b_width + kv_width].reshape(bn, seq, kvh, dh), even_b_knorm[j])
                v = zb[..., b_width + kv_width:].reshape(bn, seq, kvh, dh)
                if latent:
                    q = _axial_rope(q)
                    keys = jnp.concatenate([_axial_rope(k), cache_b_k[:, j]], axis=1)
                    vals = jnp.concatenate([v, cache_b_v[:, j]], axis=1)
                    s0 = state_a[:, j]
                else:
                    keys, vals = k, v
                    s0 = None
                    new_k.append(k)
                    new_v.append(v)
                lk = keys.shape[1]
                y_b = _attend((q * dh ** -0.5).astype(BF16).reshape(bn, seq, b_width),
                              keys.astype(BF16).reshape(bn, lk, kv_width),
                              vals.astype(BF16).reshape(bn, lk, kv_width), dh)
                r, kk, vv, w, b, kt, g, bonus = _rwkv_pre(za, PA)
                y, s_fin = _rwkv_scan_pass(r, kk, vv, w, b, kt, s0)
                if not latent:
                    new_s.append(s_fin)
                y_a = _rwkv_post(y, bonus, g, PA)
                outs.append(jnp.concatenate([y_a.astype(BF16), y_b], axis=-1).reshape(bn * seq, -1))
            mix_in = jnp.concatenate(outs, axis=0)
            x = _res_matmul(mix_in, even_w_out[j].astype(BF16), x, mod3, 2, seg, tm)
        else:
            PC = dict(c_conv=odd_c_conv[j], c_conv_b=odd_c_conv_b[j], c_fw1=odd_c_fw1[j], c_fb1=odd_c_fb1[j],
                      c_freq=odd_c_freq[j], c_fw2=odd_c_fw2[j], c_fb2=odd_c_fb2[j], c_fw3=odd_c_fw3[j],
                      c_bias=odd_c_bias[j])
            u_pre, _ = _nm_matmul(x, norm1[layer], mod3, 1, 0, seg, odd_w_in[j].astype(BF16), tm)
            zs = [_hyena_core(u_pre[:tp].reshape(bp, sp, -1), PC, c_width).reshape(tp, c_width),
                  _hyena_core(u_pre[tp:].reshape(bs, ss, -1), PC, c_width).reshape(ts, c_width)]
            x = _res_matmul(jnp.concatenate(zs, axis=0).astype(BF16), odd_w_out[j].astype(BF16), x, mod3, 2, seg, tm)
        q, hm = _nm_matmul(x, norm2[layer], mod3, 4, 3, seg, peer_wq[layer].astype(BF16), tm)
        i1, i2, gate = _peer_route(q, peer_keys[layer])
        x = _peer_experts(hm, i1, i2, gate, peer_u[layer].astype(BF16), peer_v[layer].astype(BF16),
                          x, mod3, 5, seg_peer, tm_peer, nkeys)

    y_prompt = x[:tp].reshape(bp, sp, d)
    y_sample = x[tp:].reshape(bs, ss, d)
    return (y_prompt, y_sample, jnp.stack(new_k, axis=1), jnp.stack(new_v, axis=1), jnp.stack(new_s, axis=1))
```

```python
import functools
import math

import numpy as np
import jax
import jax.numpy as jnp
from jax import lax
from jax.experimental import pallas as pl
from jax.experimental.pallas import tpu as pltpu

F32 = jnp.float32
BF16 = jnp.bfloat16
I32 = jnp.int32

NORM_EPS = 1e-6
A_HEAD_DIM = 64
A_GN_EPS = 64e-5
B_GROUP = 4
GRID_W = 64
ROPE_THETA = 10000.0
HYENA_ORDER = 2
DECAY_TARGET = 1e-2
FAST_DECAY_PCT = 0.3
SLOW_DECAY_PCT = 1.5
PEER_TOPK = 16
LANES = 128
VMEM_LIMIT = 56 * 1024 * 1024


def _cparams(sem):
    return pltpu.CompilerParams(dimension_semantics=sem, vmem_limit_bytes=VMEM_LIMIT)


def _row_tile(*lengths, cap=512):
    t = cap
    while any(n % t for n in lengths):
        t //= 2
    return t


def _mod_kernel(c_ref, w_ref, b_ref, o_ref):
    c = c_ref[...]
    s = (c * jax.nn.sigmoid(c)).astype(BF16)
    o_ref[0] = jnp.dot(s, w_ref[0].astype(BF16), preferred_element_type=F32) + b_ref[0]


def _mod_table(cond8, mod_w, mod_b):
    depth, d, n = mod_w.shape
    tn = _row_tile(n, cap=1024)
    return pl.pallas_call(
        _mod_kernel,
        grid=(depth, n // tn),
        in_specs=[pl.BlockSpec((8, d), lambda l, j: (0, 0)),
                  pl.BlockSpec((1, d, tn), lambda l, j: (l, 0, j)),
                  pl.BlockSpec((1, 1, tn), lambda l, j: (l, 0, j))],
        out_specs=pl.BlockSpec((1, 8, tn), lambda l, j: (l, 0, j)),
        out_shape=jax.ShapeDtypeStruct((depth, 8, n), F32),
        compiler_params=_cparams(("parallel", "parallel")),
        name="mod_table",
    )(cond8, mod_w, mod_b.reshape(depth, 1, n))


def _nm_matmul_kernel(seg_ref, x_ref, g_ref, sc_ref, sh_ref, w_ref, o_ref, h_ref, h_scr):
    del seg_ref

    @pl.when(pl.program_id(1) == 0)
    def _():
        x = x_ref[...]
        y = x * lax.rsqrt(jnp.mean(x * x, axis=-1, keepdims=True) + NORM_EPS) * g_ref[...]
        h = (y * (1.0 + sc_ref[0]) + sh_ref[0]).astype(BF16)
        h_scr[...] = h
        h_ref[...] = h

    o_ref[...] = jnp.dot(h_scr[...], w_ref[...], preferred_element_type=F32).astype(o_ref.dtype)


def _nm_matmul(x, g, mod3, sc_idx, sh_idx, seg, w_bf16, tm, out_dtype=F32):
    t, d = x.shape
    n = w_bf16.shape[1]
    tn = _row_tile(n, cap=512)
    grid_spec = pltpu.PrefetchScalarGridSpec(
        num_scalar_prefetch=1,
        grid=(t // tm, n // tn),
        in_specs=[pl.BlockSpec((tm, d), lambda i, j, s: (i, 0)),
                  pl.BlockSpec((1, d), lambda i, j, s: (0, 0)),
                  pl.BlockSpec((1, 1, d), lambda i, j, s: (s[i] * 6 + sc_idx, 0, 0)),
                  pl.BlockSpec((1, 1, d), lambda i, j, s: (s[i] * 6 + sh_idx, 0, 0)),
                  pl.BlockSpec((d, tn), lambda i, j, s: (0, j))],
        out_specs=[pl.BlockSpec((tm, tn), lambda i, j, s: (i, j)),
                   pl.BlockSpec((tm, d), lambda i, j, s: (i, 0))],
        scratch_shapes=[pltpu.VMEM((tm, d), BF16)])
    return pl.pallas_call(
        _nm_matmul_kernel,
        grid_spec=grid_spec,
        out_shape=[jax.ShapeDtypeStruct((t, n), out_dtype), jax.ShapeDtypeStruct((t, d), BF16)],
        compiler_params=_cparams(("parallel", "arbitrary")),
        name="norm_mod_matmul",
    )(seg, x, g.reshape(1, d), mod3, mod3, w_bf16)


def _res_matmul_kernel(seg_ref, a_ref, w_ref, r_ref, gt_ref, o_ref):
    del seg_ref
    mm = jnp.dot(a_ref[...], w_ref[...], preferred_element_type=F32)
    o_ref[...] = r_ref[...] + gt_ref[0] * mm


def _res_matmul(a_bf16, w_bf16, res, mod3, gt_idx, seg, tm):
    t, k = a_bf16.shape
    n = w_bf16.shape[1]
    tn = _row_tile(n, cap=512)
    grid_spec = pltpu.PrefetchScalarGridSpec(
        num_scalar_prefetch=1,
        grid=(t // tm, n // tn),
        in_specs=[pl.BlockSpec((tm, k), lambda i, j, s: (i, 0)),
                  pl.BlockSpec((k, tn), lambda i, j, s: (0, j)),
                  pl.BlockSpec((tm, tn), lambda i, j, s: (i, j)),
                  pl.BlockSpec((1, 1, tn), lambda i, j, s: (s[i] * 6 + gt_idx, 0, j))],
        out_specs=pl.BlockSpec((tm, tn), lambda i, j, s: (i, j)))
    return pl.pallas_call(
        _res_matmul_kernel,
        grid_spec=grid_spec,
        out_shape=jax.ShapeDtypeStruct((t, n), F32),
        compiler_params=_cparams(("parallel", "parallel")),
        name="res_matmul",
    )(seg, a_bf16, w_bf16, res, mod3)


def _scan_kernel(r_ref, kk_ref, w_ref, b_ref, kt_ref, v_ref, s0_ref, y_ref, sfin_ref, s_scr, *, tb_steps, nk):
    tb = pl.program_id(1)

    @pl.when(tb == 0)
    def _():
        s_scr[...] = s0_ref[...]

    def tree(parts):
        while len(parts) > 1:
            parts = [parts[i] + parts[i + 1] for i in range(0, len(parts), 2)]
        return parts[0]

    nacc = 4

    def step(t, carry):
        accs = [None] * nacc
        for k in range(nk):
            p = s_scr[k] * kk_ref[t, pl.ds(k, 1), :]
            accs[k % nacc] = p if accs[k % nacc] is None else accs[k % nacc] + p
        sa = tree(accs)
        v = v_ref[t]
        yacc = [None] * nacc
        for k in range(nk):
            s_new = (s_scr[k] * w_ref[t, pl.ds(k, 1), :] - sa * b_ref[t, pl.ds(k, 1), :]
                     + v * kt_ref[t, pl.ds(k, 1), :])
            s_scr[k] = s_new
            p = s_new * r_ref[t, pl.ds(k, 1), :]
            yacc[k % nacc] = p if yacc[k % nacc] is None else yacc[k % nacc] + p
        y_ref[t] = tree(yacc)
        return carry

    lax.fori_loop(0, tb_steps, step, 0)

    @pl.when(tb == pl.num_programs(1) - 1)
    def _():
        sfin_ref[...] = s_scr[...]


def _scan(r, kk, w, b, kt, v, s0):
    seq, nk, g = r.shape
    vs = v.shape[1]
    tb_steps = _row_tile(seq, cap=16)
    row = pl.BlockSpec((tb_steps, nk, LANES), lambda gi, ti: (ti, 0, gi))
    val = pl.BlockSpec((tb_steps, vs, LANES), lambda gi, ti: (ti, 0, gi))
    st = pl.BlockSpec((nk, vs, LANES), lambda gi, ti: (0, 0, gi))
    return pl.pallas_call(
        functools.partial(_scan_kernel, tb_steps=tb_steps, nk=nk),
        grid=(g // LANES, seq // tb_steps),
        in_specs=[row, row, row, row, row, val, st],
        out_specs=[val, st],
        out_shape=[jax.ShapeDtypeStruct((seq, vs, g), F32), jax.ShapeDtypeStruct((nk, vs, g), F32)],
        scratch_shapes=[pltpu.VMEM((nk, vs, LANES), F32)],
        compiler_params=_cparams(("parallel", "arbitrary")),
        name="rwkv7_scan",
    )(r, kk, w, b, kt, v, s0)


def _attn_kernel(q_ref, k_ref, v_ref, o_ref, *, dh):
    k = k_ref[0]
    v = v_ref[0]
    for g in range(B_GROUP):
        q = q_ref[0, :, g * dh:(g + 1) * dh]
        s = lax.dot_general(q, k, (((1,), (1,)), ((), ())), preferred_element_type=F32)
        m = jnp.max(s, axis=-1, keepdims=True)
        p = jnp.exp(s - m)
        den = jnp.sum(p, axis=-1, keepdims=True)
        o = jnp.dot(p.astype(BF16), v, preferred_element_type=F32) / den
        o_ref[0, :, g * dh:(g + 1) * dh] = o.astype(o_ref.dtype)


def _attend(q, k, v, dh):
    bn, lq, qw = q.shape
    lk = k.shape[1]
    kvh = k.shape[2] // dh
    gw = B_GROUP * dh
    tq = _row_tile(lq, cap=256)
    return pl.pallas_call(
        functools.partial(_attn_kernel, dh=dh),
        grid=(bn, kvh, lq // tq),
        in_specs=[pl.BlockSpec((1, tq, gw), lambda b, h, i: (b, i, h)),
                  pl.BlockSpec((1, lk, dh), lambda b, h, i: (b, 0, h)),
                  pl.BlockSpec((1, lk, dh), lambda b, h, i: (b, 0, h))],
        out_specs=pl.BlockSpec((1, tq, gw), lambda b, h, i: (b, i, h)),
        out_shape=jax.ShapeDtypeStruct((bn, lq, qw), BF16),
        compiler_params=_cparams(("parallel", "parallel", "parallel")),
        name="attention",
    )(q, k, v)


G_GROUP = 16


def _gate_matrix_kernel(i1_ref, i2_ref, g_ref, o_ref, gtmp_scr, *, nkeys):
    iota = lax.broadcasted_iota(I32, (nkeys, nkeys), 0)

    def build(grp, carry):
        base = pl.multiple_of(grp * G_GROUP, G_GROUP)
        def token(tt, c2):
            t = base + tt
            a_t = jnp.where(iota == i1_ref[pl.ds(t, 1), :], 1.0, 0.0).astype(BF16)
            b_t = jnp.where(iota == i2_ref[pl.ds(t, 1), :], g_ref[pl.ds(t, 1), :], 0.0).astype(BF16)
            gtmp_scr[pl.ds(pl.multiple_of(tt * nkeys, nkeys), nkeys), :] = lax.dot_general(
                a_t, b_t, (((1,), (1,)), ((), ())), preferred_element_type=F32)
            return c2

        lax.fori_loop(0, G_GROUP, token, 0, unroll=8)
        for n1 in range(nkeys):
            rows = gtmp_scr[pl.ds(n1, G_GROUP, stride=nkeys), :]
            o_ref[pl.ds(base, G_GROUP), n1 * nkeys:(n1 + 1) * nkeys] = rows.astype(BF16)
        return carry

    lax.fori_loop(0, o_ref.shape[0] // G_GROUP, build, 0)


def _gate_matrix(i1, i2, gate, nkeys):
    t, nj = i1.shape
    tb = LANES
    sel = pl.BlockSpec((tb, nj), lambda i: (i, 0))
    return pl.pallas_call(
        functools.partial(_gate_matrix_kernel, nkeys=nkeys),
        grid=(t // tb,),
        in_specs=[sel, sel, sel],
        out_specs=pl.BlockSpec((tb, nkeys * nkeys), lambda i: (i, 0)),
        out_shape=jax.ShapeDtypeStruct((t, nkeys * nkeys), BF16),
        scratch_shapes=[pltpu.VMEM((G_GROUP * nkeys, nkeys), F32)],
        compiler_params=_cparams(("parallel",)),
        name="peer_gate_matrix",
    )(i1, i2, gate)


def _peer_kernel(seg_ref, xb_ref, gm_ref, u_ref, v_ref, r_ref, gt_ref, o_ref, acc_scr):
    del seg_ref
    e = pl.program_id(1)

    @pl.when(e == 0)
    def _():
        acc_scr[...] = jnp.zeros_like(acc_scr)

    h = lax.dot_general(xb_ref[...], u_ref[...], (((1,), (1,)), ((), ())), preferred_element_type=F32)
    act = 0.5 * h * (1.0 + lax.erf(h * (1.0 / math.sqrt(2.0)))) * gm_ref[...].astype(F32)
    acc_scr[...] += jnp.dot(act.astype(BF16), v_ref[...], preferred_element_type=F32)

    @pl.when(e == pl.num_programs(1) - 1)
    def _():
        o_ref[...] = r_ref[...] + gt_ref[0] * acc_scr[...]


def _peer_experts(xb, gmat, u_bf16, v_bf16, res, mod3, gt_idx, seg, tm):
    t, d = xb.shape
    ne = u_bf16.shape[0]
    te = 512
    grid_spec = pltpu.PrefetchScalarGridSpec(
        num_scalar_prefetch=1,
        grid=(t // tm, ne // te),
        in_specs=[pl.BlockSpec((tm, d), lambda i, e, s: (i, 0)),
                  pl.BlockSpec((tm, te), lambda i, e, s: (i, e)),
                  pl.BlockSpec((te, d), lambda i, e, s: (e, 0)),
                  pl.BlockSpec((te, d), lambda i, e, s: (e, 0)),
                  pl.BlockSpec((tm, d), lambda i, e, s: (i, 0)),
                  pl.BlockSpec((1, 1, d), lambda i, e, s: (s[i] * 6 + gt_idx, 0, 0))],
        out_specs=pl.BlockSpec((tm, d), lambda i, e, s: (i, 0)),
        scratch_shapes=[pltpu.VMEM((tm, d), F32)])
    return pl.pallas_call(
        _peer_kernel,
        grid_spec=grid_spec,
        out_shape=jax.ShapeDtypeStruct((t, d), F32),
        compiler_params=_cparams(("parallel", "arbitrary")),
        name="peer_experts",
    )(seg, xb, gmat, u_bf16, v_bf16, res, mod3)


def _topk_rows(vals, payload, rows_out):
    big = jnp.int32(2 ** 30)
    top_v = jnp.zeros(rows_out.shape, F32)
    top_p = jnp.zeros(rows_out.shape, I32)
    for it in range(PEER_TOPK):
        m = jnp.max(vals, axis=0, keepdims=True)
        sel = jnp.min(jnp.where(vals == m, payload, big), axis=0, keepdims=True)
        top_v = jnp.where(rows_out == it, m, top_v)
        top_p = jnp.where(rows_out == it, sel, top_p)
        vals = jnp.where(payload == sel, -jnp.inf, vals)
    return top_v, top_p


def _gather_rows(table, sel):
    out = jnp.zeros(sel.shape, table.dtype)
    for i in range(PEER_TOPK):
        out = jnp.where(sel == i, table[i:i + 1, :], out)
    return out


def _route_kernel(q_ref, keys_ref, i1_ref, i2_ref, g_ref, n1_scr, n2_scr, gate_scr, *, heads, nkeys, dq):
    tt = q_ref.shape[0]
    k = PEER_TOPK
    half = k // 2
    n_iota = lax.broadcasted_iota(I32, (nkeys, tt), 0)
    rows_out = lax.broadcasted_iota(I32, (k, tt), 0)
    r = lax.broadcasted_iota(I32, (half * k + half, tt), 0)
    pair_id = jnp.where(r < half * k, r, (r - half * k + half) * k)

    def head(h, carry):
        tops = []
        for c in range(2):
            col = pl.multiple_of((h * 2 + c) * dq, dq)
            qhc = q_ref[:, pl.ds(col, dq)].astype(BF16)
            khc = keys_ref[h, c].astype(BF16)
            s = lax.dot_general(khc, qhc, (((1,), (1,)), ((), ())), preferred_element_type=F32)
            tops.append(_topk_rows(s, n_iota, rows_out))
        (s1, i1), (s2, i2) = tops
        cand = jnp.concatenate([s1[i:i + 1, :] + s2 for i in range(half)] + [s1[half:, :] + s2[0:1, :]], axis=0)
        top, ci = _topk_rows(cand, pair_id, rows_out)
        n1 = _gather_rows(i1, lax.shift_right_logical(ci, 4))
        n2 = _gather_rows(i2, lax.bitwise_and(ci, k - 1))
        ex = jnp.exp(top - top[0:1, :])
        gate = ex / jnp.sum(ex, axis=0, keepdims=True)
        row = pl.multiple_of(h * k, k)
        n1_scr[pl.ds(row, k), :] = n1
        n2_scr[pl.ds(row, k), :] = n2
        gate_scr[pl.ds(row, k), :] = gate
        return carry

    lax.fori_loop(0, heads, head, 0)
    i1_ref[...] = n1_scr[...].T
    i2_ref[...] = n2_scr[...].T
    g_ref[...] = gate_scr[...].T


def _peer_route(q, sub_keys):
    t, qw = q.shape
    heads, _, nkeys, dq = sub_keys.shape
    hk = heads * PEER_TOPK
    tt = LANES
    out = pl.BlockSpec((tt, hk), lambda i: (i, 0))
    return pl.pallas_call(
        functools.partial(_route_kernel, heads=heads, nkeys=nkeys, dq=dq),
        grid=(t // tt,),
        in_specs=[pl.BlockSpec((tt, qw), lambda i: (i, 0)),
                  pl.BlockSpec((heads, 2, nkeys, dq), lambda i: (0, 0, 0, 0))],
        out_specs=[out, out, out],
        out_shape=[jax.ShapeDtypeStruct((t, hk), I32), jax.ShapeDtypeStruct((t, hk), I32),
                   jax.ShapeDtypeStruct((t, hk), F32)],
        scratch_shapes=[pltpu.VMEM((hk, tt), I32), pltpu.VMEM((hk, tt), I32), pltpu.VMEM((hk, tt), F32)],
        compiler_params=_cparams(("parallel",)),
        name="peer_route",
    )(q, sub_keys)


def _short_conv3(x, w):
    xp = jnp.pad(x, ((0, 0), (1, 1), (0, 0)))
    return xp[:, :-2] * w[0] + xp[:, 1:-1] * w[1] + xp[:, 2:] * w[2]


def _axial_rope(x):
    seq = x.shape[1]
    n_rows = seq // GRID_W
    row = jnp.repeat(jnp.arange(n_rows, dtype=F32), GRID_W)
    col = jnp.tile(jnp.arange(GRID_W, dtype=F32), n_rows)
    half = x.shape[-1] // 2
    nf = half // 2
    inv = ROPE_THETA ** (-jnp.arange(nf, dtype=F32) / nf)
    shape = (1, seq) + (1,) * (x.ndim - 3) + (nf,)

    def rot(xh, pos):
        ang = (pos[:, None] * inv[None, :]).reshape(shape)
        cs, sn = jnp.cos(ang), jnp.sin(ang)
        x1, x2 = xh[..., :nf], xh[..., nf:]
        return jnp.concatenate([x1 * cs - x2 * sn, x1 * sn + x2 * cs], axis=-1)

    return jnp.concatenate([rot(x[..., :half], row), rot(x[..., half:], col)], axis=-1)


def _rms(x, g):
    return x * lax.rsqrt(jnp.mean(x * x, axis=-1, keepdims=True) + NORM_EPS) * g


def _bdot(a, b):
    return jnp.dot(a.astype(BF16), b.astype(BF16), preferred_element_type=F32)


def _rwkv_pre(za, P):
    bn, seq, _ = za.shape
    w_ = P['a_w0'].shape[-1]
    rw, ra = P['a_wu'].shape[1], P['a_au'].shape[1]
    heads = w_ // A_HEAD_DIM
    za = _short_conv3(za, P['a_conv'])
    r, k, v = za[..., :w_], za[..., w_:2 * w_], za[..., 2 * w_:3 * w_]
    o = 3 * w_
    wd = za[..., o:o + 2 * rw].reshape(bn, seq, 2, rw)
    o += 2 * rw
    ad = za[..., o:o + 2 * ra].reshape(bn, seq, 2, ra)
    o += 2 * ra
    gd = za[..., o:]
    lw = jnp.stack([_bdot(jnp.tanh(wd[:, :, d_]), P['a_wu'][d_]) for d_ in range(2)], axis=2)
    w_log = -jax.nn.softplus(-(P['a_w0'] + lw)) - 0.5
    w = jnp.exp(-jnp.exp(w_log))
    la = jnp.stack([_bdot(ad[:, :, d_], P['a_au'][d_]) for d_ in range(2)], axis=2)
    a = jax.nn.sigmoid(P['a_a0'] + la)
    g = _bdot(jax.nn.sigmoid(gd), P['a_gu'])
    hs = (bn, seq, heads, A_HEAD_DIM)
    kk = (k * P['a_kk']).reshape(hs)
    kk = (kk * lax.rsqrt(jnp.sum(kk * kk, axis=-1, keepdims=True) + 1e-12)).reshape(bn, seq, w_)
    kt = k[:, :, None, :] * (1.0 + (a - 1.0) * P['a_ka'])
    b = kk[:, :, None, :] * a
    bonus = (jnp.sum((r * k).reshape(hs) * P['a_rk'], axis=-1, keepdims=True) * v.reshape(hs)).reshape(bn, seq, w_)
    return r, kk, v, w, b, kt, g, bonus


def _to_chain_rows(x_dir, vh):
    two, bn, seq, heads, n = x_dir.shape
    y = jnp.transpose(x_dir, (2, 4, 0, 1, 3))
    y = jnp.broadcast_to(y[:, :, None], (seq, n, vh, two, bn, heads))
    return y.reshape(seq, n, vh * two * bn * heads)


def _rwkv_scan_pass(r, kk, v, w, b, kt, s0):
    bn, seq, w_ = r.shape
    heads = w_ // A_HEAD_DIM
    n = A_HEAD_DIM
    chains = 2 * bn * heads
    vh = max(1, LANES // chains)
    vs = n // vh
    hs = (bn, seq, heads, n)

    def both(t):
        t = t.reshape(hs)
        return jnp.stack([t, jnp.flip(t, 1)], axis=0)

    def rev1(t):
        t = t.reshape(bn, seq, 2, heads, n)
        return jnp.stack([t[:, :, 0], jnp.flip(t[:, :, 1], 1)], axis=0)

    rr = _to_chain_rows(both(r), vh)
    kr = _to_chain_rows(both(kk), vh)
    wr = _to_chain_rows(rev1(w), vh)
    br = _to_chain_rows(rev1(b), vh)
    ktr = _to_chain_rows(rev1(kt), vh)
    v2 = both(v).reshape(2, bn, seq, heads, vh, vs)
    vr = jnp.transpose(v2, (2, 5, 4, 0, 1, 3)).reshape(seq, vs, vh * chains)
    if s0 is None:
        s0r = jnp.zeros((n, vs, vh * chains), F32)
    else:
        s0r = jnp.transpose(s0.reshape(bn, 2, heads, vh, vs, n), (5, 4, 3, 1, 0, 2)).reshape(n, vs, vh * chains)
    y, sfin = _scan(rr, kr, wr, br, ktr, vr, s0r)
    y = jnp.transpose(y.reshape(seq, vs, vh, 2, bn, heads), (3, 4, 0, 5, 2, 1)).reshape(2, bn, seq, w_)
    y = y[0] + jnp.flip(y[1], 1)
    sfin = jnp.transpose(sfin.reshape(n, vs, vh, 2, bn, heads), (4, 3, 5, 2, 1, 0)).reshape(bn, 2, heads, n, n)
    return y, sfin


def _rwkv_post(y, bonus, g, P):
    bn, seq, w_ = y.shape
    hs = (bn, seq, w_ // A_HEAD_DIM, A_HEAD_DIM)
    y = y.reshape(hs)
    mu = jnp.mean(y, axis=-1, keepdims=True)
    var = jnp.mean(jnp.square(y - mu), axis=-1, keepdims=True)
    yn = ((y - mu) * lax.rsqrt(var + A_GN_EPS)).reshape(bn, seq, w_) * P['a_ln_w'] + P['a_ln_b']
    return (yn + bonus) * g


def _hyena_filters(seq, fw1, fb1, freq, fw2, fb2, fw3, c_width):
    emb = fw1.shape[0]
    bands = (emb - 1) // 2
    t = jnp.linspace(0.0, 1.0, seq, dtype=F32)[:, None]
    wpos = 2.0 * math.pi * jnp.arange(seq, dtype=F32)[:, None] / seq
    f = jnp.linspace(1e-4, bands - 1, bands, dtype=F32)[None, :]
    z = jnp.concatenate([t, jnp.cos(f * wpos), -jnp.sin(f * wpos)], axis=-1)
    hdn = jnp.sin(freq * (_bdot(z, fw1) + fb1))
    hdn = jnp.sin(freq * (_bdot(hdn, fw2) + fb2))
    filt = _bdot(hdn, fw3).reshape(seq, HYENA_ORDER, 2, c_width)
    deltas = jnp.abs(jnp.linspace(math.log(DECAY_TARGET) / SLOW_DECAY_PCT,
                                  math.log(DECAY_TARGET) / FAST_DECAY_PCT, c_width, dtype=F32))
    filt = filt * jnp.exp(-t * deltas[None, :])[:, None, None, :]
    return filt * lax.rsqrt(jnp.sum(filt * filt, axis=(0, 2), keepdims=True) + 1e-12)


def _bidir_long_conv(u, h_fwd, h_bwd):
    seq = u.shape[1]
    f = jnp.concatenate([h_fwd, jnp.zeros((1, h_fwd.shape[1]), F32), jnp.flip(h_bwd[1:], 0)], axis=0)
    uf = jnp.fft.rfft(u, n=2 * seq, axis=1)
    ff = jnp.fft.rfft(f, n=2 * seq, axis=0)
    return jnp.fft.irfft(uf * ff[None], n=2 * seq, axis=1)[:, :seq]


def _hyena_core(u_pre, P, c_width):
    seq = u_pre.shape[1]
    u = _short_conv3(u_pre, P['c_conv']) + P['c_conv_b']
    x1, x2, v = u[..., :c_width], u[..., c_width:2 * c_width], u[..., 2 * c_width:]
    filt = _hyena_filters(seq, P['c_fw1'], P['c_fb1'], P['c_freq'], P['c_fw2'], P['c_fb2'], P['c_fw3'], c_width)
    z = v
    for n_, gate in enumerate((x1, x2)):
        z = gate * (_bidir_long_conv(z, filt[:, n_, 0], filt[:, n_, 1]) + P['c_bias'][n_] * z)
    return z


def kernel(x_prompt, x_sample, cache_b_k, cache_b_v, state_a, c, c_ctx, mod_w, mod_b, norm1, norm2,
           even_w_in, even_a_conv, even_a_w0, even_a_wu, even_a_a0, even_a_au, even_a_gu, even_a_kk,
           even_a_ka, even_a_rk, even_a_ln_w, even_a_ln_b, even_b_qnorm, even_b_knorm, even_w_out,
           odd_w_in, odd_c_conv, odd_c_conv_b, odd_c_fw1, odd_c_fb1, odd_c_freq, odd_c_fw2, odd_c_fb2,
           odd_c_fw3, odd_c_bias, odd_w_out, peer_wq, peer_keys, peer_u, peer_v):
    bp, sp, d = x_prompt.shape
    bs, ss, _ = x_sample.shape
    depth = mod_w.shape[0]
    tp, ts = bp * sp, bs * ss
    t_all = tp + ts
    a_width = even_a_w0.shape[-1]
    a_cols = even_a_conv.shape[-1]
    dh = even_b_qnorm.shape[-1]
    b_width = d // 2
    kv_width = b_width // B_GROUP
    kvh = kv_width // dh
    c_width = odd_c_bias.shape[-1]
    nkeys = peer_keys.shape[3]
    assert bs + 1 <= 8 and nkeys == LANES and peer_keys.shape[1] * PEER_TOPK == LANES

    tm = _row_tile(tp, ss, cap=512)
    seg = jnp.concatenate([jnp.zeros((tp // tm,), I32),
                           1 + jnp.arange(ts // tm, dtype=I32) // (ss // tm)])

    cond8 = jnp.zeros((8, d), F32).at[0].set(c_ctx).at[1:1 + bs].set(c)
    mods = _mod_table(cond8, mod_w, mod_b)

    x = jnp.concatenate([x_prompt.reshape(tp, d), x_sample.reshape(ts, d)], axis=0)
    new_k, new_v, new_s = [], [], []
    for layer in range(depth):
        j = layer // 2
        mod3 = mods[layer].reshape(8 * 6, 1, d)
        if layer % 2 == 0:
            PA = dict(a_conv=even_a_conv[j], a_w0=even_a_w0[j], a_wu=even_a_wu[j], a_a0=even_a_a0[j],
                      a_au=even_a_au[j], a_gu=even_a_gu[j], a_kk=even_a_kk[j], a_ka=even_a_ka[j],
                      a_rk=even_a_rk[j], a_ln_w=even_a_ln_w[j], a_ln_b=even_a_ln_b[j])
            z, _ = _nm_matmul(x, norm1[layer], mod3, 1, 0, seg, even_w_in[j].astype(BF16), tm)
            outs = []
            for (z_p, bn, seq, latent) in ((z[:tp], bp, sp, False), (z[tp:], bs, ss, True)):
                z_p = z_p.reshape(bn, seq, -1)
                za, zb = z_p[..., :a_cols], z_p[..., a_cols:]
                q = _rms(zb[..., :b_width].reshape(bn, seq, kvh, B_GROUP, dh), even_b_qnorm[j])
                k = _rms(zb[..., b_width:b_width + kv_width].reshape(bn, seq, kvh, dh), even_b_knorm[j])
                v = zb[..., b_width + kv_width:].reshape(bn, seq, kvh, dh)
                if latent:
                    q = _axial_rope(q)
                    keys = jnp.concatenate([_axial_rope(k), cache_b_k[:, j]], axis=1)
                    vals = jnp.concatenate([v, cache_b_v[:, j]], axis=1)
                    s0 = state_a[:, j]
                else:
                    keys, vals = k, v
                    s0 = None
                    new_k.append(k)
                    new_v.append(v)
                lk = keys.shape[1]
                y_b = _attend((q * dh ** -0.5).astype(BF16).reshape(bn, seq, b_width),
                              keys.astype(BF16).reshape(bn, lk, kv_width),
                              vals.astype(BF16).reshape(bn, lk, kv_width), dh)
                r, kk, vv, w, b, kt, g, bonus = _rwkv_pre(za, PA)
                y, s_fin = _rwkv_scan_pass(r, kk, vv, w, b, kt, s0)
                if not latent:
                    new_s.append(s_fin)
                y_a = _rwkv_post(y, bonus, g, PA)
                outs.append(jnp.concatenate([y_a.astype(BF16), y_b], axis=-1).reshape(bn * seq, -1))
            mix_in = jnp.concatenate(outs, axis=0)
            x = _res_matmul(mix_in, even_w_out[j].astype(BF16), x, mod3, 2, seg, tm)
        else:
            PC = dict(c_conv=odd_c_conv[j], c_conv_b=odd_c_conv_b[j], c_fw1=odd_c_fw1[j], c_fb1=odd_c_fb1[j],
                      c_freq=odd_c_freq[j], c_fw2=odd_c_fw2[j], c_fb2=odd_c_fb2[j], c_fw3=odd_c_fw3[j],
                      c_bias=odd_c_bias[j])
            u_pre, _ = _nm_matmul(x, norm1[layer], mod3, 1, 0, seg, odd_w_in[j].astype(BF16), tm)
            zs = [_hyena_core(u_pre[:tp].reshape(bp, sp, -1), PC, c_width).reshape(tp, c_width),
                  _hyena_core(u_pre[tp:].reshape(bs, ss, -1), PC, c_width).reshape(ts, c_width)]
            x = _res_matmul(jnp.concatenate(zs, axis=0).astype(BF16), odd_w_out[j].astype(BF16), x, mod3, 2, seg, tm)
        q, hm = _nm_matmul(x, norm2[layer], mod3, 4, 3, seg, peer_wq[layer].astype(BF16), tm)
        i1, i2, gate = _peer_route(q, peer_keys[layer])
        gmat = _gate_matrix(i1, i2, gate, nkeys)
        x = _peer_experts(hm, gmat, peer_u[layer].astype(BF16), peer_v[layer].astype(BF16),
                          x, mod3, 5, seg, tm)

    y_prompt = x[:tp].reshape(bp, sp, d)
    y_sample = x[tp:].reshape(bs, ss, d)
    return (y_prompt, y_sample, jnp.stack(new_k, axis=1), jnp.stack(new_v, axis=1), jnp.stack(new_s, axis=1))
```

```python
import functools
import math

import numpy as np
import jax
import jax.numpy as jnp
from jax import lax
from jax.experimental import pallas as pl
from jax.experimental.pallas import tpu as pltpu

F32 = jnp.float32
BF16 = jnp.bfloat16
I32 = jnp.int32

NORM_EPS = 1e-6
A_HEAD_DIM = 64
A_GN_EPS = 64e-5
B_GROUP = 4
GRID_W = 64
ROPE_THETA = 10000.0
HYENA_ORDER = 2
DECAY_TARGET = 1e-2
FAST_DECAY_PCT = 0.3
SLOW_DECAY_PCT = 1.5
PEER_TOPK = 16
LANES = 128
VMEM_LIMIT = 56 * 1024 * 1024


def _cparams(sem):
    return pltpu.CompilerParams(dimension_semantics=sem, vmem_limit_bytes=VMEM_LIMIT)


def _row_tile(*lengths, cap=512):
    t = cap
    while any(n % t for n in lengths):
        t //= 2
    return t


def _col_tile(n, cap):
    return max(t for t in range(LANES, cap + 1, LANES) if n % t == 0)


def _mod_kernel(c_ref, w_ref, b_ref, o_ref):
    c = c_ref[...]
    s = (c * jax.nn.sigmoid(c)).astype(BF16)
    o_ref[0] = jnp.dot(s, w_ref[0].astype(BF16), preferred_element_type=F32) + b_ref[0]


def _mod_table(cond8, mod_w, mod_b):
    depth, d, n = mod_w.shape
    tn = _row_tile(n, cap=1024)
    return pl.pallas_call(
        _mod_kernel,
        grid=(depth, n // tn),
        in_specs=[pl.BlockSpec((8, d), lambda l, j: (0, 0)),
                  pl.BlockSpec((1, d, tn), lambda l, j: (l, 0, j)),
                  pl.BlockSpec((1, 1, tn), lambda l, j: (l, 0, j))],
        out_specs=pl.BlockSpec((1, 8, tn), lambda l, j: (l, 0, j)),
        out_shape=jax.ShapeDtypeStruct((depth, 8, n), F32),
        compiler_params=_cparams(("parallel", "parallel")),
        name="mod_table",
    )(cond8, mod_w, mod_b.reshape(depth, 1, n))


def _nm_matmul_kernel(seg_ref, x_ref, g_ref, sc_ref, sh_ref, w_ref, o_ref, h_ref, h_scr):
    del seg_ref

    @pl.when(pl.program_id(1) == 0)
    def _():
        x = x_ref[...]
        y = x * lax.rsqrt(jnp.mean(x * x, axis=-1, keepdims=True) + NORM_EPS) * g_ref[...]
        h = (y * (1.0 + sc_ref[0]) + sh_ref[0]).astype(BF16)
        h_scr[...] = h
        h_ref[...] = h

    o_ref[...] = jnp.dot(h_scr[...], w_ref[...], preferred_element_type=F32).astype(o_ref.dtype)


def _nm_matmul(x, g, mod3, sc_idx, sh_idx, seg, w_bf16, tm, out_dtype=F32):
    t, d = x.shape
    n = w_bf16.shape[1]
    tn = _col_tile(n, 1664)
    grid_spec = pltpu.PrefetchScalarGridSpec(
        num_scalar_prefetch=1,
        grid=(t // tm, n // tn),
        in_specs=[pl.BlockSpec((tm, d), lambda i, j, s: (i, 0)),
                  pl.BlockSpec((1, d), lambda i, j, s: (0, 0)),
                  pl.BlockSpec((1, 1, d), lambda i, j, s: (s[i] * 6 + sc_idx, 0, 0)),
                  pl.BlockSpec((1, 1, d), lambda i, j, s: (s[i] * 6 + sh_idx, 0, 0)),
                  pl.BlockSpec((d, tn), lambda i, j, s: (0, j))],
        out_specs=[pl.BlockSpec((tm, tn), lambda i, j, s: (i, j)),
                   pl.BlockSpec((tm, d), lambda i, j, s: (i, 0))],
        scratch_shapes=[pltpu.VMEM((tm, d), BF16)])
    return pl.pallas_call(
        _nm_matmul_kernel,
        grid_spec=grid_spec,
        out_shape=[jax.ShapeDtypeStruct((t, n), out_dtype), jax.ShapeDtypeStruct((t, d), BF16)],
        compiler_params=_cparams(("parallel", "arbitrary")),
        name="norm_mod_matmul",
    )(seg, x, g.reshape(1, d), mod3, mod3, w_bf16)


def _res_matmul_kernel(seg_ref, a_ref, w_ref, r_ref, gt_ref, o_ref):
    del seg_ref
    mm = jnp.dot(a_ref[...], w_ref[...], preferred_element_type=F32)
    o_ref[...] = r_ref[...] + gt_ref[0] * mm


def _res_matmul(a_bf16, w_bf16, res, mod3, gt_idx, seg, tm):
    t, k = a_bf16.shape
    n = w_bf16.shape[1]
    tn = _col_tile(n, 1024)
    grid_spec = pltpu.PrefetchScalarGridSpec(
        num_scalar_prefetch=1,
        grid=(t // tm, n // tn),
        in_specs=[pl.BlockSpec((tm, k), lambda i, j, s: (i, 0)),
                  pl.BlockSpec((k, tn), lambda i, j, s: (0, j)),
                  pl.BlockSpec((tm, tn), lambda i, j, s: (i, j)),
                  pl.BlockSpec((1, 1, tn), lambda i, j, s: (s[i] * 6 + gt_idx, 0, j))],
        out_specs=pl.BlockSpec((tm, tn), lambda i, j, s: (i, j)))
    return pl.pallas_call(
        _res_matmul_kernel,
        grid_spec=grid_spec,
        out_shape=jax.ShapeDtypeStruct((t, n), F32),
        compiler_params=_cparams(("parallel", "parallel")),
        name="res_matmul",
    )(seg, a_bf16, w_bf16, res, mod3)


def _scan_kernel(r_ref, kk_ref, w_ref, b_ref, kt_ref, v_ref, s0_ref, y_ref, sfin_ref, s_scr, *, tb_steps, nk):
    tb = pl.program_id(1)

    @pl.when(tb == 0)
    def _():
        s_scr[...] = s0_ref[...]

    def tree(parts):
        while len(parts) > 1:
            parts = [parts[i] + parts[i + 1] for i in range(0, len(parts), 2)]
        return parts[0]

    nacc = 4

    def step(t, carry):
        accs = [None] * nacc
        for k in range(nk):
            p = s_scr[k] * kk_ref[t, pl.ds(k, 1), :]
            accs[k % nacc] = p if accs[k % nacc] is None else accs[k % nacc] + p
        sa = tree(accs)
        v = v_ref[t]
        yacc = [None] * nacc
        for k in range(nk):
            s_new = (s_scr[k] * w_ref[t, pl.ds(k, 1), :] - sa * b_ref[t, pl.ds(k, 1), :]
                     + v * kt_ref[t, pl.ds(k, 1), :])
            s_scr[k] = s_new
            p = s_new * r_ref[t, pl.ds(k, 1), :]
            yacc[k % nacc] = p if yacc[k % nacc] is None else yacc[k % nacc] + p
        y_ref[t] = tree(yacc)
        return carry

    lax.fori_loop(0, tb_steps, step, 0)

    @pl.when(tb == pl.num_programs(1) - 1)
    def _():
        sfin_ref[...] = s_scr[...]


def _scan(r, kk, w, b, kt, v, s0):
    seq, nk, g = r.shape
    vs = v.shape[1]
    tb_steps = _row_tile(seq, cap=16)
    row = pl.BlockSpec((tb_steps, nk, LANES), lambda gi, ti: (ti, 0, gi))
    val = pl.BlockSpec((tb_steps, vs, LANES), lambda gi, ti: (ti, 0, gi))
    st = pl.BlockSpec((nk, vs, LANES), lambda gi, ti: (0, 0, gi))
    return pl.pallas_call(
        functools.partial(_scan_kernel, tb_steps=tb_steps, nk=nk),
        grid=(g // LANES, seq // tb_steps),
        in_specs=[row, row, row, row, row, val, st],
        out_specs=[val, st],
        out_shape=[jax.ShapeDtypeStruct((seq, vs, g), F32), jax.ShapeDtypeStruct((nk, vs, g), F32)],
        scratch_shapes=[pltpu.VMEM((nk, vs, LANES), F32)],
        compiler_params=_cparams(("parallel", "arbitrary")),
        name="rwkv7_scan",
    )(r, kk, w, b, kt, v, s0)


def _attn_kernel(q_ref, k_ref, v_ref, o_ref, *, dh):
    k = k_ref[0]
    v = v_ref[0]
    for g in range(B_GROUP):
        q = q_ref[0, :, g * dh:(g + 1) * dh]
        s = lax.dot_general(q, k, (((1,), (1,)), ((), ())), preferred_element_type=F32)
        m = jnp.max(s, axis=-1, keepdims=True)
        p = jnp.exp(s - m)
        den = jnp.sum(p, axis=-1, keepdims=True)
        o = jnp.dot(p.astype(BF16), v, preferred_element_type=F32) / den
        o_ref[0, :, g * dh:(g + 1) * dh] = o.astype(o_ref.dtype)


def _attend(q, k, v, dh):
    bn, lq, qw = q.shape
    lk = k.shape[1]
    kvh = k.shape[2] // dh
    gw = B_GROUP * dh
    tq = _row_tile(lq, cap=256)
    return pl.pallas_call(
        functools.partial(_attn_kernel, dh=dh),
        grid=(bn, kvh, lq // tq),
        in_specs=[pl.BlockSpec((1, tq, gw), lambda b, h, i: (b, i, h)),
                  pl.BlockSpec((1, lk, dh), lambda b, h, i: (b, 0, h)),
                  pl.BlockSpec((1, lk, dh), lambda b, h, i: (b, 0, h))],
        out_specs=pl.BlockSpec((1, tq, gw), lambda b, h, i: (b, i, h)),
        out_shape=jax.ShapeDtypeStruct((bn, lq, qw), BF16),
        compiler_params=_cparams(("parallel", "parallel", "parallel")),
        name="attention",
    )(q, k, v)


G_GROUP = 16


def _gate_matrix_kernel(i1_ref, i2_ref, g_ref, o_ref, gtmp_scr, *, nkeys):
    iota = lax.broadcasted_iota(I32, (nkeys, nkeys), 0)

    def build(grp, carry):
        base = pl.multiple_of(grp * G_GROUP, G_GROUP)
        def token(tt, c2):
            t = base + tt
            a_t = jnp.where(iota == i1_ref[pl.ds(t, 1), :], 1.0, 0.0).astype(BF16)
            b_t = jnp.where(iota == i2_ref[pl.ds(t, 1), :], g_ref[pl.ds(t, 1), :], 0.0).astype(BF16)
            gtmp_scr[pl.ds(pl.multiple_of(tt * nkeys, nkeys), nkeys), :] = lax.dot_general(
                a_t, b_t, (((1,), (1,)), ((), ())), preferred_element_type=F32)
            return c2

        lax.fori_loop(0, G_GROUP, token, 0, unroll=8)
        for n1 in range(nkeys):
            rows = gtmp_scr[pl.ds(n1, G_GROUP, stride=nkeys), :]
            o_ref[pl.ds(base, G_GROUP), n1 * nkeys:(n1 + 1) * nkeys] = rows.astype(BF16)
        return carry

    lax.fori_loop(0, o_ref.shape[0] // G_GROUP, build, 0)


def _gate_matrix(i1, i2, gate, nkeys):
    t, nj = i1.shape
    tb = LANES
    sel = pl.BlockSpec((tb, nj), lambda i: (i, 0))
    return pl.pallas_call(
        functools.partial(_gate_matrix_kernel, nkeys=nkeys),
        grid=(t // tb,),
        in_specs=[sel, sel, sel],
        out_specs=pl.BlockSpec((tb, nkeys * nkeys), lambda i: (i, 0)),
        out_shape=jax.ShapeDtypeStruct((t, nkeys * nkeys), BF16),
        scratch_shapes=[pltpu.VMEM((G_GROUP * nkeys, nkeys), F32)],
        compiler_params=_cparams(("parallel",)),
        name="peer_gate_matrix",
    )(i1, i2, gate)


def _peer_kernel(seg_ref, xb_ref, gm_ref, u_ref, v_ref, r_ref, gt_ref, o_ref, acc_scr):
    del seg_ref
    e = pl.program_id(1)

    @pl.when(e == 0)
    def _():
        acc_scr[...] = jnp.zeros_like(acc_scr)

    h = lax.dot_general(xb_ref[...], u_ref[...], (((1,), (1,)), ((), ())), preferred_element_type=F32)
    act = 0.5 * h * (1.0 + lax.erf(h * (1.0 / math.sqrt(2.0)))) * gm_ref[...].astype(F32)
    acc_scr[...] += jnp.dot(act.astype(BF16), v_ref[...], preferred_element_type=F32)

    @pl.when(e == pl.num_programs(1) - 1)
    def _():
        o_ref[...] = r_ref[...] + gt_ref[0] * acc_scr[...]


def _peer_experts(xb, gmat, u_bf16, v_bf16, res, mod3, gt_idx, seg, tm):
    t, d = xb.shape
    ne = u_bf16.shape[0]
    te = 512
    grid_spec = pltpu.PrefetchScalarGridSpec(
        num_scalar_prefetch=1,
        grid=(t // tm, ne // te),
        in_specs=[pl.BlockSpec((tm, d), lambda i, e, s: (i, 0)),
                  pl.BlockSpec((tm, te), lambda i, e, s: (i, e)),
                  pl.BlockSpec((te, d), lambda i, e, s: (e, 0)),
                  pl.BlockSpec((te, d), lambda i, e, s: (e, 0)),
                  pl.BlockSpec((tm, d), lambda i, e, s: (i, 0)),
                  pl.BlockSpec((1, 1, d), lambda i, e, s: (s[i] * 6 + gt_idx, 0, 0))],
        out_specs=pl.BlockSpec((tm, d), lambda i, e, s: (i, 0)),
        scratch_shapes=[pltpu.VMEM((tm, d), F32)])
    return pl.pallas_call(
        _peer_kernel,
        grid_spec=grid_spec,
        out_shape=jax.ShapeDtypeStruct((t, d), F32),
        compiler_params=_cparams(("parallel", "arbitrary")),
        name="peer_experts",
    )(seg, xb, gmat, u_bf16, v_bf16, res, mod3)


def _topk_rows(vals, payload, rows_out):
    big = jnp.int32(2 ** 30)
    top_v = jnp.zeros(rows_out.shape, F32)
    top_p = jnp.zeros(rows_out.shape, I32)
    for it in range(PEER_TOPK):
        m = jnp.max(vals, axis=0, keepdims=True)
        sel = jnp.min(jnp.where(vals == m, payload, big), axis=0, keepdims=True)
        top_v = jnp.where(rows_out == it, m, top_v)
        top_p = jnp.where(rows_out == it, sel, top_p)
        vals = jnp.where(payload == sel, -jnp.inf, vals)
    return top_v, top_p


def _gather_rows(table, sel):
    out = jnp.zeros(sel.shape, table.dtype)
    for i in range(PEER_TOPK):
        out = jnp.where(sel == i, table[i:i + 1, :], out)
    return out


def _route_kernel(q_ref, keys_ref, i1_ref, i2_ref, g_ref, n1_scr, n2_scr, gate_scr, *, heads, nkeys, dq):
    tt = q_ref.shape[0]
    k = PEER_TOPK
    half = k // 2
    n_iota = lax.broadcasted_iota(I32, (nkeys, tt), 0)
    rows_out = lax.broadcasted_iota(I32, (k, tt), 0)
    r = lax.broadcasted_iota(I32, (half * k + half, tt), 0)
    pair_id = jnp.where(r < half * k, r, (r - half * k + half) * k)

    def head(h, carry):
        tops = []
        for c in range(2):
            col = pl.multiple_of((h * 2 + c) * dq, dq)
            qhc = q_ref[:, pl.ds(col, dq)].astype(BF16)
            khc = keys_ref[h, c].astype(BF16)
            s = lax.dot_general(khc, qhc, (((1,), (1,)), ((), ())), preferred_element_type=F32)
            tops.append(_topk_rows(s, n_iota, rows_out))
        (s1, i1), (s2, i2) = tops
        cand = jnp.concatenate([s1[i:i + 1, :] + s2 for i in range(half)] + [s1[half:, :] + s2[0:1, :]], axis=0)
        top, ci = _topk_rows(cand, pair_id, rows_out)
        n1 = _gather_rows(i1, lax.shift_right_logical(ci, 4))
        n2 = _gather_rows(i2, lax.bitwise_and(ci, k - 1))
        ex = jnp.exp(top - top[0:1, :])
        gate = ex / jnp.sum(ex, axis=0, keepdims=True)
        row = pl.multiple_of(h * k, k)
        n1_scr[pl.ds(row, k), :] = n1
        n2_scr[pl.ds(row, k), :] = n2
        gate_scr[pl.ds(row, k), :] = gate
        return carry

    lax.fori_loop(0, heads, head, 0)
    i1_ref[...] = n1_scr[...].T
    i2_ref[...] = n2_scr[...].T
    g_ref[...] = gate_scr[...].T


def _peer_route(q, sub_keys):
    t, qw = q.shape
    heads, _, nkeys, dq = sub_keys.shape
    hk = heads * PEER_TOPK
    tt = LANES
    out = pl.BlockSpec((tt, hk), lambda i: (i, 0))
    return pl.pallas_call(
        functools.partial(_route_kernel, heads=heads, nkeys=nkeys, dq=dq),
        grid=(t // tt,),
        in_specs=[pl.BlockSpec((tt, qw), lambda i: (i, 0)),
                  pl.BlockSpec((heads, 2, nkeys, dq), lambda i: (0, 0, 0, 0))],
        out_specs=[out, out, out],
        out_shape=[jax.ShapeDtypeStruct((t, hk), I32), jax.ShapeDtypeStruct((t, hk), I32),
                   jax.ShapeDtypeStruct((t, hk), F32)],
        scratch_shapes=[pltpu.VMEM((hk, tt), I32), pltpu.VMEM((hk, tt), I32), pltpu.VMEM((hk, tt), F32)],
        compiler_params=_cparams(("parallel",)),
        name="peer_route",
    )(q, sub_keys)


def _split(x):
    hi = x.astype(BF16)
    return hi, (x - hi.astype(F32)).astype(BF16)


def _mm(a, b):
    return jnp.dot(a, b, preferred_element_type=F32)


def _head_sum(x, ones_blockdiag):
    hi, lo = _split(x)
    return _mm(hi, ones_blockdiag) + _mm(lo, ones_blockdiag)


def _shift_rows(x, first_row, last_row):
    n = x.shape[0]
    rows = lax.broadcasted_iota(I32, x.shape, 0)
    prev = jnp.where(rows == 0, first_row, pltpu.roll(x, 1, 0))
    nxt = jnp.where(rows == n - 1, last_row, pltpu.roll(x, n - 1, 0))
    return prev, nxt


def _conv3(x, first_row, last_row, taps):
    prev, nxt = _shift_rows(x, first_row, last_row)
    return prev * taps[0:1] + x * taps[1:2] + nxt * taps[2:3]


def _rwkv_pre_kernel(r_ref, k_ref, v_ref, low_ref, pr_ref, pk_ref, pv_ref, plow_ref, nr_ref, nk_ref, nv_ref,
                     nlow_ref, cr_ref, ck_ref, cv_ref, clow_ref, w0_ref, wu_ref, a0_ref, au_ref, gu_ref,
                     kkg_ref, ka_ref, rk_ref, ones_ref,
                     ro_ref, kko_ref, vo_ref, w0o_ref, w1o_ref, b0o_ref, b1o_ref, kt0o_ref, kt1o_ref,
                     go_ref, bonus_ref, *, rw, ra):
    r = _conv3(r_ref[...], pr_ref[0], nr_ref[0], cr_ref[...])
    k = _conv3(k_ref[...], pk_ref[0], nk_ref[0], ck_ref[...])
    v = _conv3(v_ref[...], pv_ref[0], nv_ref[0], cv_ref[...])
    low = _conv3(low_ref[...], plow_ref[0], nlow_ref[0], clow_ref[...])
    ones = ones_ref[...]
    kk = k * kkg_ref[...]
    kk = kk * lax.rsqrt(_head_sum(kk * kk, ones) + 1e-12)
    ro_ref[...] = r
    kko_ref[...] = kk
    vo_ref[...] = v
    bonus_ref[...] = _head_sum(r * k * rk_ref[...], ones) * v
    gd = low[:, 2 * rw + 2 * ra:]
    go_ref[...] = _mm(jax.nn.sigmoid(gd).astype(BF16), gu_ref[...])
    for d_, (wo, bo, kto) in enumerate(((w0o_ref, b0o_ref, kt0o_ref), (w1o_ref, b1o_ref, kt1o_ref))):
        wd = low[:, d_ * rw:(d_ + 1) * rw]
        ad = low[:, 2 * rw + d_ * ra:2 * rw + (d_ + 1) * ra]
        lw = w0_ref[d_:d_ + 1, :] + _mm(jnp.tanh(wd).astype(BF16), wu_ref[d_])
        softplus = jnp.maximum(-lw, 0.0) + jnp.log1p(jnp.exp(-jnp.abs(lw)))
        wo[...] = jnp.exp(-jnp.exp(-softplus - 0.5))
        a = jax.nn.sigmoid(a0_ref[d_:d_ + 1, :] + _mm(ad.astype(BF16), au_ref[d_]))
        bo[...] = kk * a
        kto[...] = k * (1.0 + (a - 1.0) * ka_ref[...])


def _rwkv_pre(z, prev_rows, next_rows, P, tm, offs):
    t = z.shape[0]
    w_ = P['a_w0'].shape[-1]
    rw, ra = P['a_wu'].shape[1], P['a_au'].shape[1]
    lw = offs['low_w']
    li = offs['low'] // lw
    row = lambda c, wd: pl.BlockSpec((tm, wd), lambda i: (i, c))
    edge = lambda c, wd: pl.BlockSpec((1, 1, wd), lambda i: (i, 0, c))
    full = lambda a: pl.BlockSpec(a.shape, lambda i: (0,) * a.ndim)
    conv = P['a_conv']
    consts = [conv[:, :w_], conv[:, w_:2 * w_], conv[:, 2 * w_:3 * w_], conv[:, 3 * w_:],
              P['a_w0'], P['a_wu'].astype(BF16), P['a_a0'], P['a_au'].astype(BF16), P['a_gu'].astype(BF16),
              P['a_kk'].reshape(1, w_), P['a_ka'].reshape(1, w_), P['a_rk'].reshape(1, w_),
              jnp.asarray(_blockdiag_ones(w_), BF16)]
    out = pl.BlockSpec((tm, w_), lambda i: (i, 0))
    return pl.pallas_call(
        functools.partial(_rwkv_pre_kernel, rw=rw, ra=ra),
        grid=(t // tm,),
        in_specs=[row(0, w_), row(1, w_), row(2, w_), row(li, lw),
                  edge(0, w_), edge(1, w_), edge(2, w_), edge(li, lw),
                  edge(0, w_), edge(1, w_), edge(2, w_), edge(li, lw)] + [full(a) for a in consts],
        out_specs=[out] * 11,
        out_shape=[jax.ShapeDtypeStruct((t, w_), F32)] * 11,
        compiler_params=_cparams(("parallel",)),
        name="rwkv7_pre",
    )(z, z, z, z, prev_rows, prev_rows, prev_rows, prev_rows, next_rows, next_rows, next_rows, next_rows, *consts)


def _blockdiag_ones(width):
    idx = np.arange(width) // A_HEAD_DIM
    return (idx[:, None] == idx[None, :]).astype(np.float32)


def _rwkv_post_kernel(yf_ref, yb_ref, bonus_ref, g_ref, lnw_ref, lnb_ref, ones_ref, o_ref):
    ones = ones_ref[...]
    y = yf_ref[...] + yb_ref[...]
    mu = _head_sum(y, ones) * (1.0 / A_HEAD_DIM)
    dlt = y - mu
    var = _head_sum(dlt * dlt, ones) * (1.0 / A_HEAD_DIM)
    yn = dlt * lax.rsqrt(var + A_GN_EPS) * lnw_ref[...] + lnb_ref[...]
    o_ref[...] = ((yn + bonus_ref[...]) * g_ref[...]).astype(o_ref.dtype)


def _rwkv_post(yf, yb, bonus, g, P, tm):
    t, w_ = yf.shape
    row = pl.BlockSpec((tm, w_), lambda i: (i, 0))
    vec = pl.BlockSpec((1, w_), lambda i: (0, 0))
    return pl.pallas_call(
        _rwkv_post_kernel,
        grid=(t // tm,),
        in_specs=[row, row, row, row, vec, vec, pl.BlockSpec((w_, w_), lambda i: (0, 0))],
        out_specs=row,
        out_shape=jax.ShapeDtypeStruct((t, w_), BF16),
        compiler_params=_cparams(("parallel",)),
        name="rwkv7_post",
    )(yf, yb, bonus, g, P['a_ln_w'].reshape(1, w_), P['a_ln_b'].reshape(1, w_),
      jnp.asarray(_blockdiag_ones(w_), BF16))


def _attn_pre_kernel(q_ref, kv_ref, qn_ref, kn_ref, cos_ref, sin_ref, qo_ref, ko_ref, vo_ref, kf_ref, vf_ref,
                     *, dh, latent):
    def rms(x, g):
        return x * lax.rsqrt(jnp.mean(x * x, axis=-1, keepdims=True) + NORM_EPS) * g

    def rope(x):
        if not latent:
            return x
        lanes = lax.broadcasted_iota(I32, x.shape, 1)
        quarter = dh // 4
        partner = jnp.where(lanes % (2 * quarter) < quarter, pltpu.roll(x, dh - quarter, 1), pltpu.roll(x, quarter, 1))
        return x * cos_ref[...] + partner * sin_ref[...]

    nq = q_ref.shape[1] // dh
    nkv = kv_ref.shape[1] // (2 * dh)
    for h in range(nq):
        q = rope(rms(q_ref[:, h * dh:(h + 1) * dh], qn_ref[...]))
        qo_ref[:, h * dh:(h + 1) * dh] = (q * dh ** -0.5).astype(BF16)
    for h in range(nkv):
        k = rms(kv_ref[:, h * dh:(h + 1) * dh], kn_ref[...])
        v = kv_ref[:, (nkv + h) * dh:(nkv + h + 1) * dh]
        kf_ref[:, h * dh:(h + 1) * dh] = k
        vf_ref[:, h * dh:(h + 1) * dh] = v
        ko_ref[:, h * dh:(h + 1) * dh] = rope(k).astype(BF16)
        vo_ref[:, h * dh:(h + 1) * dh] = v.astype(BF16)


def _rope_tables(seq, dh):
    quarter = dh // 4
    inv = ROPE_THETA ** (-np.arange(quarter, dtype=np.float64) / quarter)
    pos = np.arange(seq)
    ang_r = (pos // GRID_W)[:, None] * inv[None, :]
    ang_c = (pos % GRID_W)[:, None] * inv[None, :]
    cos = np.concatenate([np.cos(ang_r)] * 2 + [np.cos(ang_c)] * 2, axis=1)
    sin = np.concatenate([-np.sin(ang_r), np.sin(ang_r), -np.sin(ang_c), np.sin(ang_c)], axis=1)
    return jnp.asarray(cos, F32), jnp.asarray(sin, F32)


def _attn_pre(z, qnorm, knorm, offs, row0, rows, seq, tm, latent):
    dh = qnorm.shape[-1]
    bw, kvw2 = offs['q_w'], offs['kv_w']
    r0 = row0 // tm
    per_seq = seq // tm
    cos, sin = _rope_tables(seq, dh) if latent else (jnp.zeros((tm, dh), F32), jnp.zeros((tm, dh), F32))
    tab = pl.BlockSpec((tm, dh), (lambda i: (i % per_seq, 0)) if latent else (lambda i: (0, 0)))
    vec = pl.BlockSpec((1, dh), lambda i: (0, 0))
    kvo = pl.BlockSpec((tm, kvw2 // 2), lambda i: (i, 0))
    return pl.pallas_call(
        functools.partial(_attn_pre_kernel, dh=dh, latent=latent),
        grid=(rows // tm,),
        in_specs=[pl.BlockSpec((tm, bw), lambda i: (r0 + i, offs['q'] // bw)),
                  pl.BlockSpec((tm, kvw2), lambda i: (r0 + i, offs['kv'] // kvw2)),
                  vec, vec, tab, tab],
        out_specs=[pl.BlockSpec((tm, bw), lambda i: (i, 0)), kvo, kvo, kvo, kvo],
        out_shape=[jax.ShapeDtypeStruct((rows, bw), BF16), jax.ShapeDtypeStruct((rows, kvw2 // 2), BF16),
                   jax.ShapeDtypeStruct((rows, kvw2 // 2), BF16), jax.ShapeDtypeStruct((rows, kvw2 // 2), F32),
                   jax.ShapeDtypeStruct((rows, kvw2 // 2), F32)],
        compiler_params=_cparams(("parallel",)),
        name="attention_pre",
    )(z, z, qnorm.reshape(1, dh), knorm.reshape(1, dh), cos, sin)


HY_BLK = 256


def _dft_consts():
    n = 2 * HY_BLK
    k = np.arange(HY_BLK, dtype=np.float64)[:, None] + 0.5
    s = np.arange(HY_BLK, dtype=np.float64)[None, :]
    th = 2.0 * np.pi * k * s / n
    fwd = np.concatenate([np.cos(th), -np.sin(th)], axis=0)
    tau = np.arange(n, dtype=np.float64)[:, None]
    ph = 2.0 * np.pi * tau * (np.arange(HY_BLK, dtype=np.float64)[None, :] + 0.5) / n
    inv = np.concatenate([np.cos(ph), -np.sin(ph)], axis=1) * (2.0 / n)
    inv_cat = np.concatenate([inv[:HY_BLK], inv[HY_BLK:]], axis=1)

    def hl(a):
        a32 = jnp.asarray(a, F32)
        hi = a32.astype(BF16)
        return hi, (a32 - hi.astype(F32)).astype(BF16)

    return hl(fwd), hl(inv_cat)


def _lag_features(seq, emb):
    bands = (emb - 1) // 2
    t = np.linspace(0.0, 1.0, seq)
    wpos = 2.0 * np.pi * np.arange(seq) / seq
    f = np.linspace(1e-4, bands - 1, bands)
    z = np.concatenate([t[:, None], np.cos(f[None, :] * wpos[:, None]), -np.sin(f[None, :] * wpos[:, None])], axis=1)
    lag = np.concatenate([np.zeros(1, np.int64), np.arange(seq - 1, 0, -1), np.arange(seq)])
    return jnp.asarray(z[lag], F32)


def _hyena_filter_kernel(z_ref, fw1_ref, fb1_ref, freq_ref, fw2_ref, fb2_ref, w3b_ref, w3f_ref, dl_ref,
                         fh_ref, fl_ref, g_ref, hdn_scr, f_scr, *, seq):
    @pl.when((pl.program_id(0) == 0) & (pl.program_id(1) == 0))
    def _():
        h1 = jnp.sin(freq_ref[...] * (_mm(z_ref[...].astype(BF16), fw1_ref[...]) + fb1_ref[...]))
        hdn_scr[...] = jnp.sin(freq_ref[...] * (_mm(h1.astype(BF16), fw2_ref[...]) + fb2_ref[...])).astype(BF16)

    decay = jnp.exp(-z_ref[:, 0:1] * dl_ref[...])
    f_scr[0:seq, :] = _mm(hdn_scr[0:seq, :], w3b_ref[...]) * decay[0:seq, :]
    f_scr[seq:2 * seq, :] = _mm(hdn_scr[seq:2 * seq, :], w3f_ref[...]) * decay[seq:2 * seq, :]
    f = f_scr[...]
    scale = lax.rsqrt(jnp.sum(f * f, axis=0, keepdims=True) + 1e-12)
    rows = lax.broadcasted_iota(I32, f.shape, 0)
    f_scr[...] = jnp.where(rows == 0, 0.0, f * scale)
    for m in range(2 * seq // HY_BLK):
        hi, lo = _split(f_scr[m * HY_BLK:(m + 1) * HY_BLK, :])
        g_ref[0, m] = _mm(fh_ref[...], hi) + (_mm(fh_ref[...], lo) + _mm(fl_ref[...], hi))


def _hyena_spectra(seq, P, c_width, tc):
    emb, hid = P['c_fw1'].shape
    z = _lag_features(seq, emb)
    embp = 64
    z = jnp.pad(z, ((0, 0), (0, embp - emb)))
    fw1 = jnp.pad(P['c_fw1'], ((0, embp - emb), (0, 0))).astype(BF16)
    deltas = jnp.asarray(np.abs(np.linspace(math.log(DECAY_TARGET) / SLOW_DECAY_PCT,
                                            math.log(DECAY_TARGET) / FAST_DECAY_PCT, c_width)), F32).reshape(1, c_width)
    (fh, fl), _ = _dft_consts()
    nct = c_width // tc
    nseg = 2 * seq // HY_BLK
    full = lambda a: pl.BlockSpec(a.shape, lambda o, j: (0,) * a.ndim)
    w3 = P['c_fw3'].astype(BF16)
    consts = [z, fw1, P['c_fb1'].reshape(1, hid), P['c_freq'].reshape(1, hid), P['c_fw2'].astype(BF16),
              P['c_fb2'].reshape(1, hid)]
    return pl.pallas_call(
        functools.partial(_hyena_filter_kernel, seq=seq),
        grid=(HYENA_ORDER, nct),
        in_specs=[full(a) for a in consts] + [
            pl.BlockSpec((hid, tc), lambda o, j: (0, (o * 2 + 1) * nct + j)),
            pl.BlockSpec((hid, tc), lambda o, j: (0, (o * 2) * nct + j)),
            pl.BlockSpec((1, tc), lambda o, j: (0, j)), full(fh), full(fl)],
        out_specs=pl.BlockSpec((1, nseg, 2 * HY_BLK, tc), lambda o, j: (o, 0, 0, j)),
        out_shape=jax.ShapeDtypeStruct((HYENA_ORDER, nseg, 2 * HY_BLK, c_width), F32),
        scratch_shapes=[pltpu.VMEM((2 * seq, hid), BF16), pltpu.VMEM((2 * seq, tc), F32)],
        compiler_params=_cparams(("arbitrary", "arbitrary")),
        name="hyena_spectra",
    )(*consts, w3, w3, deltas, fh, fl)


def _hyena_conv_kernel(zin_ref, gate_ref, tz_ref, bz_ref, tg_ref, bg_ref, bias_ref, g_ref, fh_ref, fl_ref,
                       ih_ref, il_ref, o_ref, z_scr, gate_scr, u_scr, y_scr, *, nb, conv_in):
    half = HY_BLK
    zero = jnp.zeros((1, zin_ref.shape[-1]), F32)
    z = zin_ref[0]
    if conv_in:
        z = _conv3(z, zero, zero, tz_ref[...]) + bz_ref[...]
    z_scr[...] = z
    for j in range(nb):
        hi, lo = _split(z_scr[j * half:(j + 1) * half, :])
        u_scr[j] = _mm(fh_ref[...], hi) + (_mm(fh_ref[...], lo) + _mm(fl_ref[...], hi))

    chunk = 64

    def spectrum(ii, carry):
        for c in range(half // chunk):
            re = pl.ds(c * chunk, chunk)
            im = pl.ds(half + c * chunk, chunk)
            acc_re = jnp.zeros((chunk, zin_ref.shape[-1]), F32)
            acc_im = jnp.zeros((chunk, zin_ref.shape[-1]), F32)
            for j in range(nb):
                m = ii - 1 - j + nb
                ur, ui = u_scr[j, re, :], u_scr[j, im, :]
                gr, gi = g_ref[0, m, re, :], g_ref[0, m, im, :]
                acc_re = acc_re + (ur * gr - ui * gi)
                acc_im = acc_im + (ur * gi + ui * gr)
            y_scr[ii, re, :] = acc_re
            y_scr[ii, im, :] = acc_im
        return carry

    lax.fori_loop(0, nb + 1, spectrum, 0)

    gate_scr[...] = _conv3(gate_ref[0], zero, zero, tg_ref[...]) + bg_ref[...]

    def block(i, carry):
        rows = pl.ds(pl.multiple_of(i * half, half), half)
        ycat = jnp.concatenate([y_scr[i + 1], y_scr[i]], axis=0)
        hi, lo = _split(ycat)
        conv = _mm(ih_ref[...], hi) + (_mm(ih_ref[...], lo) + _mm(il_ref[...], hi))
        o_ref[0, rows, :] = gate_scr[rows, :] * (conv + bias_ref[...] * z_scr[rows, :])
        return carry

    lax.fori_loop(0, nb, block, 0)


def _hyena_conv(zin, zin_col0, gate_src, gate_col0, taps, tap_bias, bias, spectra, order, row0, nseq, seq, tc, conv_in):
    c_width = bias.shape[-1]
    nb = seq // HY_BLK
    nct = c_width // tc
    (fh, fl), (ih, il) = _dft_consts()
    s0 = row0 // seq
    zc, gc = zin_col0 // tc, gate_col0 // tc
    tapc = (zc if conv_in else gc)
    full = lambda a: pl.BlockSpec(a.shape, lambda j, b: (0,) * a.ndim)
    zin3 = zin.reshape(-1, seq, zin.shape[-1])
    gate3 = gate_src.reshape(-1, seq, gate_src.shape[-1])
    zs0 = s0 if conv_in else 0
    return pl.pallas_call(
        functools.partial(_hyena_conv_kernel, nb=nb, conv_in=conv_in),
        grid=(nct, nseq),
        in_specs=[pl.BlockSpec((1, seq, tc), lambda j, b: (zs0 + b, 0, zc + j)),
                  pl.BlockSpec((1, seq, tc), lambda j, b: (s0 + b, 0, gc + j)),
                  pl.BlockSpec((3, tc), lambda j, b: (0, tapc + j)),
                  pl.BlockSpec((1, tc), lambda j, b: (0, tapc + j)),
                  pl.BlockSpec((3, tc), lambda j, b: (0, gc + j)),
                  pl.BlockSpec((1, tc), lambda j, b: (0, gc + j)),
                  pl.BlockSpec((1, tc), lambda j, b: (0, j)),
                  pl.BlockSpec((1, 2 * nb, 2 * HY_BLK, tc), lambda j, b: (order, 0, 0, j)),
                  full(fh), full(fl), full(ih), full(il)],
        out_specs=pl.BlockSpec((1, seq, tc), lambda j, b: (b, 0, j)),
        out_shape=jax.ShapeDtypeStruct((nseq, seq, c_width), F32),
        scratch_shapes=[pltpu.VMEM((seq, tc), F32), pltpu.VMEM((seq, tc), F32),
                        pltpu.VMEM((nb, 2 * HY_BLK, tc), F32), pltpu.VMEM((nb + 1, 2 * HY_BLK, tc), F32)],
        compiler_params=_cparams(("arbitrary", "arbitrary")),
        name="hyena_conv",
    )(zin3, gate3, taps, tap_bias, taps, tap_bias, bias, spectra, fh, fl, ih, il)


def _to_chain_rows(x_dir, vh):
    two, bn, seq, heads, n = x_dir.shape
    y = jnp.transpose(x_dir, (2, 4, 0, 1, 3))
    y = jnp.broadcast_to(y[:, :, None], (seq, n, vh, two, bn, heads))
    return y.reshape(seq, n, vh * two * bn * heads)


def _rwkv_scan_pass(r, kk, v, w2, b2, kt2, s0, bn, seq):
    w_ = r.shape[-1]
    heads = w_ // A_HEAD_DIM
    n = A_HEAD_DIM
    chains = 2 * bn * heads
    vh = max(1, LANES // chains)
    vs = n // vh
    hs = (bn, seq, heads, n)

    def both(t):
        t = t.reshape(hs)
        return jnp.stack([t, jnp.flip(t, 1)], axis=0)

    def rev1(pair):
        return jnp.stack([pair[0].reshape(hs), jnp.flip(pair[1].reshape(hs), 1)], axis=0)

    rr = _to_chain_rows(both(r), vh)
    kr = _to_chain_rows(both(kk), vh)
    wr = _to_chain_rows(rev1(w2), vh)
    br = _to_chain_rows(rev1(b2), vh)
    ktr = _to_chain_rows(rev1(kt2), vh)
    v2 = both(v).reshape(2, bn, seq, heads, vh, vs)
    vr = jnp.transpose(v2, (2, 5, 4, 0, 1, 3)).reshape(seq, vs, vh * chains)
    if s0 is None:
        s0r = jnp.zeros((n, vs, vh * chains), F32)
    else:
        s0r = jnp.transpose(s0.reshape(bn, 2, heads, vh, vs, n), (5, 4, 3, 1, 0, 2)).reshape(n, vs, vh * chains)
    y, sfin = _scan(rr, kr, wr, br, ktr, vr, s0r)
    y = jnp.transpose(y.reshape(seq, vs, vh, 2, bn, heads), (3, 4, 0, 5, 2, 1)).reshape(2, bn, seq, w_)
    sfin = jnp.transpose(sfin.reshape(n, vs, vh, 2, bn, heads), (4, 3, 5, 2, 1, 0)).reshape(bn, 2, heads, n, n)
    return y[0].reshape(bn * seq, w_), jnp.flip(y[1], 1).reshape(bn * seq, w_), sfin


def kernel(x_prompt, x_sample, cache_b_k, cache_b_v, state_a, c, c_ctx, mod_w, mod_b, norm1, norm2,
           even_w_in, even_a_conv, even_a_w0, even_a_wu, even_a_a0, even_a_au, even_a_gu, even_a_kk,
           even_a_ka, even_a_rk, even_a_ln_w, even_a_ln_b, even_b_qnorm, even_b_knorm, even_w_out,
           odd_w_in, odd_c_conv, odd_c_conv_b, odd_c_fw1, odd_c_fb1, odd_c_freq, odd_c_fw2, odd_c_fb2,
           odd_c_fw3, odd_c_bias, odd_w_out, peer_wq, peer_keys, peer_u, peer_v):
    bp, sp, d = x_prompt.shape
    bs, ss, _ = x_sample.shape
    depth = mod_w.shape[0]
    tp, ts = bp * sp, bs * ss
    t_all = tp + ts
    a_width = even_a_w0.shape[-1]
    a_cols = even_a_conv.shape[-1]
    dh = even_b_qnorm.shape[-1]
    b_width = d // 2
    kv_width = b_width // B_GROUP
    kvh = kv_width // dh
    c_width = odd_c_bias.shape[-1]
    nkeys = peer_keys.shape[3]
    assert bs + 1 <= 8 and nkeys == LANES and peer_keys.shape[1] * PEER_TOPK == LANES

    tm = _row_tile(tp, ss, cap=512)
    seg = jnp.concatenate([jnp.zeros((tp // tm,), I32),
                           1 + jnp.arange(ts // tm, dtype=I32) // (ss // tm)])

    cond8 = jnp.zeros((8, d), F32).at[0].set(c_ctx).at[1:1 + bs].set(c)
    mods = _mod_table(cond8, mod_w, mod_b)

    x = jnp.concatenate([x_prompt.reshape(tp, d), x_sample.reshape(ts, d)], axis=0)
    new_k, new_v, new_s = [], [], []
    for layer in range(depth):
        j = layer // 2
        mod3 = mods[layer].reshape(8 * 6, 1, d)
        if layer % 2 == 0:
            PA = dict(a_conv=even_a_conv[j], a_w0=even_a_w0[j], a_wu=even_a_wu[j], a_a0=even_a_a0[j],
                      a_au=even_a_au[j], a_gu=even_a_gu[j], a_kk=even_a_kk[j], a_ka=even_a_ka[j],
                      a_rk=even_a_rk[j], a_ln_w=even_a_ln_w[j], a_ln_b=even_a_ln_b[j])
            w_in = even_w_in[j]
            w_perm = jnp.concatenate([w_in[:, :3 * a_width], w_in[:, a_cols:], w_in[:, 3 * a_width:a_cols]],
                                     axis=1).astype(BF16)
            offs = dict(q=3 * a_width, q_w=b_width, kv=3 * a_width + b_width, kv_w=2 * kv_width,
                        low=3 * a_width + b_width + 2 * kv_width, low_w=a_cols - 3 * a_width)
            assert a_width == b_width and offs['kv'] % offs['kv_w'] == 0 and offs['low'] % offs['low_w'] == 0
            z, _ = _nm_matmul(x, norm1[layer], mod3, 1, 0, seg, w_perm, tm)
            tmr = _row_tile(sp, ss, cap=256)
            nt = t_all // tmr
            starts = np.concatenate([np.arange(0, tp, sp), tp + np.arange(0, ts, ss), [t_all]])
            tile0 = np.arange(nt) * tmr
            keep_prev = jnp.asarray(~np.isin(tile0, starts), F32)[:, None]
            keep_next = jnp.asarray(~np.isin(tile0 + tmr, starts), F32)[:, None]
            zt = z.reshape(nt, tmr, z.shape[-1])
            zero_row = jnp.zeros((1, z.shape[-1]), F32)
            prev_rows = (jnp.concatenate([zero_row, zt[:-1, tmr - 1]], axis=0) * keep_prev)[:, None, :]
            next_rows = (jnp.concatenate([zt[1:, 0], zero_row], axis=0) * keep_next)[:, None, :]
            r, kk, vv, w0, w1, b0, b1, kt0, kt1, g, bonus = _rwkv_pre(z, prev_rows, next_rows, PA, tmr, offs)
            outs = []
            for (row0, bn, seq, latent) in ((0, bp, sp, False), (tp, bs, ss, True)):
                rows = bn * seq
                sl = slice(row0, row0 + rows)
                q, k, v, k_f32, v_f32 = _attn_pre(z, even_b_qnorm[j], even_b_knorm[j], offs, row0, rows, seq,
                                                  tmr, latent)
                k = k.reshape(bn, seq, kv_width)
                v = v.reshape(bn, seq, kv_width)
                if latent:
                    past = cache_b_k.shape[2]
                    k = jnp.concatenate([k, cache_b_k[:, j].astype(BF16).reshape(bn, past, kv_width)], axis=1)
                    v = jnp.concatenate([v, cache_b_v[:, j].astype(BF16).reshape(bn, past, kv_width)], axis=1)
                    s0 = state_a[:, j]
                else:
                    s0 = None
                    new_k.append(k_f32.reshape(bn, seq, kvh, dh))
                    new_v.append(v_f32.reshape(bn, seq, kvh, dh))
                y_b = _attend(q.reshape(bn, seq, b_width), k, v, dh).reshape(rows, b_width)
                yf, yb, s_fin = _rwkv_scan_pass(r[sl], kk[sl], vv[sl], (w0[sl], w1[sl]), (b0[sl], b1[sl]),
                                                (kt0[sl], kt1[sl]), s0, bn, seq)
                if not latent:
                    new_s.append(s_fin)
                y_a = _rwkv_post(yf, yb, bonus[sl], g[sl], PA, tmr)
                outs.append(jnp.concatenate([y_a, y_b], axis=-1))
            mix_in = jnp.concatenate(outs, axis=0)
            x = _res_matmul(mix_in, even_w_out[j].astype(BF16), x, mod3, 2, seg, tm)
        else:
            PC = dict(c_fw1=odd_c_fw1[j], c_fb1=odd_c_fb1[j], c_freq=odd_c_freq[j], c_fw2=odd_c_fw2[j],
                      c_fb2=odd_c_fb2[j], c_fw3=odd_c_fw3[j])
            u_pre, _ = _nm_matmul(x, norm1[layer], mod3, 1, 0, seg, odd_w_in[j].astype(BF16), tm)
            taps = odd_c_conv[j]
            tap_bias = odd_c_conv_b[j].reshape(1, 3 * c_width)
            tc = LANES
            zs = []
            for (row0, bn, seq) in ((0, bp, sp), (tp, bs, ss)):
                spectra = _hyena_spectra(seq, PC, c_width, tc)
                z1 = _hyena_conv(u_pre, 2 * c_width, u_pre, 0, taps, tap_bias, odd_c_bias[j, 0:1], spectra, 0,
                                 row0, bn, seq, tc, True)
                z2 = _hyena_conv(z1.reshape(bn * seq, c_width), 0, u_pre, c_width, taps, tap_bias,
                                 odd_c_bias[j, 1:2], spectra, 1, row0, bn, seq, tc, False)
                zs.append(z2.reshape(bn * seq, c_width))
            x = _res_matmul(jnp.concatenate(zs, axis=0).astype(BF16), odd_w_out[j].astype(BF16), x, mod3, 2, seg, tm)
        q, hm = _nm_matmul(x, norm2[layer], mod3, 4, 3, seg, peer_wq[layer].astype(BF16), tm)
        i1, i2, gate = _peer_route(q, peer_keys[layer])
        gmat = _gate_matrix(i1, i2, gate, nkeys)
        x = _peer_experts(hm, gmat, peer_u[layer].astype(BF16), peer_v[layer].astype(BF16),
                          x, mod3, 5, seg, tm)

    y_prompt = x[:tp].reshape(bp, sp, d)
    y_sample = x[tp:].reshape(bs, ss, d)
    return (y_prompt, y_sample, jnp.stack(new_k, axis=1), jnp.stack(new_v, axis=1), jnp.stack(new_s, axis=1))
```

```python
import functools
import math

import numpy as np
import jax
import jax.numpy as jnp
from jax import lax
from jax.experimental import pallas as pl
from jax.experimental.pallas import tpu as pltpu

F32 = jnp.float32
BF16 = jnp.bfloat16
I32 = jnp.int32

NORM_EPS = 1e-6
A_HEAD_DIM = 64
A_GN_EPS = 64e-5
B_GROUP = 4
GRID_W = 64
ROPE_THETA = 10000.0
HYENA_ORDER = 2
DECAY_TARGET = 1e-2
FAST_DECAY_PCT = 0.3
SLOW_DECAY_PCT = 1.5
PEER_TOPK = 16
LANES = 128
VMEM_LIMIT = 56 * 1024 * 1024


def _cparams(sem):
    return pltpu.CompilerParams(dimension_semantics=sem, vmem_limit_bytes=VMEM_LIMIT)


def _row_tile(*lengths, cap=512):
    t = cap
    while any(n % t for n in lengths):
        t //= 2
    return t


def _col_tile(n, cap):
    return max(t for t in range(LANES, cap + 1, LANES) if n % t == 0)


def _mod_kernel(c_ref, w_ref, b_ref, o_ref):
    c = c_ref[...]
    s = (c * jax.nn.sigmoid(c)).astype(BF16)
    o_ref[0] = jnp.dot(s, w_ref[0].astype(BF16), preferred_element_type=F32) + b_ref[0]


def _mod_table(cond8, mod_w, mod_b):
    depth, d, n = mod_w.shape
    tn = _row_tile(n, cap=1024)
    return pl.pallas_call(
        _mod_kernel,
        grid=(depth, n // tn),
        in_specs=[pl.BlockSpec((8, d), lambda l, j: (0, 0)),
                  pl.BlockSpec((1, d, tn), lambda l, j: (l, 0, j)),
                  pl.BlockSpec((1, 1, tn), lambda l, j: (l, 0, j))],
        out_specs=pl.BlockSpec((1, 8, tn), lambda l, j: (l, 0, j)),
        out_shape=jax.ShapeDtypeStruct((depth, 8, n), F32),
        compiler_params=_cparams(("parallel", "parallel")),
        name="mod_table",
    )(cond8, mod_w, mod_b.reshape(depth, 1, n))


def _nm_matmul_kernel(seg_ref, x_ref, g_ref, sc_ref, sh_ref, w_ref, o_ref, h_ref, h_scr):
    del seg_ref

    @pl.when(pl.program_id(1) == 0)
    def _():
        x = x_ref[...]
        y = x * lax.rsqrt(jnp.mean(x * x, axis=-1, keepdims=True) + NORM_EPS) * g_ref[...]
        h = (y * (1.0 + sc_ref[0]) + sh_ref[0]).astype(BF16)
        h_scr[...] = h
        h_ref[...] = h

    o_ref[...] = jnp.dot(h_scr[...], w_ref[...], preferred_element_type=F32).astype(o_ref.dtype)


def _nm_matmul(x, g, mod3, sc_idx, sh_idx, seg, w_bf16, tm, out_dtype=F32):
    t, d = x.shape
    n = w_bf16.shape[1]
    tn = _col_tile(n, 1664)
    grid_spec = pltpu.PrefetchScalarGridSpec(
        num_scalar_prefetch=1,
        grid=(t // tm, n // tn),
        in_specs=[pl.BlockSpec((tm, d), lambda i, j, s: (i, 0)),
                  pl.BlockSpec((1, d), lambda i, j, s: (0, 0)),
                  pl.BlockSpec((1, 1, d), lambda i, j, s: (s[i] * 6 + sc_idx, 0, 0)),
                  pl.BlockSpec((1, 1, d), lambda i, j, s: (s[i] * 6 + sh_idx, 0, 0)),
                  pl.BlockSpec((d, tn), lambda i, j, s: (0, j))],
        out_specs=[pl.BlockSpec((tm, tn), lambda i, j, s: (i, j)),
                   pl.BlockSpec((tm, d), lambda i, j, s: (i, 0))],
        scratch_shapes=[pltpu.VMEM((tm, d), BF16)])
    return pl.pallas_call(
        _nm_matmul_kernel,
        grid_spec=grid_spec,
        out_shape=[jax.ShapeDtypeStruct((t, n), out_dtype), jax.ShapeDtypeStruct((t, d), BF16)],
        compiler_params=_cparams(("parallel", "arbitrary")),
        name="norm_mod_matmul",
    )(seg, x, g.reshape(1, d), mod3, mod3, w_bf16)


def _res_matmul_kernel(seg_ref, a_ref, w_ref, r_ref, gt_ref, o_ref):
    del seg_ref
    mm = jnp.dot(a_ref[...], w_ref[...], preferred_element_type=F32)
    o_ref[...] = r_ref[...] + gt_ref[0] * mm


def _res_matmul(a_bf16, w_bf16, res, mod3, gt_idx, seg, tm):
    t, k = a_bf16.shape
    n = w_bf16.shape[1]
    tn = _col_tile(n, 1024)
    grid_spec = pltpu.PrefetchScalarGridSpec(
        num_scalar_prefetch=1,
        grid=(t // tm, n // tn),
        in_specs=[pl.BlockSpec((tm, k), lambda i, j, s: (i, 0)),
                  pl.BlockSpec((k, tn), lambda i, j, s: (0, j)),
                  pl.BlockSpec((tm, tn), lambda i, j, s: (i, j)),
                  pl.BlockSpec((1, 1, tn), lambda i, j, s: (s[i] * 6 + gt_idx, 0, j))],
        out_specs=pl.BlockSpec((tm, tn), lambda i, j, s: (i, j)))
    return pl.pallas_call(
        _res_matmul_kernel,
        grid_spec=grid_spec,
        out_shape=jax.ShapeDtypeStruct((t, n), F32),
        compiler_params=_cparams(("parallel", "parallel")),
        name="res_matmul",
    )(seg, a_bf16, w_bf16, res, mod3)


def _scan_kernel(rf_ref, kkf_ref, vf_ref, wf_ref, bf_ref, ktf_ref, rb_ref, kkb_ref, vb_ref, wb_ref, bb_ref,
                 ktb_ref, s0_ref, yf_ref, yb_ref, sfin_ref, s_scr, *, tb_steps, nk, kq):
    ti = pl.program_id(1)

    @pl.when(ti == 0)
    def _():
        s_scr[...] = s0_ref[...]

    def tree(parts):
        while len(parts) > 1:
            parts = [parts[i] + parts[i + 1] for i in range(0, len(parts), 2)]
        return parts[0]

    def all_parts(p):
        part = LANES // kq
        return tree([p] + [pltpu.roll(p, i * part, 1) for i in range(1, kq)])

    nacc = 4

    def advance(d, t, r_ref, kk_ref, v_ref, w_ref, b_ref, kt_ref, y_ref):
        accs = [None] * nacc
        for k in range(nk):
            p = s_scr[d, k] * kk_ref[t, pl.ds(k, 1), :]
            accs[k % nacc] = p if accs[k % nacc] is None else accs[k % nacc] + p
        sa = all_parts(tree(accs))
        v = v_ref[t]
        yacc = [None] * nacc
        for k in range(nk):
            s_new = (s_scr[d, k] * w_ref[0, t, pl.ds(k, 1), :] - sa * b_ref[0, t, pl.ds(k, 1), :]
                     + v * kt_ref[0, t, pl.ds(k, 1), :])
            s_scr[d, k] = s_new
            p = s_new * r_ref[t, pl.ds(k, 1), :]
            yacc[k % nacc] = p if yacc[k % nacc] is None else yacc[k % nacc] + p
        y_ref[t] = all_parts(tree(yacc))

    def step(s, carry):
        advance(0, s, rf_ref, kkf_ref, vf_ref, wf_ref, bf_ref, ktf_ref, yf_ref)
        advance(1, tb_steps - 1 - s, rb_ref, kkb_ref, vb_ref, wb_ref, bb_ref, ktb_ref, yb_ref)
        return carry

    lax.fori_loop(0, tb_steps, step, 0)

    @pl.when(ti == pl.num_programs(1) - 1)
    def _():
        sfin_ref[...] = s_scr[...]


def _scan(r, kk, v, w2, b2, kt2, s0, kq):
    seq, nk, c = r.shape
    nv = v.shape[1]
    tb_steps = _row_tile(seq, cap=16)
    nt = seq // tb_steps
    rowf = pl.BlockSpec((tb_steps, nk, LANES), lambda gi, ti: (ti, 0, gi))
    rowb = pl.BlockSpec((tb_steps, nk, LANES), lambda gi, ti: (nt - 1 - ti, 0, gi))
    valf = pl.BlockSpec((tb_steps, nv, LANES), lambda gi, ti: (ti, 0, gi))
    valb = pl.BlockSpec((tb_steps, nv, LANES), lambda gi, ti: (nt - 1 - ti, 0, gi))
    dirf = pl.BlockSpec((1, tb_steps, nk, LANES), lambda gi, ti: (0, ti, 0, gi))
    dirb = pl.BlockSpec((1, tb_steps, nk, LANES), lambda gi, ti: (1, nt - 1 - ti, 0, gi))
    st = pl.BlockSpec((2, nk, nv, LANES), lambda gi, ti: (0, 0, 0, gi))
    return pl.pallas_call(
        functools.partial(_scan_kernel, tb_steps=tb_steps, nk=nk, kq=kq),
        grid=(c // LANES, nt),
        in_specs=[rowf, rowf, valf, dirf, dirf, dirf, rowb, rowb, valb, dirb, dirb, dirb, st],
        out_specs=[valf, valb, st],
        out_shape=[jax.ShapeDtypeStruct((seq, nv, c), F32), jax.ShapeDtypeStruct((seq, nv, c), F32),
                   jax.ShapeDtypeStruct((2, nk, nv, c), F32)],
        scratch_shapes=[pltpu.VMEM((2, nk, nv, LANES), F32)],
        compiler_params=_cparams(("parallel", "arbitrary")),
        name="rwkv7_scan",
    )(r, kk, v, w2, b2, kt2, r, kk, v, w2, b2, kt2, s0)


def _attn_kernel(q_ref, k_ref, v_ref, o_ref, *, dh):
    k = k_ref[0]
    v = v_ref[0]
    for g in range(B_GROUP):
        q = q_ref[0, :, g * dh:(g + 1) * dh]
        s = lax.dot_general(q, k, (((1,), (1,)), ((), ())), preferred_element_type=F32)
        m = jnp.max(s, axis=-1, keepdims=True)
        p = jnp.exp(s - m)
        den = jnp.sum(p, axis=-1, keepdims=True)
        o = jnp.dot(p.astype(BF16), v, preferred_element_type=F32) / den
        o_ref[0, :, g * dh:(g + 1) * dh] = o.astype(o_ref.dtype)


def _attend(q, k, v, dh):
    bn, lq, qw = q.shape
    lk = k.shape[1]
    kvh = k.shape[2] // dh
    gw = B_GROUP * dh
    tq = _row_tile(lq, cap=256)
    return pl.pallas_call(
        functools.partial(_attn_kernel, dh=dh),
        grid=(bn, kvh, lq // tq),
        in_specs=[pl.BlockSpec((1, tq, gw), lambda b, h, i: (b, i, h)),
                  pl.BlockSpec((1, lk, dh), lambda b, h, i: (b, 0, h)),
                  pl.BlockSpec((1, lk, dh), lambda b, h, i: (b, 0, h))],
        out_specs=pl.BlockSpec((1, tq, gw), lambda b, h, i: (b, i, h)),
        out_shape=jax.ShapeDtypeStruct((bn, lq, qw), BF16),
        compiler_params=_cparams(("parallel", "parallel", "parallel")),
        name="attention",
    )(q, k, v)


G_GROUP = 16


def _gate_matrix_kernel(i1_ref, i2_ref, g_ref, o_ref, gtmp_scr, *, nkeys):
    iota = lax.broadcasted_iota(I32, (nkeys, nkeys), 0)

    def build(grp, carry):
        base = pl.multiple_of(grp * G_GROUP, G_GROUP)
        def token(tt, c2):
            t = base + tt
            a_t = jnp.where(iota == i1_ref[pl.ds(t, 1), :], 1.0, 0.0).astype(BF16)
            b_t = jnp.where(iota == i2_ref[pl.ds(t, 1), :], g_ref[pl.ds(t, 1), :], 0.0).astype(BF16)
            gtmp_scr[tt] = lax.dot_general(a_t, b_t, (((1,), (1,)), ((), ())), preferred_element_type=F32)
            return c2

        lax.fori_loop(0, G_GROUP, token, 0, unroll=8)
        by_n1 = jnp.swapaxes(gtmp_scr[...], 0, 1).astype(BF16)
        for n1 in range(nkeys):
            o_ref[pl.ds(base, G_GROUP), n1 * nkeys:(n1 + 1) * nkeys] = by_n1[n1]
        return carry

    lax.fori_loop(0, o_ref.shape[0] // G_GROUP, build, 0)


def _gate_matrix(i1, i2, gate, nkeys):
    t, nj = i1.shape
    tb = LANES
    sel = pl.BlockSpec((tb, nj), lambda i: (i, 0))
    return pl.pallas_call(
        functools.partial(_gate_matrix_kernel, nkeys=nkeys),
        grid=(t // tb,),
        in_specs=[sel, sel, sel],
        out_specs=pl.BlockSpec((tb, nkeys * nkeys), lambda i: (i, 0)),
        out_shape=jax.ShapeDtypeStruct((t, nkeys * nkeys), BF16),
        scratch_shapes=[pltpu.VMEM((G_GROUP, nkeys, nkeys), F32)],
        compiler_params=_cparams(("parallel",)),
        name="peer_gate_matrix",
    )(i1, i2, gate)


def _peer_kernel(seg_ref, xb_ref, gm_ref, u_ref, v_ref, r_ref, gt_ref, o_ref, acc_scr):
    del seg_ref
    e = pl.program_id(1)

    @pl.when(e == 0)
    def _():
        acc_scr[...] = jnp.zeros_like(acc_scr)

    h = lax.dot_general(xb_ref[...], u_ref[...], (((1,), (1,)), ((), ())), preferred_element_type=F32)
    act = 0.5 * h * (1.0 + lax.erf(h * (1.0 / math.sqrt(2.0)))) * gm_ref[...].astype(F32)
    acc_scr[...] += jnp.dot(act.astype(BF16), v_ref[...], preferred_element_type=F32)

    @pl.when(e == pl.num_programs(1) - 1)
    def _():
        o_ref[...] = r_ref[...] + gt_ref[0] * acc_scr[...]


def _peer_experts(xb, gmat, u_bf16, v_bf16, res, mod3, gt_idx, seg, tm):
    t, d = xb.shape
    ne = u_bf16.shape[0]
    te = 1024
    grid_spec = pltpu.PrefetchScalarGridSpec(
        num_scalar_prefetch=1,
        grid=(t // tm, ne // te),
        in_specs=[pl.BlockSpec((tm, d), lambda i, e, s: (i, 0)),
                  pl.BlockSpec((tm, te), lambda i, e, s: (i, e)),
                  pl.BlockSpec((te, d), lambda i, e, s: (e, 0)),
                  pl.BlockSpec((te, d), lambda i, e, s: (e, 0)),
                  pl.BlockSpec((tm, d), lambda i, e, s: (i, 0)),
                  pl.BlockSpec((1, 1, d), lambda i, e, s: (s[i] * 6 + gt_idx, 0, 0))],
        out_specs=pl.BlockSpec((tm, d), lambda i, e, s: (i, 0)),
        scratch_shapes=[pltpu.VMEM((tm, d), F32)])
    return pl.pallas_call(
        _peer_kernel,
        grid_spec=grid_spec,
        out_shape=jax.ShapeDtypeStruct((t, d), F32),
        compiler_params=_cparams(("parallel", "arbitrary")),
        name="peer_experts",
    )(seg, xb, gmat, u_bf16, v_bf16, res, mod3)


def _topk_rows(vals, payload, rows_out):
    big = jnp.float32(2 ** 30)
    top_v = jnp.zeros(rows_out.shape, F32)
    top_p = jnp.zeros(rows_out.shape, F32)
    for it in range(PEER_TOPK):
        m = jnp.max(vals, axis=0, keepdims=True)
        sel = jnp.min(jnp.where(vals == m, payload, big), axis=0, keepdims=True)
        top_v = jnp.where(rows_out == it, m, top_v)
        top_p = jnp.where(rows_out == it, sel, top_p)
        vals = jnp.where(payload == sel, -jnp.inf, vals)
    return top_v, top_p


def _gather_rows(table, sel):
    out = jnp.zeros(sel.shape, table.dtype)
    for i in range(PEER_TOPK):
        out = jnp.where(sel == i, table[i:i + 1, :], out)
    return out


def _route_kernel(q_ref, keys_ref, i1_ref, i2_ref, g_ref, n1_scr, n2_scr, gate_scr, *, heads, nkeys, dq):
    tt = q_ref.shape[0]
    k = PEER_TOPK
    half = k // 2
    n_iota = lax.broadcasted_iota(I32, (nkeys, tt), 0).astype(F32)
    rows_out = lax.broadcasted_iota(I32, (k, tt), 0)
    r = lax.broadcasted_iota(I32, (half * k + half, tt), 0)
    pair_id = jnp.where(r < half * k, r, (r - half * k + half) * k).astype(F32)

    def head(h, carry):
        tops = []
        for c in range(2):
            col = pl.multiple_of((h * 2 + c) * dq, dq)
            qhc = q_ref[:, pl.ds(col, dq)].astype(BF16)
            khc = keys_ref[h, c].astype(BF16)
            s = lax.dot_general(khc, qhc, (((1,), (1,)), ((), ())), preferred_element_type=F32)
            tops.append(_topk_rows(s, n_iota, rows_out))
        (s1, i1), (s2, i2) = tops
        cand = jnp.concatenate([s1[i:i + 1, :] + s2 for i in range(half)] + [s1[half:, :] + s2[0:1, :]], axis=0)
        top, ci = _topk_rows(cand, pair_id, rows_out)
        ci = ci.astype(I32)
        n1 = _gather_rows(i1, lax.shift_right_logical(ci, 4)).astype(I32)
        n2 = _gather_rows(i2, lax.bitwise_and(ci, k - 1)).astype(I32)
        ex = jnp.exp(top - top[0:1, :])
        gate = ex / jnp.sum(ex, axis=0, keepdims=True)
        row = pl.multiple_of(h * k, k)
        n1_scr[pl.ds(row, k), :] = n1
        n2_scr[pl.ds(row, k), :] = n2
        gate_scr[pl.ds(row, k), :] = gate
        return carry

    lax.fori_loop(0, heads, head, 0)
    i1_ref[...] = n1_scr[...].T
    i2_ref[...] = n2_scr[...].T
    g_ref[...] = gate_scr[...].T


def _peer_route(q, sub_keys):
    t, qw = q.shape
    heads, _, nkeys, dq = sub_keys.shape
    hk = heads * PEER_TOPK
    tt = LANES
    out = pl.BlockSpec((tt, hk), lambda i: (i, 0))
    return pl.pallas_call(
        functools.partial(_route_kernel, heads=heads, nkeys=nkeys, dq=dq),
        grid=(t // tt,),
        in_specs=[pl.BlockSpec((tt, qw), lambda i: (i, 0)),
                  pl.BlockSpec((heads, 2, nkeys, dq), lambda i: (0, 0, 0, 0))],
        out_specs=[out, out, out],
        out_shape=[jax.ShapeDtypeStruct((t, hk), I32), jax.ShapeDtypeStruct((t, hk), I32),
                   jax.ShapeDtypeStruct((t, hk), F32)],
        scratch_shapes=[pltpu.VMEM((hk, tt), I32), pltpu.VMEM((hk, tt), I32), pltpu.VMEM((hk, tt), F32)],
        compiler_params=_cparams(("parallel",)),
        name="peer_route",
    )(q, sub_keys)


def _split(x):
    hi = x.astype(BF16)
    return hi, (x - hi.astype(F32)).astype(BF16)


def _mm(a, b):
    return jnp.dot(a, b, preferred_element_type=F32)


def _head_sum(x, ones_blockdiag):
    hi, lo = _split(x)
    return _mm(hi, ones_blockdiag) + _mm(lo, ones_blockdiag)


def _shift_rows(x, first_row, last_row):
    n = x.shape[0]
    rows = lax.broadcasted_iota(I32, x.shape, 0)
    prev = jnp.where(rows == 0, first_row, pltpu.roll(x, 1, 0))
    nxt = jnp.where(rows == n - 1, last_row, pltpu.roll(x, n - 1, 0))
    return prev, nxt


def _conv3(x, first_row, last_row, taps):
    prev, nxt = _shift_rows(x, first_row, last_row)
    return prev * taps[0:1] + x * taps[1:2] + nxt * taps[2:3]


def _rwkv_pre_kernel(r_ref, k_ref, v_ref, low_ref, pr_ref, pk_ref, pv_ref, plow_ref, nr_ref, nk_ref, nv_ref,
                     nlow_ref, cr_ref, ck_ref, cv_ref, clow_ref, w0_ref, wu_ref, a0_ref, au_ref, gu_ref,
                     kkg_ref, ka_ref, rk_ref, ones_ref,
                     ro_ref, kko_ref, vo_ref, w0o_ref, w1o_ref, b0o_ref, b1o_ref, kt0o_ref, kt1o_ref,
                     go_ref, bonus_ref, *, rw, ra):
    r = _conv3(r_ref[...], pr_ref[0], nr_ref[0], cr_ref[...])
    k = _conv3(k_ref[...], pk_ref[0], nk_ref[0], ck_ref[...])
    v = _conv3(v_ref[...], pv_ref[0], nv_ref[0], cv_ref[...])
    low = _conv3(low_ref[...], plow_ref[0], nlow_ref[0], clow_ref[...])
    ones = ones_ref[...]
    kk = k * kkg_ref[...]
    kk = kk * lax.rsqrt(_head_sum(kk * kk, ones) + 1e-12)
    ro_ref[...] = r
    kko_ref[...] = kk
    vo_ref[...] = v
    bonus_ref[...] = _head_sum(r * k * rk_ref[...], ones) * v
    gd = low[:, 2 * rw + 2 * ra:]
    go_ref[...] = _mm(jax.nn.sigmoid(gd).astype(BF16), gu_ref[...])
    for d_, (wo, bo, kto) in enumerate(((w0o_ref, b0o_ref, kt0o_ref), (w1o_ref, b1o_ref, kt1o_ref))):
        wd = low[:, d_ * rw:(d_ + 1) * rw]
        ad = low[:, 2 * rw + d_ * ra:2 * rw + (d_ + 1) * ra]
        lw = w0_ref[d_:d_ + 1, :] + _mm(jnp.tanh(wd).astype(BF16), wu_ref[d_])
        softplus = jnp.maximum(-lw, 0.0) + jnp.log1p(jnp.exp(-jnp.abs(lw)))
        wo[...] = jnp.exp(-jnp.exp(-softplus - 0.5))
        a = jax.nn.sigmoid(a0_ref[d_:d_ + 1, :] + _mm(ad.astype(BF16), au_ref[d_]))
        bo[...] = kk * a
        kto[...] = k * (1.0 + (a - 1.0) * ka_ref[...])


def _rwkv_pre(z, prev_rows, next_rows, P, tm, offs):
    t = z.shape[0]
    w_ = P['a_w0'].shape[-1]
    rw, ra = P['a_wu'].shape[1], P['a_au'].shape[1]
    lw = offs['low_w']
    li = offs['low'] // lw
    row = lambda c, wd: pl.BlockSpec((tm, wd), lambda i: (i, c))
    edge = lambda c, wd: pl.BlockSpec((1, 1, wd), lambda i: (i, 0, c))
    full = lambda a: pl.BlockSpec(a.shape, lambda i: (0,) * a.ndim)
    conv = P['a_conv']
    consts = [conv[:, :w_], conv[:, w_:2 * w_], conv[:, 2 * w_:3 * w_], conv[:, 3 * w_:],
              P['a_w0'], P['a_wu'].astype(BF16), P['a_a0'], P['a_au'].astype(BF16), P['a_gu'].astype(BF16),
              P['a_kk'].reshape(1, w_), P['a_ka'].reshape(1, w_), P['a_rk'].reshape(1, w_),
              jnp.asarray(_blockdiag_ones(w_), BF16)]
    out = pl.BlockSpec((tm, w_), lambda i: (i, 0))
    return pl.pallas_call(
        functools.partial(_rwkv_pre_kernel, rw=rw, ra=ra),
        grid=(t // tm,),
        in_specs=[row(0, w_), row(1, w_), row(2, w_), row(li, lw),
                  edge(0, w_), edge(1, w_), edge(2, w_), edge(li, lw),
                  edge(0, w_), edge(1, w_), edge(2, w_), edge(li, lw)] + [full(a) for a in consts],
        out_specs=[out] * 11,
        out_shape=[jax.ShapeDtypeStruct((t, w_), F32)] * 11,
        compiler_params=_cparams(("parallel",)),
        name="rwkv7_pre",
    )(z, z, z, z, prev_rows, prev_rows, prev_rows, prev_rows, next_rows, next_rows, next_rows, next_rows, *consts)


def _blockdiag_ones(width):
    idx = np.arange(width) // A_HEAD_DIM
    return (idx[:, None] == idx[None, :]).astype(np.float32)


def _rwkv_post_kernel(yf_ref, yb_ref, bonus_ref, g_ref, lnw_ref, lnb_ref, ones_ref, o_ref):
    ones = ones_ref[...]
    y = yf_ref[...] + yb_ref[...]
    mu = _head_sum(y, ones) * (1.0 / A_HEAD_DIM)
    dlt = y - mu
    var = _head_sum(dlt * dlt, ones) * (1.0 / A_HEAD_DIM)
    yn = dlt * lax.rsqrt(var + A_GN_EPS) * lnw_ref[...] + lnb_ref[...]
    o_ref[...] = ((yn + bonus_ref[...]) * g_ref[...]).astype(o_ref.dtype)


def _rwkv_post(yf, yb, bonus, g, P, tm):
    t, w_ = yf.shape
    row = pl.BlockSpec((tm, w_), lambda i: (i, 0))
    vec = pl.BlockSpec((1, w_), lambda i: (0, 0))
    return pl.pallas_call(
        _rwkv_post_kernel,
        grid=(t // tm,),
        in_specs=[row, row, row, row, vec, vec, pl.BlockSpec((w_, w_), lambda i: (0, 0))],
        out_specs=row,
        out_shape=jax.ShapeDtypeStruct((t, w_), BF16),
        compiler_params=_cparams(("parallel",)),
        name="rwkv7_post",
    )(yf, yb, bonus, g, P['a_ln_w'].reshape(1, w_), P['a_ln_b'].reshape(1, w_),
      jnp.asarray(_blockdiag_ones(w_), BF16))


def _attn_pre_kernel(q_ref, kv_ref, qn_ref, kn_ref, cos_ref, sin_ref, qo_ref, ko_ref, vo_ref, kf_ref, vf_ref,
                     *, dh, latent):
    def rms(x, g):
        return x * lax.rsqrt(jnp.mean(x * x, axis=-1, keepdims=True) + NORM_EPS) * g

    def rope(x):
        if not latent:
            return x
        lanes = lax.broadcasted_iota(I32, x.shape, 1)
        quarter = dh // 4
        partner = jnp.where(lanes % (2 * quarter) < quarter, pltpu.roll(x, dh - quarter, 1), pltpu.roll(x, quarter, 1))
        return x * cos_ref[...] + partner * sin_ref[...]

    nq = q_ref.shape[1] // dh
    nkv = kv_ref.shape[1] // (2 * dh)
    for h in range(nq):
        q = rope(rms(q_ref[:, h * dh:(h + 1) * dh], qn_ref[...]))
        qo_ref[:, h * dh:(h + 1) * dh] = (q * dh ** -0.5).astype(BF16)
    for h in range(nkv):
        k = rms(kv_ref[:, h * dh:(h + 1) * dh], kn_ref[...])
        v = kv_ref[:, (nkv + h) * dh:(nkv + h + 1) * dh]
        kf_ref[:, h * dh:(h + 1) * dh] = k
        vf_ref[:, h * dh:(h + 1) * dh] = v
        ko_ref[:, h * dh:(h + 1) * dh] = rope(k).astype(BF16)
        vo_ref[:, h * dh:(h + 1) * dh] = v.astype(BF16)


def _rope_tables(seq, dh):
    quarter = dh // 4
    inv = ROPE_THETA ** (-np.arange(quarter, dtype=np.float64) / quarter)
    pos = np.arange(seq)
    ang_r = (pos // GRID_W)[:, None] * inv[None, :]
    ang_c = (pos % GRID_W)[:, None] * inv[None, :]
    cos = np.concatenate([np.cos(ang_r)] * 2 + [np.cos(ang_c)] * 2, axis=1)
    sin = np.concatenate([-np.sin(ang_r), np.sin(ang_r), -np.sin(ang_c), np.sin(ang_c)], axis=1)
    return jnp.asarray(cos, F32), jnp.asarray(sin, F32)


def _attn_pre(z, qnorm, knorm, offs, row0, rows, seq, tm, latent):
    dh = qnorm.shape[-1]
    bw, kvw2 = offs['q_w'], offs['kv_w']
    r0 = row0 // tm
    per_seq = seq // tm
    cos, sin = _rope_tables(seq, dh) if latent else (jnp.zeros((tm, dh), F32), jnp.zeros((tm, dh), F32))
    tab = pl.BlockSpec((tm, dh), (lambda i: (i % per_seq, 0)) if latent else (lambda i: (0, 0)))
    vec = pl.BlockSpec((1, dh), lambda i: (0, 0))
    kvo = pl.BlockSpec((tm, kvw2 // 2), lambda i: (i, 0))
    return pl.pallas_call(
        functools.partial(_attn_pre_kernel, dh=dh, latent=latent),
        grid=(rows // tm,),
        in_specs=[pl.BlockSpec((tm, bw), lambda i: (r0 + i, offs['q'] // bw)),
                  pl.BlockSpec((tm, kvw2), lambda i: (r0 + i, offs['kv'] // kvw2)),
                  vec, vec, tab, tab],
        out_specs=[pl.BlockSpec((tm, bw), lambda i: (i, 0)), kvo, kvo, kvo, kvo],
        out_shape=[jax.ShapeDtypeStruct((rows, bw), BF16), jax.ShapeDtypeStruct((rows, kvw2 // 2), BF16),
                   jax.ShapeDtypeStruct((rows, kvw2 // 2), BF16), jax.ShapeDtypeStruct((rows, kvw2 // 2), F32),
                   jax.ShapeDtypeStruct((rows, kvw2 // 2), F32)],
        compiler_params=_cparams(("parallel",)),
        name="attention_pre",
    )(z, z, qnorm.reshape(1, dh), knorm.reshape(1, dh), cos, sin)


HY_BLK = 256


def _dft_consts():
    n = 2 * HY_BLK
    k = np.arange(HY_BLK, dtype=np.float64)[:, None] + 0.5
    s = np.arange(HY_BLK, dtype=np.float64)[None, :]
    th = 2.0 * np.pi * k * s / n
    fwd = np.concatenate([np.cos(th), -np.sin(th)], axis=0)
    tau = np.arange(n, dtype=np.float64)[:, None]
    ph = 2.0 * np.pi * tau * (np.arange(HY_BLK, dtype=np.float64)[None, :] + 0.5) / n
    inv = np.concatenate([np.cos(ph), -np.sin(ph)], axis=1) * (2.0 / n)
    inv_cat = np.concatenate([inv[:HY_BLK], inv[HY_BLK:]], axis=1)

    def hl(a):
        a32 = jnp.asarray(a, F32)
        hi = a32.astype(BF16)
        return hi, (a32 - hi.astype(F32)).astype(BF16)

    return hl(fwd), hl(inv_cat)


def _lag_features(seq, emb):
    bands = (emb - 1) // 2
    t = np.linspace(0.0, 1.0, seq)
    wpos = 2.0 * np.pi * np.arange(seq) / seq
    f = np.linspace(1e-4, bands - 1, bands)
    z = np.concatenate([t[:, None], np.cos(f[None, :] * wpos[:, None]), -np.sin(f[None, :] * wpos[:, None])], axis=1)
    lag = np.concatenate([np.zeros(1, np.int64), np.arange(seq - 1, 0, -1), np.arange(seq)])
    return jnp.asarray(z[lag], F32)


def _hyena_filter_kernel(z_ref, fw1_ref, fb1_ref, freq_ref, fw2_ref, fb2_ref, w3b_ref, w3f_ref, dl_ref,
                         fh_ref, fl_ref, g_ref, hdn_scr, f_scr, *, seq):
    @pl.when((pl.program_id(0) == 0) & (pl.program_id(1) == 0))
    def _():
        h1 = jnp.sin(freq_ref[...] * (_mm(z_ref[...].astype(BF16), fw1_ref[...]) + fb1_ref[...]))
        hdn_scr[...] = jnp.sin(freq_ref[...] * (_mm(h1.astype(BF16), fw2_ref[...]) + fb2_ref[...])).astype(BF16)

    decay = jnp.exp(-z_ref[:, 0:1] * dl_ref[...])
    f_scr[0:seq, :] = _mm(hdn_scr[0:seq, :], w3b_ref[...]) * decay[0:seq, :]
    f_scr[seq:2 * seq, :] = _mm(hdn_scr[seq:2 * seq, :], w3f_ref[...]) * decay[seq:2 * seq, :]
    f = f_scr[...]
    scale = lax.rsqrt(jnp.sum(f * f, axis=0, keepdims=True) + 1e-12)
    rows = lax.broadcasted_iota(I32, f.shape, 0)
    f_scr[...] = jnp.where(rows == 0, 0.0, f * scale)
    for m in range(2 * seq // HY_BLK):
        hi, lo = _split(f_scr[m * HY_BLK:(m + 1) * HY_BLK, :])
        g_ref[0, m] = _mm(fh_ref[...], hi) + (_mm(fh_ref[...], lo) + _mm(fl_ref[...], hi))


def _hyena_spectra(seq, P, c_width, tc):
    emb, hid = P['c_fw1'].shape
    z = _lag_features(seq, emb)
    embp = 64
    z = jnp.pad(z, ((0, 0), (0, embp - emb)))
    fw1 = jnp.pad(P['c_fw1'], ((0, embp - emb), (0, 0))).astype(BF16)
    deltas = jnp.asarray(np.abs(np.linspace(math.log(DECAY_TARGET) / SLOW_DECAY_PCT,
                                            math.log(DECAY_TARGET) / FAST_DECAY_PCT, c_width)), F32).reshape(1, c_width)
    (fh, fl), _ = _dft_consts()
    nct = c_width // tc
    nseg = 2 * seq // HY_BLK
    full = lambda a: pl.BlockSpec(a.shape, lambda o, j: (0,) * a.ndim)
    w3 = P['c_fw3'].astype(BF16)
    consts = [z, fw1, P['c_fb1'].reshape(1, hid), P['c_freq'].reshape(1, hid), P['c_fw2'].astype(BF16),
              P['c_fb2'].reshape(1, hid)]
    return pl.pallas_call(
        functools.partial(_hyena_filter_kernel, seq=seq),
        grid=(HYENA_ORDER, nct),
        in_specs=[full(a) for a in consts] + [
            pl.BlockSpec((hid, tc), lambda o, j: (0, (o * 2 + 1) * nct + j)),
            pl.BlockSpec((hid, tc), lambda o, j: (0, (o * 2) * nct + j)),
            pl.BlockSpec((1, tc), lambda o, j: (0, j)), full(fh), full(fl)],
        out_specs=pl.BlockSpec((1, nseg, 2 * HY_BLK, tc), lambda o, j: (o, 0, 0, j)),
        out_shape=jax.ShapeDtypeStruct((HYENA_ORDER, nseg, 2 * HY_BLK, c_width), F32),
        scratch_shapes=[pltpu.VMEM((2 * seq, hid), BF16), pltpu.VMEM((2 * seq, tc), F32)],
        compiler_params=_cparams(("arbitrary", "arbitrary")),
        name="hyena_spectra",
    )(*consts, w3, w3, deltas, fh, fl)


def _hyena_conv_kernel(zin_ref, gate_ref, tz_ref, bz_ref, tg_ref, bg_ref, bias_ref, g_ref, fh_ref, fl_ref,
                       ih_ref, il_ref, o_ref, z_scr, gate_scr, u_scr, y_scr, *, nb, conv_in):
    half = HY_BLK
    zero = jnp.zeros((1, zin_ref.shape[-1]), F32)
    z = zin_ref[0]
    if conv_in:
        z = _conv3(z, zero, zero, tz_ref[...]) + bz_ref[...]
    z_scr[...] = z
    for j in range(nb):
        hi, lo = _split(z_scr[j * half:(j + 1) * half, :])
        u_scr[j] = _mm(fh_ref[...], hi) + (_mm(fh_ref[...], lo) + _mm(fl_ref[...], hi))

    chunk = 32
    width = zin_ref.shape[-1]

    def spectra(pieces):
        for c in range(half // chunk):
            re = pl.ds(c * chunk, chunk)
            im = pl.ds(half + c * chunk, chunk)

            def add_block(j, acc):
                ur, ui = u_scr[j, re, :], u_scr[j, im, :]
                out = []
                for a, ii in enumerate(pieces):
                    m = ii - 1 - j + nb
                    gr, gi = g_ref[0, m, re, :], g_ref[0, m, im, :]
                    out.append((acc[a][0] + (ur * gr - ui * gi), acc[a][1] + (ur * gi + ui * gr)))
                return tuple(out)

            zero = jnp.zeros((chunk, width), F32)
            acc = lax.fori_loop(0, nb, add_block, tuple((zero, zero) for _ in pieces), unroll=min(4, nb))
            for a, ii in enumerate(pieces):
                y_scr[ii, re, :] = acc[a][0]
                y_scr[ii, im, :] = acc[a][1]

    def pair(p, carry):
        spectra([2 * p, 2 * p + 1])
        return carry

    lax.fori_loop(0, (nb + 1) // 2, pair, 0)
    if (nb + 1) % 2:
        spectra([nb])

    gate_scr[...] = _conv3(gate_ref[0], zero, zero, tg_ref[...]) + bg_ref[...]

    def block(i, carry):
        rows = pl.ds(pl.multiple_of(i * half, half), half)
        ycat = jnp.concatenate([y_scr[i + 1], y_scr[i]], axis=0)
        hi, lo = _split(ycat)
        conv = _mm(ih_ref[...], hi) + (_mm(ih_ref[...], lo) + _mm(il_ref[...], hi))
        o_ref[0, rows, :] = gate_scr[rows, :] * (conv + bias_ref[...] * z_scr[rows, :])
        return carry

    lax.fori_loop(0, nb, block, 0)


def _hyena_conv(zin, zin_col0, gate_src, gate_col0, taps, tap_bias, bias, spectra, order, row0, nseq, seq, tc, conv_in):
    c_width = bias.shape[-1]
    nb = seq // HY_BLK
    nct = c_width // tc
    (fh, fl), (ih, il) = _dft_consts()
    s0 = row0 // seq
    zc, gc = zin_col0 // tc, gate_col0 // tc
    tapc = (zc if conv_in else gc)
    full = lambda a: pl.BlockSpec(a.shape, lambda j, b: (0,) * a.ndim)
    zin3 = zin.reshape(-1, seq, zin.shape[-1])
    gate3 = gate_src.reshape(-1, seq, gate_src.shape[-1])
    zs0 = s0 if conv_in else 0
    return pl.pallas_call(
        functools.partial(_hyena_conv_kernel, nb=nb, conv_in=conv_in),
        grid=(nct, nseq),
        in_specs=[pl.BlockSpec((1, seq, tc), lambda j, b: (zs0 + b, 0, zc + j)),
                  pl.BlockSpec((1, seq, tc), lambda j, b: (s0 + b, 0, gc + j)),
                  pl.BlockSpec((3, tc), lambda j, b: (0, tapc + j)),
                  pl.BlockSpec((1, tc), lambda j, b: (0, tapc + j)),
                  pl.BlockSpec((3, tc), lambda j, b: (0, gc + j)),
                  pl.BlockSpec((1, tc), lambda j, b: (0, gc + j)),
                  pl.BlockSpec((1, tc), lambda j, b: (0, j)),
                  pl.BlockSpec((1, 2 * nb, 2 * HY_BLK, tc), lambda j, b: (order, 0, 0, j)),
                  full(fh), full(fl), full(ih), full(il)],
        out_specs=pl.BlockSpec((1, seq, tc), lambda j, b: (b, 0, j)),
        out_shape=jax.ShapeDtypeStruct((nseq, seq, c_width), F32),
        scratch_shapes=[pltpu.VMEM((seq, tc), F32), pltpu.VMEM((seq, tc), F32),
                        pltpu.VMEM((nb, 2 * HY_BLK, tc), F32), pltpu.VMEM((nb + 1, 2 * HY_BLK, tc), F32)],
        compiler_params=_cparams(("arbitrary", "arbitrary")),
        name="hyena_conv",
    )(zin3, gate3, taps, tap_bias, taps, tap_bias, bias, spectra, fh, fl, ih, il)


def _rwkv_scan_pass(r, kk, v, w2, b2, kt2, s0, bn, seq):
    w_ = r.shape[-1]
    heads = w_ // A_HEAD_DIM
    n = A_HEAD_DIM
    bh = bn * heads
    kq = max(1, LANES // bh)
    nk = n // kq
    c = kq * bh

    def rows(x):
        x = x.reshape(bn, seq, heads, kq, nk)
        return jnp.transpose(x, (1, 4, 3, 0, 2)).reshape(seq, nk, c)

    vr = jnp.transpose(v.reshape(bn, seq, heads, n), (1, 3, 0, 2))
    vr = jnp.broadcast_to(vr[:, :, None], (seq, n, kq, bn, heads)).reshape(seq, n, c)
    if s0 is None:
        s0r = jnp.zeros((2, nk, n, c), F32)
    else:
        s0r = jnp.transpose(s0.reshape(bn, 2, heads, n, kq, nk), (1, 5, 3, 4, 0, 2)).reshape(2, nk, n, c)
    yf, yb, sfin = _scan(rows(r), rows(kk), vr, jnp.stack([rows(w2[0]), rows(w2[1])]),
                         jnp.stack([rows(b2[0]), rows(b2[1])]), jnp.stack([rows(kt2[0]), rows(kt2[1])]), s0r, kq)

    def tokens(y):
        y = y[:, :, :bh].reshape(seq, n, bn, heads)
        return jnp.transpose(y, (2, 0, 3, 1)).reshape(bn * seq, w_)

    sfin = jnp.transpose(sfin.reshape(2, nk, n, kq, bn, heads), (4, 0, 5, 2, 3, 1)).reshape(bn, 2, heads, n, n)
    return tokens(yf), tokens(yb), sfin


def kernel(x_prompt, x_sample, cache_b_k, cache_b_v, state_a, c, c_ctx, mod_w, mod_b, norm1, norm2,
           even_w_in, even_a_conv, even_a_w0, even_a_wu, even_a_a0, even_a_au, even_a_gu, even_a_kk,
           even_a_ka, even_a_rk, even_a_ln_w, even_a_ln_b, even_b_qnorm, even_b_knorm, even_w_out,
           odd_w_in, odd_c_conv, odd_c_conv_b, odd_c_fw1, odd_c_fb1, odd_c_freq, odd_c_fw2, odd_c_fb2,
           odd_c_fw3, odd_c_bias, odd_w_out, peer_wq, peer_keys, peer_u, peer_v):
    bp, sp, d = x_prompt.shape
    bs, ss, _ = x_sample.shape
    depth = mod_w.shape[0]
    tp, ts = bp * sp, bs * ss
    t_all = tp + ts
    a_width = even_a_w0.shape[-1]
    a_cols = even_a_conv.shape[-1]
    dh = even_b_qnorm.shape[-1]
    b_width = d // 2
    kv_width = b_width // B_GROUP
    kvh = kv_width // dh
    c_width = odd_c_bias.shape[-1]
    nkeys = peer_keys.shape[3]
    assert bs + 1 <= 8 and nkeys == LANES and peer_keys.shape[1] * PEER_TOPK == LANES

    tm = _row_tile(tp, ss, cap=512)
    seg = jnp.concatenate([jnp.zeros((tp // tm,), I32),
                           1 + jnp.arange(ts // tm, dtype=I32) // (ss // tm)])

    cond8 = jnp.zeros((8, d), F32).at[0].set(c_ctx).at[1:1 + bs].set(c)
    mods = _mod_table(cond8, mod_w, mod_b)

    x = jnp.concatenate([x_prompt.reshape(tp, d), x_sample.reshape(ts, d)], axis=0)
    new_k, new_v, new_s = [], [], []
    for layer in range(depth):
        j = layer // 2
        mod3 = mods[layer].reshape(8 * 6, 1, d)
        if layer % 2 == 0:
            PA = dict(a_conv=even_a_conv[j], a_w0=even_a_w0[j], a_wu=even_a_wu[j], a_a0=even_a_a0[j],
                      a_au=even_a_au[j], a_gu=even_a_gu[j], a_kk=even_a_kk[j], a_ka=even_a_ka[j],
                      a_rk=even_a_rk[j], a_ln_w=even_a_ln_w[j], a_ln_b=even_a_ln_b[j])
            w_in = even_w_in[j]
            w_perm = jnp.concatenate([w_in[:, :3 * a_width], w_in[:, a_cols:], w_in[:, 3 * a_width:a_cols]],
                                     axis=1).astype(BF16)
            offs = dict(q=3 * a_width, q_w=b_width, kv=3 * a_width + b_width, kv_w=2 * kv_width,
                        low=3 * a_width + b_width + 2 * kv_width, low_w=a_cols - 3 * a_width)
            assert a_width == b_width and offs['kv'] % offs['kv_w'] == 0 and offs['low'] % offs['low_w'] == 0
            z, _ = _nm_matmul(x, norm1[layer], mod3, 1, 0, seg, w_perm, tm)
            tmr = _row_tile(sp, ss, cap=256)
            nt = t_all // tmr
            starts = np.concatenate([np.arange(0, tp, sp), tp + np.arange(0, ts, ss), [t_all]])
            tile0 = np.arange(nt) * tmr
            keep_prev = jnp.asarray(~np.isin(tile0, starts), F32)[:, None]
            keep_next = jnp.asarray(~np.isin(tile0 + tmr, starts), F32)[:, None]
            zt = z.reshape(nt, tmr, z.shape[-1])
            zero_row = jnp.zeros((1, z.shape[-1]), F32)
            prev_rows = (jnp.concatenate([zero_row, zt[:-1, tmr - 1]], axis=0) * keep_prev)[:, None, :]
            next_rows = (jnp.concatenate([zt[1:, 0], zero_row], axis=0) * keep_next)[:, None, :]
            r, kk, vv, w0, w1, b0, b1, kt0, kt1, g, bonus = _rwkv_pre(z, prev_rows, next_rows, PA, tmr, offs)
            outs = []
            for (row0, bn, seq, latent) in ((0, bp, sp, False), (tp, bs, ss, True)):
                rows = bn * seq
                sl = slice(row0, row0 + rows)
                q, k, v, k_f32, v_f32 = _attn_pre(z, even_b_qnorm[j], even_b_knorm[j], offs, row0, rows, seq,
                                                  tmr, latent)
                k = k.reshape(bn, seq, kv_width)
                v = v.reshape(bn, seq, kv_width)
                if latent:
                    past = cache_b_k.shape[2]
                    k = jnp.concatenate([k, cache_b_k[:, j].astype(BF16).reshape(bn, past, kv_width)], axis=1)
                    v = jnp.concatenate([v, cache_b_v[:, j].astype(BF16).reshape(bn, past, kv_width)], axis=1)
                    s0 = state_a[:, j]
                else:
                    s0 = None
                    new_k.append(k_f32.reshape(bn, seq, kvh, dh))
                    new_v.append(v_f32.reshape(bn, seq, kvh, dh))
                y_b = _attend(q.reshape(bn, seq, b_width), k, v, dh).reshape(rows, b_width)
                yf, yb, s_fin = _rwkv_scan_pass(r[sl], kk[sl], vv[sl], (w0[sl], w1[sl]), (b0[sl], b1[sl]),
                                                (kt0[sl], kt1[sl]), s0, bn, seq)
                if not latent:
                    new_s.append(s_fin)
                y_a = _rwkv_post(yf, yb, bonus[sl], g[sl], PA, tmr)
                outs.append(jnp.concatenate([y_a, y_b], axis=-1))
            mix_in = jnp.concatenate(outs, axis=0)
            x = _res_matmul(mix_in, even_w_out[j].astype(BF16), x, mod3, 2, seg, tm)
        else:
            PC = dict(c_fw1=odd_c_fw1[j], c_fb1=odd_c_fb1[j], c_freq=odd_c_freq[j], c_fw2=odd_c_fw2[j],
                      c_fb2=odd_c_fb2[j], c_fw3=odd_c_fw3[j])
            u_pre, _ = _nm_matmul(x, norm1[layer], mod3, 1, 0, seg, odd_w_in[j].astype(BF16), tm)
            taps = odd_c_conv[j]
            tap_bias = odd_c_conv_b[j].reshape(1, 3 * c_width)
            tc = LANES
            zs = []
            for (row0, bn, seq) in ((0, bp, sp), (tp, bs, ss)):
                spectra = _hyena_spectra(seq, PC, c_width, tc)
                z1 = _hyena_conv(u_pre, 2 * c_width, u_pre, 0, taps, tap_bias, odd_c_bias[j, 0:1], spectra, 0,
                                 row0, bn, seq, tc, True)
                z2 = _hyena_conv(z1.reshape(bn * seq, c_width), 0, u_pre, c_width, taps, tap_bias,
                                 odd_c_bias[j, 1:2], spectra, 1, row0, bn, seq, tc, False)
                zs.append(z2.reshape(bn * seq, c_width))
            x = _res_matmul(jnp.concatenate(zs, axis=0).astype(BF16), odd_w_out[j].astype(BF16), x, mod3, 2, seg, tm)
        q, hm = _nm_matmul(x, norm2[layer], mod3, 4, 3, seg, peer_wq[layer].astype(BF16), tm)
        i1, i2, gate = _peer_route(q, peer_keys[layer])
        gmat = _gate_matrix(i1, i2, gate, nkeys)
        x = _peer_experts(hm, gmat, peer_u[layer].astype(BF16), peer_v[layer].astype(BF16),
                          x, mod3, 5, seg, tm)

    y_prompt = x[:tp].reshape(bp, sp, d)
    y_sample = x[tp:].reshape(bs, ss, d)
    return (y_prompt, y_sample, jnp.stack(new_k, axis=1), jnp.stack(new_v, axis=1), jnp.stack(new_s, axis=1))
```

```python
import functools
import math

import numpy as np
import jax
import jax.numpy as jnp
from jax import lax
from jax.experimental import pallas as pl
from jax.experimental.pallas import tpu as pltpu

F32 = jnp.float32
BF16 = jnp.bfloat16
I32 = jnp.int32

NORM_EPS = 1e-6
A_HEAD_DIM = 64
A_GN_EPS = 64e-5
B_GROUP = 4
GRID_W = 64
ROPE_THETA = 10000.0
HYENA_ORDER = 2
DECAY_TARGET = 1e-2
FAST_DECAY_PCT = 0.3
SLOW_DECAY_PCT = 1.5
PEER_TOPK = 16
LANES = 128
VMEM_LIMIT = 56 * 1024 * 1024


def _cparams(sem):
    return pltpu.CompilerParams(dimension_semantics=sem, vmem_limit_bytes=VMEM_LIMIT)


def _row_tile(*lengths, cap=512):
    t = cap
    while any(n % t for n in lengths):
        t //= 2
    return t


def _col_tile(n, cap):
    return max(t for t in range(LANES, cap + 1, LANES) if n % t == 0)


def _mod_kernel(c_ref, w_ref, b_ref, o_ref):
    c = c_ref[...]
    s = (c * jax.nn.sigmoid(c)).astype(BF16)
    o_ref[0] = jnp.dot(s, w_ref[0].astype(BF16), preferred_element_type=F32) + b_ref[0]


def _mod_table(cond8, mod_w, mod_b):
    depth, d, n = mod_w.shape
    tn = _row_tile(n, cap=1024)
    return pl.pallas_call(
        _mod_kernel,
        grid=(depth, n // tn),
        in_specs=[pl.BlockSpec((8, d), lambda l, j: (0, 0)),
                  pl.BlockSpec((1, d, tn), lambda l, j: (l, 0, j)),
                  pl.BlockSpec((1, 1, tn), lambda l, j: (l, 0, j))],
        out_specs=pl.BlockSpec((1, 8, tn), lambda l, j: (l, 0, j)),
        out_shape=jax.ShapeDtypeStruct((depth, 8, n), F32),
        compiler_params=_cparams(("parallel", "parallel")),
        name="mod_table",
    )(cond8, mod_w, mod_b.reshape(depth, 1, n))


def _nm_matmul_kernel(seg_ref, x_ref, g_ref, sc_ref, sh_ref, w_ref, o_ref, h_ref, h_scr):
    del seg_ref

    @pl.when(pl.program_id(1) == 0)
    def _():
        x = x_ref[...]
        y = x * lax.rsqrt(jnp.mean(x * x, axis=-1, keepdims=True) + NORM_EPS) * g_ref[...]
        h = (y * (1.0 + sc_ref[0]) + sh_ref[0]).astype(BF16)
        h_scr[...] = h
        h_ref[...] = h

    o_ref[...] = jnp.dot(h_scr[...], w_ref[...], preferred_element_type=F32).astype(o_ref.dtype)


def _nm_matmul(x, g, mod3, sc_idx, sh_idx, seg, w_bf16, tm, out_dtype=F32):
    t, d = x.shape
    n = w_bf16.shape[1]
    tn = _col_tile(n, 1664)
    grid_spec = pltpu.PrefetchScalarGridSpec(
        num_scalar_prefetch=1,
        grid=(t // tm, n // tn),
        in_specs=[pl.BlockSpec((tm, d), lambda i, j, s: (i, 0)),
                  pl.BlockSpec((1, d), lambda i, j, s: (0, 0)),
                  pl.BlockSpec((1, 1, d), lambda i, j, s: (s[i] * 6 + sc_idx, 0, 0)),
                  pl.BlockSpec((1, 1, d), lambda i, j, s: (s[i] * 6 + sh_idx, 0, 0)),
                  pl.BlockSpec((d, tn), lambda i, j, s: (0, j))],
        out_specs=[pl.BlockSpec((tm, tn), lambda i, j, s: (i, j)),
                   pl.BlockSpec((tm, d), lambda i, j, s: (i, 0))],
        scratch_shapes=[pltpu.VMEM((tm, d), BF16)])
    return pl.pallas_call(
        _nm_matmul_kernel,
        grid_spec=grid_spec,
        out_shape=[jax.ShapeDtypeStruct((t, n), out_dtype), jax.ShapeDtypeStruct((t, d), BF16)],
        compiler_params=_cparams(("parallel", "arbitrary")),
        name="norm_mod_matmul",
    )(seg, x, g.reshape(1, d), mod3, mod3, w_bf16)


def _res_matmul_kernel(seg_ref, a_ref, w_ref, r_ref, gt_ref, o_ref):
    del seg_ref
    mm = jnp.dot(a_ref[...], w_ref[...], preferred_element_type=F32)
    o_ref[...] = r_ref[...] + gt_ref[0] * mm


def _res_matmul(a_bf16, w_bf16, res, mod3, gt_idx, seg, tm):
    t, k = a_bf16.shape
    n = w_bf16.shape[1]
    tn = _col_tile(n, 1024)
    grid_spec = pltpu.PrefetchScalarGridSpec(
        num_scalar_prefetch=1,
        grid=(t // tm, n // tn),
        in_specs=[pl.BlockSpec((tm, k), lambda i, j, s: (i, 0)),
                  pl.BlockSpec((k, tn), lambda i, j, s: (0, j)),
                  pl.BlockSpec((tm, tn), lambda i, j, s: (i, j)),
                  pl.BlockSpec((1, 1, tn), lambda i, j, s: (s[i] * 6 + gt_idx, 0, j))],
        out_specs=pl.BlockSpec((tm, tn), lambda i, j, s: (i, j)))
    return pl.pallas_call(
        _res_matmul_kernel,
        grid_spec=grid_spec,
        out_shape=jax.ShapeDtypeStruct((t, n), F32),
        compiler_params=_cparams(("parallel", "parallel")),
        name="res_matmul",
    )(seg, a_bf16, w_bf16, res, mod3)


def _scan_kernel(rf_ref, kkf_ref, vf_ref, wf_ref, bf_ref, ktf_ref, rb_ref, kkb_ref, vb_ref, wb_ref, bb_ref,
                 ktb_ref, s0_ref, yf_ref, yb_ref, sfin_ref, s_scr, *, tb_steps, nk, kq):
    ti = pl.program_id(1)

    @pl.when(ti == 0)
    def _():
        s_scr[...] = s0_ref[...]

    def tree(parts):
        while len(parts) > 1:
            parts = [parts[i] + parts[i + 1] for i in range(0, len(parts), 2)]
        return parts[0]

    def all_parts(p):
        part = LANES // kq
        return tree([p] + [pltpu.roll(p, i * part, 1) for i in range(1, kq)])

    nacc = 4

    def advance(d, t, r_ref, kk_ref, v_ref, w_ref, b_ref, kt_ref, y_ref):
        row = pl.ds(t, 1)
        accs = [None] * nacc
        for k in range(nk):
            p = s_scr[d, k] * kk_ref[k, row, :]
            accs[k % nacc] = p if accs[k % nacc] is None else accs[k % nacc] + p
        sa = all_parts(tree(accs))
        v = v_ref[t]
        yacc = [None] * nacc
        for k in range(nk):
            s_new = s_scr[d, k] * w_ref[k, row, :] - sa * b_ref[k, row, :] + v * kt_ref[k, row, :]
            s_scr[d, k] = s_new
            p = s_new * r_ref[k, row, :]
            yacc[k % nacc] = p if yacc[k % nacc] is None else yacc[k % nacc] + p
        y_ref[t] = all_parts(tree(yacc))

    def step(s, carry):
        advance(0, s, rf_ref, kkf_ref, vf_ref, wf_ref, bf_ref, ktf_ref, yf_ref)
        advance(1, tb_steps - 1 - s, rb_ref, kkb_ref, vb_ref, wb_ref, bb_ref, ktb_ref, yb_ref)
        return carry

    lax.fori_loop(0, tb_steps, step, 0)

    @pl.when(ti == pl.num_programs(1) - 1)
    def _():
        sfin_ref[...] = s_scr[...]


def _scan(r, kk, v, w2, b2, kt2, s0, kq):
    nk, seq, c = r.shape
    nv = v.shape[1]
    tb_steps = _row_tile(seq, cap=16)
    nt = seq // tb_steps
    rowf = pl.BlockSpec((nk, tb_steps, LANES), lambda gi, ti: (0, ti, gi))
    rowb = pl.BlockSpec((nk, tb_steps, LANES), lambda gi, ti: (0, nt - 1 - ti, gi))
    valf = pl.BlockSpec((tb_steps, nv, LANES), lambda gi, ti: (ti, 0, gi))
    valb = pl.BlockSpec((tb_steps, nv, LANES), lambda gi, ti: (nt - 1 - ti, 0, gi))
    st = pl.BlockSpec((2, nk, nv, LANES), lambda gi, ti: (0, 0, 0, gi))
    return pl.pallas_call(
        functools.partial(_scan_kernel, tb_steps=tb_steps, nk=nk, kq=kq),
        grid=(c // LANES, nt),
        in_specs=[rowf, rowf, valf, rowf, rowf, rowf, rowb, rowb, valb, rowb, rowb, rowb, st],
        out_specs=[valf, valb, st],
        out_shape=[jax.ShapeDtypeStruct((seq, nv, c), F32), jax.ShapeDtypeStruct((seq, nv, c), F32),
                   jax.ShapeDtypeStruct((2, nk, nv, c), F32)],
        scratch_shapes=[pltpu.VMEM((2, nk, nv, LANES), F32)],
        compiler_params=_cparams(("parallel", "arbitrary")),
        name="rwkv7_scan",
    )(r, kk, v, w2[0], b2[0], kt2[0], r, kk, v, w2[1], b2[1], kt2[1], s0)


CHAIN_TB = 128


def _to_chain_kernel(x_ref, o_ref, a_scr, *stage, heads, nk, kq, values):
    bg, tb = x_ref.shape[0], x_ref.shape[1]
    n = A_HEAD_DIM
    for b in range(bg):
        xt = x_ref[b].T
        a_scr[b] = jnp.swapaxes(xt.reshape(heads, n, tb), 0, 1)
    for p in range(n if values else nk):
        pieces = [a_scr[b, p if values else q * nk + p] for q in range(kq) for b in range(bg)]
        tile = jnp.concatenate(pieces, axis=0).T
        if values:
            stage[0][p] = tile
        else:
            o_ref[p] = tile
    if values:
        o_ref[...] = jnp.swapaxes(stage[0][...], 0, 1)


def _to_chain(x, row0, bn, seq, kq, values):
    w_ = x.shape[-1]
    heads = w_ // A_HEAD_DIM
    n = A_HEAD_DIM
    nk = n // kq
    bg = LANES // (kq * heads)
    ng = bn // bg
    c = kq * bn * heads
    assert kq * bg * heads == LANES and c == ng * LANES and row0 % (bg * seq) == 0
    tb = min(CHAIN_TB, seq)
    g0 = row0 // (bg * seq)
    scratch = [pltpu.VMEM((bg, n, heads, tb), F32)]
    if values:
        out_spec = pl.BlockSpec((tb, n, LANES), lambda g, t: (t, 0, g))
        out_shape = jax.ShapeDtypeStruct((seq, n, c), F32)
        scratch.append(pltpu.VMEM((n, tb, LANES), F32))
    else:
        out_spec = pl.BlockSpec((nk, tb, LANES), lambda g, t: (0, t, g))
        out_shape = jax.ShapeDtypeStruct((nk, seq, c), F32)
    return pl.pallas_call(
        functools.partial(_to_chain_kernel, heads=heads, nk=nk, kq=kq, values=values),
        grid=(ng, seq // tb),
        in_specs=[pl.BlockSpec((bg, tb, w_), lambda g, t: (g0 + g, t, 0))],
        out_specs=out_spec,
        out_shape=out_shape,
        scratch_shapes=scratch,
        compiler_params=_cparams(("parallel", "parallel")),
        name="to_chain_layout",
    )(x.reshape(-1, seq, w_))


def _from_chain_kernel(y_ref, o_ref, stage, a_scr, *, heads):
    bg = o_ref.shape[0]
    n = A_HEAD_DIM
    tb = y_ref.shape[0]
    stage[...] = jnp.swapaxes(y_ref[...], 0, 1)
    for v in range(n):
        rows = stage[v].T
        for b in range(bg):
            a_scr[b, v] = rows[b * heads:(b + 1) * heads]
    for b in range(bg):
        o_ref[b] = jnp.swapaxes(a_scr[b], 0, 1).reshape(heads * n, tb).T


def _from_chain(y, bn, heads, kq):
    seq, n, c = y.shape
    bg = LANES // (kq * heads)
    tb = min(CHAIN_TB, seq)
    out = pl.pallas_call(
        functools.partial(_from_chain_kernel, heads=heads),
        grid=(c // LANES, seq // tb),
        in_specs=[pl.BlockSpec((tb, n, LANES), lambda g, t: (t, 0, g))],
        out_specs=pl.BlockSpec((bg, tb, heads * n), lambda g, t: (g, t, 0)),
        out_shape=jax.ShapeDtypeStruct((bn, seq, heads * n), F32),
        scratch_shapes=[pltpu.VMEM((n, tb, LANES), F32), pltpu.VMEM((bg, n, heads, tb), F32)],
        compiler_params=_cparams(("parallel", "parallel")),
        name="from_chain_layout",
    )(y)
    return out.reshape(bn * seq, heads * n)


def _attn_kernel(q_ref, k_ref, v_ref, o_ref, *, dh):
    k = k_ref[0]
    v = v_ref[0]
    for g in range(B_GROUP):
        q = q_ref[0, :, g * dh:(g + 1) * dh]
        s = lax.dot_general(q, k, (((1,), (1,)), ((), ())), preferred_element_type=F32)
        m = jnp.max(s, axis=-1, keepdims=True)
        p = jnp.exp(s - m)
        den = jnp.sum(p, axis=-1, keepdims=True)
        o = jnp.dot(p.astype(BF16), v, preferred_element_type=F32) / den
        o_ref[0, :, g * dh:(g + 1) * dh] = o.astype(o_ref.dtype)


def _attend(q, k, v, dh):
    bn, lq, qw = q.shape
    lk = k.shape[1]
    kvh = k.shape[2] // dh
    gw = B_GROUP * dh
    tq = _row_tile(lq, cap=256)
    return pl.pallas_call(
        functools.partial(_attn_kernel, dh=dh),
        grid=(bn, kvh, lq // tq),
        in_specs=[pl.BlockSpec((1, tq, gw), lambda b, h, i: (b, i, h)),
                  pl.BlockSpec((1, lk, dh), lambda b, h, i: (b, 0, h)),
                  pl.BlockSpec((1, lk, dh), lambda b, h, i: (b, 0, h))],
        out_specs=pl.BlockSpec((1, tq, gw), lambda b, h, i: (b, i, h)),
        out_shape=jax.ShapeDtypeStruct((bn, lq, qw), BF16),
        compiler_params=_cparams(("parallel", "parallel", "parallel")),
        name="attention",
    )(q, k, v)


G_GROUP = 16


def _gate_matrix_kernel(i1_ref, i2_ref, g_ref, o_ref, gtmp_scr, *, nkeys):
    iota = lax.broadcasted_iota(I32, (nkeys, nkeys), 0)

    def build(grp, carry):
        base = pl.multiple_of(grp * G_GROUP, G_GROUP)
        def token(tt, c2):
            t = base + tt
            a_t = jnp.where(iota == i1_ref[pl.ds(t, 1), :], 1.0, 0.0).astype(BF16)
            b_t = jnp.where(iota == i2_ref[pl.ds(t, 1), :], g_ref[pl.ds(t, 1), :], 0.0).astype(BF16)
            gtmp_scr[tt] = lax.dot_general(a_t, b_t, (((1,), (1,)), ((), ())), preferred_element_type=F32)
            return c2

        lax.fori_loop(0, G_GROUP, token, 0, unroll=G_GROUP)
        by_n1 = jnp.swapaxes(gtmp_scr[...], 0, 1).astype(BF16)
        for n1 in range(nkeys):
            o_ref[pl.ds(base, G_GROUP), n1 * nkeys:(n1 + 1) * nkeys] = by_n1[n1]
        return carry

    lax.fori_loop(0, o_ref.shape[0] // G_GROUP, build, 0)


def _gate_matrix(i1, i2, gate, nkeys):
    t, nj = i1.shape
    tb = LANES
    sel = pl.BlockSpec((tb, nj), lambda i: (i, 0))
    return pl.pallas_call(
        functools.partial(_gate_matrix_kernel, nkeys=nkeys),
        grid=(t // tb,),
        in_specs=[sel, sel, sel],
        out_specs=pl.BlockSpec((tb, nkeys * nkeys), lambda i: (i, 0)),
        out_shape=jax.ShapeDtypeStruct((t, nkeys * nkeys), BF16),
        scratch_shapes=[pltpu.VMEM((G_GROUP, nkeys, nkeys), F32)],
        compiler_params=_cparams(("parallel",)),
        name="peer_gate_matrix",
    )(i1, i2, gate)


def _peer_kernel(seg_ref, xb_ref, gm_ref, u_ref, v_ref, r_ref, gt_ref, o_ref, acc_scr):
    del seg_ref
    e = pl.program_id(1)

    @pl.when(e == 0)
    def _():
        acc_scr[...] = jnp.zeros_like(acc_scr)

    h = lax.dot_general(xb_ref[...], u_ref[...], (((1,), (1,)), ((), ())), preferred_element_type=F32)
    act = 0.5 * h * (1.0 + lax.erf(h * (1.0 / math.sqrt(2.0)))) * gm_ref[...].astype(F32)
    acc_scr[...] += jnp.dot(act.astype(BF16), v_ref[...], preferred_element_type=F32)

    @pl.when(e == pl.num_programs(1) - 1)
    def _():
        o_ref[...] = r_ref[...] + gt_ref[0] * acc_scr[...]


def _peer_experts(xb, gmat, u_bf16, v_bf16, res, mod3, gt_idx, seg, tm):
    t, d = xb.shape
    ne = u_bf16.shape[0]
    te = 1024
    grid_spec = pltpu.PrefetchScalarGridSpec(
        num_scalar_prefetch=1,
        grid=(t // tm, ne // te),
        in_specs=[pl.BlockSpec((tm, d), lambda i, e, s: (i, 0)),
                  pl.BlockSpec((tm, te), lambda i, e, s: (i, e)),
                  pl.BlockSpec((te, d), lambda i, e, s: (e, 0)),
                  pl.BlockSpec((te, d), lambda i, e, s: (e, 0)),
                  pl.BlockSpec((tm, d), lambda i, e, s: (i, 0)),
                  pl.BlockSpec((1, 1, d), lambda i, e, s: (s[i] * 6 + gt_idx, 0, 0))],
        out_specs=pl.BlockSpec((tm, d), lambda i, e, s: (i, 0)),
        scratch_shapes=[pltpu.VMEM((tm, d), F32)])
    return pl.pallas_call(
        _peer_kernel,
        grid_spec=grid_spec,
        out_shape=jax.ShapeDtypeStruct((t, d), F32),
        compiler_params=_cparams(("parallel", "arbitrary")),
        name="peer_experts",
    )(seg, xb, gmat, u_bf16, v_bf16, res, mod3)


def _topk_rows(vals, payload, rows_out):
    big = jnp.float32(2 ** 30)
    top_v = jnp.zeros(rows_out.shape, F32)
    top_p = jnp.zeros(rows_out.shape, F32)
    for it in range(PEER_TOPK):
        m = jnp.max(vals, axis=0, keepdims=True)
        sel = jnp.min(jnp.where(vals == m, payload, big), axis=0, keepdims=True)
        top_v = jnp.where(rows_out == it, m, top_v)
        top_p = jnp.where(rows_out == it, sel, top_p)
        vals = jnp.where(payload == sel, -jnp.inf, vals)
    return top_v, top_p


def _gather_rows(table, sel):
    out = jnp.zeros(sel.shape, table.dtype)
    for i in range(PEER_TOPK):
        out = jnp.where(sel == i, table[i:i + 1, :], out)
    return out


def _route_kernel(q_ref, keys_ref, i1_ref, i2_ref, g_ref, n1_scr, n2_scr, gate_scr, *, heads, nkeys, dq):
    tt = q_ref.shape[0]
    k = PEER_TOPK
    half = k // 2
    n_iota = lax.broadcasted_iota(I32, (nkeys, tt), 0).astype(F32)
    rows_out = lax.broadcasted_iota(I32, (k, tt), 0)
    r = lax.broadcasted_iota(I32, (half * k + half, tt), 0)
    pair_id = jnp.where(r < half * k, r, (r - half * k + half) * k).astype(F32)

    def head(h, carry):
        tops = []
        for c in range(2):
            col = pl.multiple_of((h * 2 + c) * dq, dq)
            qhc = q_ref[:, pl.ds(col, dq)].astype(BF16)
            khc = keys_ref[h, c].astype(BF16)
            s = lax.dot_general(khc, qhc, (((1,), (1,)), ((), ())), preferred_element_type=F32)
            tops.append(_topk_rows(s, n_iota, rows_out))
        (s1, i1), (s2, i2) = tops
        cand = jnp.concatenate([s1[i:i + 1, :] + s2 for i in range(half)] + [s1[half:, :] + s2[0:1, :]], axis=0)
        top, ci = _topk_rows(cand, pair_id, rows_out)
        ci = ci.astype(I32)
        n1 = _gather_rows(i1, lax.shift_right_logical(ci, 4)).astype(I32)
        n2 = _gather_rows(i2, lax.bitwise_and(ci, k - 1)).astype(I32)
        ex = jnp.exp(top - top[0:1, :])
        gate = ex / jnp.sum(ex, axis=0, keepdims=True)
        row = pl.multiple_of(h * k, k)
        n1_scr[pl.ds(row, k), :] = n1
        n2_scr[pl.ds(row, k), :] = n2
        gate_scr[pl.ds(row, k), :] = gate
        return carry

    lax.fori_loop(0, heads, head, 0)
    i1_ref[...] = n1_scr[...].T
    i2_ref[...] = n2_scr[...].T
    g_ref[...] = gate_scr[...].T


def _peer_route(q, sub_keys):
    t, qw = q.shape
    heads, _, nkeys, dq = sub_keys.shape
    hk = heads * PEER_TOPK
    tt = LANES
    out = pl.BlockSpec((tt, hk), lambda i: (i, 0))
    return pl.pallas_call(
        functools.partial(_route_kernel, heads=heads, nkeys=nkeys, dq=dq),
        grid=(t // tt,),
        in_specs=[pl.BlockSpec((tt, qw), lambda i: (i, 0)),
                  pl.BlockSpec((heads, 2, nkeys, dq), lambda i: (0, 0, 0, 0))],
        out_specs=[out, out, out],
        out_shape=[jax.ShapeDtypeStruct((t, hk), I32), jax.ShapeDtypeStruct((t, hk), I32),
                   jax.ShapeDtypeStruct((t, hk), F32)],
        scratch_shapes=[pltpu.VMEM((hk, tt), I32), pltpu.VMEM((hk, tt), I32), pltpu.VMEM((hk, tt), F32)],
        compiler_params=_cparams(("parallel",)),
        name="peer_route",
    )(q, sub_keys)


def _split(x):
    hi = x.astype(BF16)
    return hi, (x - hi.astype(F32)).astype(BF16)


def _mm(a, b):
    return jnp.dot(a, b, preferred_element_type=F32)


def _head_sum(x, ones_blockdiag):
    hi, lo = _split(x)
    return _mm(hi, ones_blockdiag) + _mm(lo, ones_blockdiag)


def _shift_rows(x, first_row, last_row):
    n = x.shape[0]
    rows = lax.broadcasted_iota(I32, x.shape, 0)
    prev = jnp.where(rows == 0, first_row, pltpu.roll(x, 1, 0))
    nxt = jnp.where(rows == n - 1, last_row, pltpu.roll(x, n - 1, 0))
    return prev, nxt


def _conv3(x, first_row, last_row, taps):
    prev, nxt = _shift_rows(x, first_row, last_row)
    return prev * taps[0:1] + x * taps[1:2] + nxt * taps[2:3]


def _rwkv_pre_kernel(r_ref, k_ref, v_ref, low_ref, pr_ref, pk_ref, pv_ref, plow_ref, nr_ref, nk_ref, nv_ref,
                     nlow_ref, cr_ref, ck_ref, cv_ref, clow_ref, w0_ref, wu_ref, a0_ref, au_ref, gu_ref,
                     kkg_ref, ka_ref, rk_ref, ones_ref,
                     ro_ref, kko_ref, vo_ref, w0o_ref, w1o_ref, b0o_ref, b1o_ref, kt0o_ref, kt1o_ref,
                     go_ref, bonus_ref, *, rw, ra):
    r = _conv3(r_ref[...], pr_ref[0], nr_ref[0], cr_ref[...])
    k = _conv3(k_ref[...], pk_ref[0], nk_ref[0], ck_ref[...])
    v = _conv3(v_ref[...], pv_ref[0], nv_ref[0], cv_ref[...])
    low = _conv3(low_ref[...], plow_ref[0], nlow_ref[0], clow_ref[...])
    ones = ones_ref[...]
    kk = k * kkg_ref[...]
    kk = kk * lax.rsqrt(_head_sum(kk * kk, ones) + 1e-12)
    ro_ref[...] = r
    kko_ref[...] = kk
    vo_ref[...] = v
    bonus_ref[...] = _head_sum(r * k * rk_ref[...], ones) * v
    gd = low[:, 2 * rw + 2 * ra:]
    go_ref[...] = _mm(jax.nn.sigmoid(gd).astype(BF16), gu_ref[...])
    for d_, (wo, bo, kto) in enumerate(((w0o_ref, b0o_ref, kt0o_ref), (w1o_ref, b1o_ref, kt1o_ref))):
        wd = low[:, d_ * rw:(d_ + 1) * rw]
        ad = low[:, 2 * rw + d_ * ra:2 * rw + (d_ + 1) * ra]
        lw = w0_ref[d_:d_ + 1, :] + _mm(jnp.tanh(wd).astype(BF16), wu_ref[d_])
        softplus = jnp.maximum(-lw, 0.0) + jnp.log1p(jnp.exp(-jnp.abs(lw)))
        wo[...] = jnp.exp(-jnp.exp(-softplus - 0.5))
        a = jax.nn.sigmoid(a0_ref[d_:d_ + 1, :] + _mm(ad.astype(BF16), au_ref[d_]))
        bo[...] = kk * a
        kto[...] = k * (1.0 + (a - 1.0) * ka_ref[...])


def _rwkv_pre(z, prev_rows, next_rows, P, tm, offs):
    t = z.shape[0]
    w_ = P['a_w0'].shape[-1]
    rw, ra = P['a_wu'].shape[1], P['a_au'].shape[1]
    lw = offs['low_w']
    li = offs['low'] // lw
    row = lambda c, wd: pl.BlockSpec((tm, wd), lambda i: (i, c))
    edge = lambda c, wd: pl.BlockSpec((1, 1, wd), lambda i: (i, 0, c))
    full = lambda a: pl.BlockSpec(a.shape, lambda i: (0,) * a.ndim)
    conv = P['a_conv']
    consts = [conv[:, :w_], conv[:, w_:2 * w_], conv[:, 2 * w_:3 * w_], conv[:, 3 * w_:],
              P['a_w0'], P['a_wu'].astype(BF16), P['a_a0'], P['a_au'].astype(BF16), P['a_gu'].astype(BF16),
              P['a_kk'].reshape(1, w_), P['a_ka'].reshape(1, w_), P['a_rk'].reshape(1, w_),
              jnp.asarray(_blockdiag_ones(w_), BF16)]
    out = pl.BlockSpec((tm, w_), lambda i: (i, 0))
    return pl.pallas_call(
        functools.partial(_rwkv_pre_kernel, rw=rw, ra=ra),
        grid=(t // tm,),
        in_specs=[row(0, w_), row(1, w_), row(2, w_), row(li, lw),
                  edge(0, w_), edge(1, w_), edge(2, w_), edge(li, lw),
                  edge(0, w_), edge(1, w_), edge(2, w_), edge(li, lw)] + [full(a) for a in consts],
        out_specs=[out] * 11,
        out_shape=[jax.ShapeDtypeStruct((t, w_), F32)] * 11,
        compiler_params=_cparams(("parallel",)),
        name="rwkv7_pre",
    )(z, z, z, z, prev_rows, prev_rows, prev_rows, prev_rows, next_rows, next_rows, next_rows, next_rows, *consts)


def _blockdiag_ones(width):
    idx = np.arange(width) // A_HEAD_DIM
    return (idx[:, None] == idx[None, :]).astype(np.float32)


def _rwkv_post_kernel(yf_ref, yb_ref, bonus_ref, g_ref, lnw_ref, lnb_ref, ones_ref, o_ref):
    ones = ones_ref[...]
    y = yf_ref[...] + yb_ref[...]
    mu = _head_sum(y, ones) * (1.0 / A_HEAD_DIM)
    dlt = y - mu
    var = _head_sum(dlt * dlt, ones) * (1.0 / A_HEAD_DIM)
    yn = dlt * lax.rsqrt(var + A_GN_EPS) * lnw_ref[...] + lnb_ref[...]
    o_ref[...] = ((yn + bonus_ref[...]) * g_ref[...]).astype(o_ref.dtype)


def _rwkv_post(yf, yb, bonus, g, row0, P, tm):
    t, w_ = yf.shape
    r0 = row0 // tm
    row = pl.BlockSpec((tm, w_), lambda i: (i, 0))
    off = pl.BlockSpec((tm, w_), lambda i: (r0 + i, 0))
    vec = pl.BlockSpec((1, w_), lambda i: (0, 0))
    return pl.pallas_call(
        _rwkv_post_kernel,
        grid=(t // tm,),
        in_specs=[row, row, off, off, vec, vec, pl.BlockSpec((w_, w_), lambda i: (0, 0))],
        out_specs=row,
        out_shape=jax.ShapeDtypeStruct((t, w_), BF16),
        compiler_params=_cparams(("parallel",)),
        name="rwkv7_post",
    )(yf, yb, bonus, g, P['a_ln_w'].reshape(1, w_), P['a_ln_b'].reshape(1, w_),
      jnp.asarray(_blockdiag_ones(w_), BF16))


def _attn_pre_kernel(q_ref, kv_ref, qn_ref, kn_ref, cos_ref, sin_ref, qo_ref, ko_ref, vo_ref, kf_ref, vf_ref,
                     *, dh, latent):
    def rms(x, g):
        return x * lax.rsqrt(jnp.mean(x * x, axis=-1, keepdims=True) + NORM_EPS) * g

    def rope(x):
        if not latent:
            return x
        lanes = lax.broadcasted_iota(I32, x.shape, 1)
        quarter = dh // 4
        partner = jnp.where(lanes % (2 * quarter) < quarter, pltpu.roll(x, dh - quarter, 1), pltpu.roll(x, quarter, 1))
        return x * cos_ref[...] + partner * sin_ref[...]

    nq = q_ref.shape[1] // dh
    nkv = kv_ref.shape[1] // (2 * dh)
    for h in range(nq):
        q = rope(rms(q_ref[:, h * dh:(h + 1) * dh], qn_ref[...]))
        qo_ref[:, h * dh:(h + 1) * dh] = (q * dh ** -0.5).astype(BF16)
    for h in range(nkv):
        k = rms(kv_ref[:, h * dh:(h + 1) * dh], kn_ref[...])
        v = kv_ref[:, (nkv + h) * dh:(nkv + h + 1) * dh]
        kf_ref[:, h * dh:(h + 1) * dh] = k
        vf_ref[:, h * dh:(h + 1) * dh] = v
        ko_ref[:, h * dh:(h + 1) * dh] = rope(k).astype(BF16)
        vo_ref[:, h * dh:(h + 1) * dh] = v.astype(BF16)


def _rope_tables(seq, dh):
    quarter = dh // 4
    inv = ROPE_THETA ** (-np.arange(quarter, dtype=np.float64) / quarter)
    pos = np.arange(seq)
    ang_r = (pos // GRID_W)[:, None] * inv[None, :]
    ang_c = (pos % GRID_W)[:, None] * inv[None, :]
    cos = np.concatenate([np.cos(ang_r)] * 2 + [np.cos(ang_c)] * 2, axis=1)
    sin = np.concatenate([-np.sin(ang_r), np.sin(ang_r), -np.sin(ang_c), np.sin(ang_c)], axis=1)
    return jnp.asarray(cos, F32), jnp.asarray(sin, F32)


def _attn_pre(z, qnorm, knorm, offs, row0, rows, seq, tm, latent):
    dh = qnorm.shape[-1]
    bw, kvw2 = offs['q_w'], offs['kv_w']
    r0 = row0 // tm
    per_seq = seq // tm
    cos, sin = _rope_tables(seq, dh) if latent else (jnp.zeros((tm, dh), F32), jnp.zeros((tm, dh), F32))
    tab = pl.BlockSpec((tm, dh), (lambda i: (i % per_seq, 0)) if latent else (lambda i: (0, 0)))
    vec = pl.BlockSpec((1, dh), lambda i: (0, 0))
    kvo = pl.BlockSpec((tm, kvw2 // 2), lambda i: (i, 0))
    return pl.pallas_call(
        functools.partial(_attn_pre_kernel, dh=dh, latent=latent),
        grid=(rows // tm,),
        in_specs=[pl.BlockSpec((tm, bw), lambda i: (r0 + i, offs['q'] // bw)),
                  pl.BlockSpec((tm, kvw2), lambda i: (r0 + i, offs['kv'] // kvw2)),
                  vec, vec, tab, tab],
        out_specs=[pl.BlockSpec((tm, bw), lambda i: (i, 0)), kvo, kvo, kvo, kvo],
        out_shape=[jax.ShapeDtypeStruct((rows, bw), BF16), jax.ShapeDtypeStruct((rows, kvw2 // 2), BF16),
                   jax.ShapeDtypeStruct((rows, kvw2 // 2), BF16), jax.ShapeDtypeStruct((rows, kvw2 // 2), F32),
                   jax.ShapeDtypeStruct((rows, kvw2 // 2), F32)],
        compiler_params=_cparams(("parallel",)),
        name="attention_pre",
    )(z, z, qnorm.reshape(1, dh), knorm.reshape(1, dh), cos, sin)


HY_BLK = 256
HY_TILE_ELEMS = 128 * 1024


def _dft_consts():
    n = 2 * HY_BLK
    k = np.arange(HY_BLK, dtype=np.float64)[:, None] + 0.5
    s = np.arange(HY_BLK, dtype=np.float64)[None, :]
    th = 2.0 * np.pi * k * s / n
    fwd = np.concatenate([np.cos(th), -np.sin(th)], axis=0)
    tau = np.arange(n, dtype=np.float64)[:, None]
    ph = 2.0 * np.pi * tau * (np.arange(HY_BLK, dtype=np.float64)[None, :] + 0.5) / n
    inv = np.concatenate([np.cos(ph), -np.sin(ph)], axis=1) * (2.0 / n)
    inv_cat = np.concatenate([inv[:HY_BLK], inv[HY_BLK:]], axis=1)

    def hl(a):
        a32 = jnp.asarray(a, F32)
        hi = a32.astype(BF16)
        return hi, (a32 - hi.astype(F32)).astype(BF16)

    return hl(fwd), hl(inv_cat)


def _lag_features(seq, emb):
    bands = (emb - 1) // 2
    t = np.linspace(0.0, 1.0, seq)
    wpos = 2.0 * np.pi * np.arange(seq) / seq
    f = np.linspace(1e-4, bands - 1, bands)
    z = np.concatenate([t[:, None], np.cos(f[None, :] * wpos[:, None]), -np.sin(f[None, :] * wpos[:, None])], axis=1)
    lag = np.concatenate([np.zeros(1, np.int64), np.arange(seq - 1, 0, -1), np.arange(seq)])
    return jnp.asarray(z[lag], F32)


def _hyena_filter_kernel(z_ref, fw1_ref, fb1_ref, freq_ref, fw2_ref, fb2_ref, w3b_ref, w3f_ref, dl_ref,
                         fh_ref, fl_ref, g_ref, hdn_scr, f_scr, *, seq):
    @pl.when((pl.program_id(0) == 0) & (pl.program_id(1) == 0))
    def _():
        h1 = jnp.sin(freq_ref[...] * (_mm(z_ref[...].astype(BF16), fw1_ref[...]) + fb1_ref[...]))
        hdn_scr[...] = jnp.sin(freq_ref[...] * (_mm(h1.astype(BF16), fw2_ref[...]) + fb2_ref[...])).astype(BF16)

    decay = jnp.exp(-z_ref[:, 0:1] * dl_ref[...])
    f_scr[0:seq, :] = _mm(hdn_scr[0:seq, :], w3b_ref[...]) * decay[0:seq, :]
    f_scr[seq:2 * seq, :] = _mm(hdn_scr[seq:2 * seq, :], w3f_ref[...]) * decay[seq:2 * seq, :]
    f = f_scr[...]
    scale = lax.rsqrt(jnp.sum(f * f, axis=0, keepdims=True) + 1e-12)
    rows = lax.broadcasted_iota(I32, f.shape, 0)
    f_scr[...] = jnp.where(rows == 0, 0.0, f * scale)
    for m in range(2 * seq // HY_BLK):
        hi, lo = _split(f_scr[m * HY_BLK:(m + 1) * HY_BLK, :])
        g_ref[0, m] = _mm(fh_ref[...], hi) + (_mm(fh_ref[...], lo) + _mm(fl_ref[...], hi))


def _hyena_spectra(seq, P, c_width, tc):
    emb, hid = P['c_fw1'].shape
    z = _lag_features(seq, emb)
    embp = 64
    z = jnp.pad(z, ((0, 0), (0, embp - emb)))
    fw1 = jnp.pad(P['c_fw1'], ((0, embp - emb), (0, 0))).astype(BF16)
    deltas = jnp.asarray(np.abs(np.linspace(math.log(DECAY_TARGET) / SLOW_DECAY_PCT,
                                            math.log(DECAY_TARGET) / FAST_DECAY_PCT, c_width)), F32).reshape(1, c_width)
    (fh, fl), _ = _dft_consts()
    nct = c_width // tc
    nseg = 2 * seq // HY_BLK
    full = lambda a: pl.BlockSpec(a.shape, lambda o, j: (0,) * a.ndim)
    w3 = P['c_fw3'].astype(BF16)
    consts = [z, fw1, P['c_fb1'].reshape(1, hid), P['c_freq'].reshape(1, hid), P['c_fw2'].astype(BF16),
              P['c_fb2'].reshape(1, hid)]
    return pl.pallas_call(
        functools.partial(_hyena_filter_kernel, seq=seq),
        grid=(HYENA_ORDER, nct),
        in_specs=[full(a) for a in consts] + [
            pl.BlockSpec((hid, tc), lambda o, j: (0, (o * 2 + 1) * nct + j)),
            pl.BlockSpec((hid, tc), lambda o, j: (0, (o * 2) * nct + j)),
            pl.BlockSpec((1, tc), lambda o, j: (0, j)), full(fh), full(fl)],
        out_specs=pl.BlockSpec((1, nseg, 2 * HY_BLK, tc), lambda o, j: (o, 0, 0, j)),
        out_shape=jax.ShapeDtypeStruct((HYENA_ORDER, nseg, 2 * HY_BLK, c_width), F32),
        scratch_shapes=[pltpu.VMEM((2 * seq, hid), BF16), pltpu.VMEM((2 * seq, tc), F32)],
        compiler_params=_cparams(("arbitrary", "arbitrary")),
        name="hyena_spectra",
    )(*consts, w3, w3, deltas, fh, fl)


def _hyena_conv_kernel(zin_ref, gate_ref, tz_ref, bz_ref, tg_ref, bg_ref, bias_ref, g_ref, fh_ref, fl_ref,
                       ih_ref, il_ref, o_ref, z_scr, gate_scr, u_scr, y_scr, *, nb, conv_in):
    half = HY_BLK
    zero = jnp.zeros((1, zin_ref.shape[-1]), F32)
    z = zin_ref[0]
    if conv_in:
        z = _conv3(z, zero, zero, tz_ref[...]) + bz_ref[...]
    z_scr[...] = z
    for j in range(nb):
        hi, lo = _split(z_scr[j * half:(j + 1) * half, :])
        u_scr[j] = _mm(fh_ref[...], hi) + (_mm(fh_ref[...], lo) + _mm(fl_ref[...], hi))

    width = zin_ref.shape[-1]
    chunk = max(8, 32 * LANES // width)

    def spectra(pieces):
        for c in range(half // chunk):
            re = pl.ds(c * chunk, chunk)
            im = pl.ds(half + c * chunk, chunk)

            def add_block(j, acc):
                ur, ui = u_scr[j, re, :], u_scr[j, im, :]
                out = []
                for a, ii in enumerate(pieces):
                    m = ii - 1 - j + nb
                    gr, gi = g_ref[0, m, re, :], g_ref[0, m, im, :]
                    out.append((acc[a][0] + (ur * gr - ui * gi), acc[a][1] + (ur * gi + ui * gr)))
                return tuple(out)

            zero = jnp.zeros((chunk, width), F32)
            acc = lax.fori_loop(0, nb, add_block, tuple((zero, zero) for _ in pieces), unroll=min(4, nb))
            for a, ii in enumerate(pieces):
                y_scr[ii, re, :] = acc[a][0]
                y_scr[ii, im, :] = acc[a][1]

    def pair(p, carry):
        spectra([2 * p, 2 * p + 1])
        return carry

    lax.fori_loop(0, (nb + 1) // 2, pair, 0)
    if (nb + 1) % 2:
        spectra([nb])

    gate_scr[...] = _conv3(gate_ref[0], zero, zero, tg_ref[...]) + bg_ref[...]

    def block(i, carry):
        rows = pl.ds(pl.multiple_of(i * half, half), half)
        ycat = jnp.concatenate([y_scr[i + 1], y_scr[i]], axis=0)
        hi, lo = _split(ycat)
        conv = _mm(ih_ref[...], hi) + (_mm(ih_ref[...], lo) + _mm(il_ref[...], hi))
        o_ref[0, rows, :] = gate_scr[rows, :] * (conv + bias_ref[...] * z_scr[rows, :])
        return carry

    lax.fori_loop(0, nb, block, 0)


def _hyena_conv(zin, zin_col0, gate_src, gate_col0, taps, tap_bias, bias, spectra, order, row0, nseq, seq, tc, conv_in):
    c_width = bias.shape[-1]
    nb = seq // HY_BLK
    nct = c_width // tc
    (fh, fl), (ih, il) = _dft_consts()
    s0 = row0 // seq
    zc, gc = zin_col0 // tc, gate_col0 // tc
    tapc = (zc if conv_in else gc)
    full = lambda a: pl.BlockSpec(a.shape, lambda j, b: (0,) * a.ndim)
    zin3 = zin.reshape(-1, seq, zin.shape[-1])
    gate3 = gate_src.reshape(-1, seq, gate_src.shape[-1])
    zs0 = s0 if conv_in else 0
    return pl.pallas_call(
        functools.partial(_hyena_conv_kernel, nb=nb, conv_in=conv_in),
        grid=(nct, nseq),
        in_specs=[pl.BlockSpec((1, seq, tc), lambda j, b: (zs0 + b, 0, zc + j)),
                  pl.BlockSpec((1, seq, tc), lambda j, b: (s0 + b, 0, gc + j)),
                  pl.BlockSpec((3, tc), lambda j, b: (0, tapc + j)),
                  pl.BlockSpec((1, tc), lambda j, b: (0, tapc + j)),
                  pl.BlockSpec((3, tc), lambda j, b: (0, gc + j)),
                  pl.BlockSpec((1, tc), lambda j, b: (0, gc + j)),
                  pl.BlockSpec((1, tc), lambda j, b: (0, j)),
                  pl.BlockSpec((1, 2 * nb, 2 * HY_BLK, tc), lambda j, b: (order, 0, 0, j)),
                  full(fh), full(fl), full(ih), full(il)],
        out_specs=pl.BlockSpec((1, seq, tc), lambda j, b: (b, 0, j)),
        out_shape=jax.ShapeDtypeStruct((nseq, seq, c_width), F32),
        scratch_shapes=[pltpu.VMEM((seq, tc), F32), pltpu.VMEM((seq, tc), F32),
                        pltpu.VMEM((nb, 2 * HY_BLK, tc), F32), pltpu.VMEM((nb + 1, 2 * HY_BLK, tc), F32)],
        compiler_params=_cparams(("arbitrary", "arbitrary")),
        name="hyena_conv",
    )(zin3, gate3, taps, tap_bias, taps, tap_bias, bias, spectra, fh, fl, ih, il)


def _rwkv_scan_pass(r, kk, v, w2, b2, kt2, s0, row0, bn, seq):
    w_ = r.shape[-1]
    heads = w_ // A_HEAD_DIM
    n = A_HEAD_DIM
    bh = bn * heads
    kq = max(1, LANES // bh)
    nk = n // kq
    c = kq * bh
    rows = lambda x: _to_chain(x, row0, bn, seq, kq, False)
    if s0 is None:
        s0r = jnp.zeros((2, nk, n, c), F32)
    else:
        s0r = jnp.transpose(s0.reshape(bn, 2, heads, n, kq, nk), (1, 5, 3, 4, 0, 2)).reshape(2, nk, n, c)
    yf, yb, sfin = _scan(rows(r), rows(kk), _to_chain(v, row0, bn, seq, kq, True),
                         (rows(w2[0]), rows(w2[1])), (rows(b2[0]), rows(b2[1])), (rows(kt2[0]), rows(kt2[1])),
                         s0r, kq)
    sfin = jnp.transpose(sfin.reshape(2, nk, n, kq, bn, heads), (4, 0, 5, 2, 3, 1)).reshape(bn, 2, heads, n, n)
    return _from_chain(yf, bn, heads, kq), _from_chain(yb, bn, heads, kq), sfin


def kernel(x_prompt, x_sample, cache_b_k, cache_b_v, state_a, c, c_ctx, mod_w, mod_b, norm1, norm2,
           even_w_in, even_a_conv, even_a_w0, even_a_wu, even_a_a0, even_a_au, even_a_gu, even_a_kk,
           even_a_ka, even_a_rk, even_a_ln_w, even_a_ln_b, even_b_qnorm, even_b_knorm, even_w_out,
           odd_w_in, odd_c_conv, odd_c_conv_b, odd_c_fw1, odd_c_fb1, odd_c_freq, odd_c_fw2, odd_c_fb2,
           odd_c_fw3, odd_c_bias, odd_w_out, peer_wq, peer_keys, peer_u, peer_v):
    bp, sp, d = x_prompt.shape
    bs, ss, _ = x_sample.shape
    depth = mod_w.shape[0]
    tp, ts = bp * sp, bs * ss
    t_all = tp + ts
    a_width = even_a_w0.shape[-1]
    a_cols = even_a_conv.shape[-1]
    dh = even_b_qnorm.shape[-1]
    b_width = d // 2
    kv_width = b_width // B_GROUP
    kvh = kv_width // dh
    c_width = odd_c_bias.shape[-1]
    nkeys = peer_keys.shape[3]
    assert bs + 1 <= 8 and nkeys == LANES and peer_keys.shape[1] * PEER_TOPK == LANES

    tm = _row_tile(tp, ss, cap=512)
    seg = jnp.concatenate([jnp.zeros((tp // tm,), I32),
                           1 + jnp.arange(ts // tm, dtype=I32) // (ss // tm)])

    cond8 = jnp.zeros((8, d), F32).at[0].set(c_ctx).at[1:1 + bs].set(c)
    mods = _mod_table(cond8, mod_w, mod_b)

    x = jnp.concatenate([x_prompt.reshape(tp, d), x_sample.reshape(ts, d)], axis=0)
    new_k, new_v, new_s = [], [], []
    for layer in range(depth):
        j = layer // 2
        mod3 = mods[layer].reshape(8 * 6, 1, d)
        if layer % 2 == 0:
            PA = dict(a_conv=even_a_conv[j], a_w0=even_a_w0[j], a_wu=even_a_wu[j], a_a0=even_a_a0[j],
                      a_au=even_a_au[j], a_gu=even_a_gu[j], a_kk=even_a_kk[j], a_ka=even_a_ka[j],
                      a_rk=even_a_rk[j], a_ln_w=even_a_ln_w[j], a_ln_b=even_a_ln_b[j])
            w_in = even_w_in[j]
            w_perm = jnp.concatenate([w_in[:, :3 * a_width], w_in[:, a_cols:], w_in[:, 3 * a_width:a_cols]],
                                     axis=1).astype(BF16)
            offs = dict(q=3 * a_width, q_w=b_width, kv=3 * a_width + b_width, kv_w=2 * kv_width,
                        low=3 * a_width + b_width + 2 * kv_width, low_w=a_cols - 3 * a_width)
            assert a_width == b_width and offs['kv'] % offs['kv_w'] == 0 and offs['low'] % offs['low_w'] == 0
            z, _ = _nm_matmul(x, norm1[layer], mod3, 1, 0, seg, w_perm, tm)
            tmr = _row_tile(sp, ss, cap=256)
            nt = t_all // tmr
            starts = np.concatenate([np.arange(0, tp, sp), tp + np.arange(0, ts, ss), [t_all]])
            tile0 = np.arange(nt) * tmr
            keep_prev = jnp.asarray(~np.isin(tile0, starts), F32)[:, None]
            keep_next = jnp.asarray(~np.isin(tile0 + tmr, starts), F32)[:, None]
            zt = z.reshape(nt, tmr, z.shape[-1])
            zero_row = jnp.zeros((1, z.shape[-1]), F32)
            prev_rows = (jnp.concatenate([zero_row, zt[:-1, tmr - 1]], axis=0) * keep_prev)[:, None, :]
            next_rows = (jnp.concatenate([zt[1:, 0], zero_row], axis=0) * keep_next)[:, None, :]
            r, kk, vv, w0, w1, b0, b1, kt0, kt1, g, bonus = _rwkv_pre(z, prev_rows, next_rows, PA, tmr, offs)
            outs = []
            for (row0, bn, seq, latent) in ((0, bp, sp, False), (tp, bs, ss, True)):
                rows = bn * seq
                q, k, v, k_f32, v_f32 = _attn_pre(z, even_b_qnorm[j], even_b_knorm[j], offs, row0, rows, seq,
                                                  tmr, latent)
                k = k.reshape(bn, seq, kv_width)
                v = v.reshape(bn, seq, kv_width)
                if latent:
                    past = cache_b_k.shape[2]
                    k = jnp.concatenate([k, cache_b_k[:, j].astype(BF16).reshape(bn, past, kv_width)], axis=1)
                    v = jnp.concatenate([v, cache_b_v[:, j].astype(BF16).reshape(bn, past, kv_width)], axis=1)
                    s0 = state_a[:, j]
                else:
                    s0 = None
                    new_k.append(k_f32.reshape(bn, seq, kvh, dh))
                    new_v.append(v_f32.reshape(bn, seq, kvh, dh))
                y_b = _attend(q.reshape(bn, seq, b_width), k, v, dh).reshape(rows, b_width)
                yf, yb, s_fin = _rwkv_scan_pass(r, kk, vv, (w0, w1), (b0, b1), (kt0, kt1), s0, row0, bn, seq)
                if not latent:
                    new_s.append(s_fin)
                y_a = _rwkv_post(yf, yb, bonus, g, row0, PA, tmr)
                outs.append(jnp.concatenate([y_a, y_b], axis=-1))
            mix_in = jnp.concatenate(outs, axis=0)
            x = _res_matmul(mix_in, even_w_out[j].astype(BF16), x, mod3, 2, seg, tm)
        else:
            PC = dict(c_fw1=odd_c_fw1[j], c_fb1=odd_c_fb1[j], c_freq=odd_c_freq[j], c_fw2=odd_c_fw2[j],
                      c_fb2=odd_c_fb2[j], c_fw3=odd_c_fw3[j])
            u_pre, _ = _nm_matmul(x, norm1[layer], mod3, 1, 0, seg, odd_w_in[j].astype(BF16), tm)
            taps = odd_c_conv[j]
            tap_bias = odd_c_conv_b[j].reshape(1, 3 * c_width)
            zs = []
            for (row0, bn, seq) in ((0, bp, sp), (tp, bs, ss)):
                tc = _col_tile(c_width, max(LANES, HY_TILE_ELEMS // seq))
                spectra = _hyena_spectra(seq, PC, c_width, tc)
                z1 = _hyena_conv(u_pre, 2 * c_width, u_pre, 0, taps, tap_bias, odd_c_bias[j, 0:1], spectra, 0,
                                 row0, bn, seq, tc, True)
                z2 = _hyena_conv(z1.reshape(bn * seq, c_width), 0, u_pre, c_width, taps, tap_bias,
                                 odd_c_bias[j, 1:2], spectra, 1, row0, bn, seq, tc, False)
                zs.append(z2.reshape(bn * seq, c_width))
            x = _res_matmul(jnp.concatenate(zs, axis=0).astype(BF16), odd_w_out[j].astype(BF16), x, mod3, 2, seg, tm)
        q, hm = _nm_matmul(x, norm2[layer], mod3, 4, 3, seg, peer_wq[layer].astype(BF16), tm)
        i1, i2, gate = _peer_route(q, peer_keys[layer])
        gmat = _gate_matrix(i1, i2, gate, nkeys)
        x = _peer_experts(hm, gmat, peer_u[layer].astype(BF16), peer_v[layer].astype(BF16),
                          x, mod3, 5, seg, tm)

    y_prompt = x[:tp].reshape(bp, sp, d)
    y_sample = x[tp:].reshape(bs, ss, d)
    return (y_prompt, y_sample, jnp.stack(new_k, axis=1), jnp.stack(new_v, axis=1), jnp.stack(new_s, axis=1))
```

```python
import functools
import math

import numpy as np
import jax
import jax.numpy as jnp
from jax import lax
from jax.experimental import pallas as pl
from jax.experimental.pallas import tpu as pltpu

F32 = jnp.float32
BF16 = jnp.bfloat16
I32 = jnp.int32

NORM_EPS = 1e-6
A_HEAD_DIM = 64
A_GN_EPS = 64e-5
B_GROUP = 4
GRID_W = 64
ROPE_THETA = 10000.0
HYENA_ORDER = 2
DECAY_TARGET = 1e-2
FAST_DECAY_PCT = 0.3
SLOW_DECAY_PCT = 1.5
PEER_TOPK = 16
LANES = 128
VMEM_LIMIT = 56 * 1024 * 1024


def _cparams(sem):
    return pltpu.CompilerParams(dimension_semantics=sem, vmem_limit_bytes=VMEM_LIMIT)


def _row_tile(*lengths, cap=512):
    t = cap
    while any(n % t for n in lengths):
        t //= 2
    return t


def _col_tile(n, cap):
    return max(t for t in range(LANES, cap + 1, LANES) if n % t == 0)


def _mod_kernel(c_ref, w_ref, b_ref, o_ref):
    c = c_ref[...]
    s = (c * jax.nn.sigmoid(c)).astype(BF16)
    o_ref[0] = jnp.dot(s, w_ref[0].astype(BF16), preferred_element_type=F32) + b_ref[0]


def _mod_table(cond8, mod_w, mod_b):
    depth, d, n = mod_w.shape
    tn = _row_tile(n, cap=1024)
    return pl.pallas_call(
        _mod_kernel,
        grid=(depth, n // tn),
        in_specs=[pl.BlockSpec((8, d), lambda l, j: (0, 0)),
                  pl.BlockSpec((1, d, tn), lambda l, j: (l, 0, j)),
                  pl.BlockSpec((1, 1, tn), lambda l, j: (l, 0, j))],
        out_specs=pl.BlockSpec((1, 8, tn), lambda l, j: (l, 0, j)),
        out_shape=jax.ShapeDtypeStruct((depth, 8, n), F32),
        compiler_params=_cparams(("parallel", "parallel")),
        name="mod_table",
    )(cond8, mod_w, mod_b.reshape(depth, 1, n))


def _nm_matmul_kernel(seg_ref, x_ref, g_ref, sc_ref, sh_ref, w_ref, o_ref, h_ref, h_scr):
    del seg_ref

    @pl.when(pl.program_id(1) == 0)
    def _():
        x = x_ref[...]
        y = x * lax.rsqrt(jnp.mean(x * x, axis=-1, keepdims=True) + NORM_EPS) * g_ref[...]
        h = (y * (1.0 + sc_ref[0]) + sh_ref[0]).astype(BF16)
        h_scr[...] = h
        h_ref[...] = h

    o_ref[...] = jnp.dot(h_scr[...], w_ref[...], preferred_element_type=F32).astype(o_ref.dtype)


def _nm_matmul(x, g, mod3, sc_idx, sh_idx, seg, w_bf16, tm, out_dtype=F32):
    t, d = x.shape
    n = w_bf16.shape[1]
    tn = _col_tile(n, 1664)
    grid_spec = pltpu.PrefetchScalarGridSpec(
        num_scalar_prefetch=1,
        grid=(t // tm, n // tn),
        in_specs=[pl.BlockSpec((tm, d), lambda i, j, s: (i, 0)),
                  pl.BlockSpec((1, d), lambda i, j, s: (0, 0)),
                  pl.BlockSpec((1, 1, d), lambda i, j, s: (s[i] * 6 + sc_idx, 0, 0)),
                  pl.BlockSpec((1, 1, d), lambda i, j, s: (s[i] * 6 + sh_idx, 0, 0)),
                  pl.BlockSpec((d, tn), lambda i, j, s: (0, j))],
        out_specs=[pl.BlockSpec((tm, tn), lambda i, j, s: (i, j)),
                   pl.BlockSpec((tm, d), lambda i, j, s: (i, 0))],
        scratch_shapes=[pltpu.VMEM((tm, d), BF16)])
    return pl.pallas_call(
        _nm_matmul_kernel,
        grid_spec=grid_spec,
        out_shape=[jax.ShapeDtypeStruct((t, n), out_dtype), jax.ShapeDtypeStruct((t, d), BF16)],
        compiler_params=_cparams(("parallel", "arbitrary")),
        name="norm_mod_matmul",
    )(seg, x, g.reshape(1, d), mod3, mod3, w_bf16)


def _res_matmul_kernel(seg_ref, a_ref, w_ref, r_ref, gt_ref, o_ref):
    del seg_ref
    mm = jnp.dot(a_ref[...], w_ref[...], preferred_element_type=F32)
    o_ref[...] = r_ref[...] + gt_ref[0] * mm


def _res_matmul(a_bf16, w_bf16, res, mod3, gt_idx, seg, tm):
    t, k = a_bf16.shape
    n = w_bf16.shape[1]
    tn = _col_tile(n, 1024)
    grid_spec = pltpu.PrefetchScalarGridSpec(
        num_scalar_prefetch=1,
        grid=(t // tm, n // tn),
        in_specs=[pl.BlockSpec((tm, k), lambda i, j, s: (i, 0)),
                  pl.BlockSpec((k, tn), lambda i, j, s: (0, j)),
                  pl.BlockSpec((tm, tn), lambda i, j, s: (i, j)),
                  pl.BlockSpec((1, 1, tn), lambda i, j, s: (s[i] * 6 + gt_idx, 0, j))],
        out_specs=pl.BlockSpec((tm, tn), lambda i, j, s: (i, j)))
    return pl.pallas_call(
        _res_matmul_kernel,
        grid_spec=grid_spec,
        out_shape=jax.ShapeDtypeStruct((t, n), F32),
        compiler_params=_cparams(("parallel", "parallel")),
        name="res_matmul",
    )(seg, a_bf16, w_bf16, res, mod3)


def _scan_kernel(rf_ref, kkf_ref, vf_ref, wf_ref, bf_ref, ktf_ref, rb_ref, kkb_ref, vb_ref, wb_ref, bb_ref,
                 ktb_ref, s0_ref, yf_ref, yb_ref, sfin_ref, s_scr, *, tb_steps, nk, kq):
    ti = pl.program_id(1)

    @pl.when(ti == 0)
    def _():
        s_scr[...] = s0_ref[...]

    def tree(parts):
        while len(parts) > 1:
            parts = [parts[i] + parts[i + 1] for i in range(0, len(parts), 2)]
        return parts[0]

    def all_parts(p):
        part = LANES // kq
        return tree([p] + [pltpu.roll(p, i * part, 1) for i in range(1, kq)])

    nacc = 4

    def advance(d, t, r_ref, kk_ref, v_ref, w_ref, b_ref, kt_ref, y_ref):
        row = pl.ds(t, 1)
        accs = [None] * nacc
        for k in range(nk):
            p = s_scr[d, k] * kk_ref[k, row, :]
            accs[k % nacc] = p if accs[k % nacc] is None else accs[k % nacc] + p
        sa = all_parts(tree(accs))
        v = v_ref[t]
        yacc = [None] * nacc
        for k in range(nk):
            s_new = s_scr[d, k] * w_ref[k, row, :] - sa * b_ref[k, row, :] + v * kt_ref[k, row, :]
            s_scr[d, k] = s_new
            p = s_new * r_ref[k, row, :]
            yacc[k % nacc] = p if yacc[k % nacc] is None else yacc[k % nacc] + p
        y_ref[t] = all_parts(tree(yacc))

    def step(s, carry):
        advance(0, s, rf_ref, kkf_ref, vf_ref, wf_ref, bf_ref, ktf_ref, yf_ref)
        advance(1, tb_steps - 1 - s, rb_ref, kkb_ref, vb_ref, wb_ref, bb_ref, ktb_ref, yb_ref)
        return carry

    lax.fori_loop(0, tb_steps, step, 0, unroll=2)

    @pl.when(ti == pl.num_programs(1) - 1)
    def _():
        sfin_ref[...] = s_scr[...]


def _scan(r, kk, v, w2, b2, kt2, s0, kq):
    nk, seq, c = r.shape
    nv = v.shape[1]
    tb_steps = _row_tile(seq, cap=16)
    nt = seq // tb_steps
    rowf = pl.BlockSpec((nk, tb_steps, LANES), lambda gi, ti: (0, ti, gi))
    rowb = pl.BlockSpec((nk, tb_steps, LANES), lambda gi, ti: (0, nt - 1 - ti, gi))
    valf = pl.BlockSpec((tb_steps, nv, LANES), lambda gi, ti: (ti, 0, gi))
    valb = pl.BlockSpec((tb_steps, nv, LANES), lambda gi, ti: (nt - 1 - ti, 0, gi))
    st = pl.BlockSpec((2, nk, nv, LANES), lambda gi, ti: (0, 0, 0, gi))
    return pl.pallas_call(
        functools.partial(_scan_kernel, tb_steps=tb_steps, nk=nk, kq=kq),
        grid=(c // LANES, nt),
        in_specs=[rowf, rowf, valf, rowf, rowf, rowf, rowb, rowb, valb, rowb, rowb, rowb, st],
        out_specs=[valf, valb, st],
        out_shape=[jax.ShapeDtypeStruct((seq, nv, c), F32), jax.ShapeDtypeStruct((seq, nv, c), F32),
                   jax.ShapeDtypeStruct((2, nk, nv, c), F32)],
        scratch_shapes=[pltpu.VMEM((2, nk, nv, LANES), F32)],
        compiler_params=_cparams(("parallel", "arbitrary")),
        name="rwkv7_scan",
    )(r, kk, v, w2[0], b2[0], kt2[0], r, kk, v, w2[1], b2[1], kt2[1], s0)


CHAIN_TB = 128


def _to_chain_kernel(x_ref, o_ref, a_scr, *stage, heads, nk, kq, values):
    bg, tb = x_ref.shape[0], x_ref.shape[1]
    n = A_HEAD_DIM
    for b in range(bg):
        xt = x_ref[b].T
        a_scr[b] = jnp.swapaxes(xt.reshape(heads, n, tb), 0, 1)
    for p in range(n if values else nk):
        pieces = [a_scr[b, p if values else q * nk + p] for q in range(kq) for b in range(bg)]
        tile = jnp.concatenate(pieces, axis=0).T
        if values:
            stage[0][p] = tile
        else:
            o_ref[p] = tile
    if values:
        o_ref[...] = jnp.swapaxes(stage[0][...], 0, 1)


def _to_chain(x, row0, bn, seq, kq, values):
    w_ = x.shape[-1]
    heads = w_ // A_HEAD_DIM
    n = A_HEAD_DIM
    nk = n // kq
    bg = LANES // (kq * heads)
    ng = bn // bg
    c = kq * bn * heads
    assert kq * bg * heads == LANES and c == ng * LANES and row0 % (bg * seq) == 0
    tb = min(CHAIN_TB, seq)
    g0 = row0 // (bg * seq)
    scratch = [pltpu.VMEM((bg, n, heads, tb), F32)]
    if values:
        out_spec = pl.BlockSpec((tb, n, LANES), lambda g, t: (t, 0, g))
        out_shape = jax.ShapeDtypeStruct((seq, n, c), F32)
        scratch.append(pltpu.VMEM((n, tb, LANES), F32))
    else:
        out_spec = pl.BlockSpec((nk, tb, LANES), lambda g, t: (0, t, g))
        out_shape = jax.ShapeDtypeStruct((nk, seq, c), F32)
    return pl.pallas_call(
        functools.partial(_to_chain_kernel, heads=heads, nk=nk, kq=kq, values=values),
        grid=(ng, seq // tb),
        in_specs=[pl.BlockSpec((bg, tb, w_), lambda g, t: (g0 + g, t, 0))],
        out_specs=out_spec,
        out_shape=out_shape,
        scratch_shapes=scratch,
        compiler_params=_cparams(("parallel", "parallel")),
        name="to_chain_layout",
    )(x.reshape(-1, seq, w_))


def _from_chain_kernel(y_ref, o_ref, stage, a_scr, *, heads):
    bg = o_ref.shape[0]
    n = A_HEAD_DIM
    tb = y_ref.shape[0]
    stage[...] = jnp.swapaxes(y_ref[...], 0, 1)
    for v in range(n):
        rows = stage[v].T
        for b in range(bg):
            a_scr[b, v] = rows[b * heads:(b + 1) * heads]
    for b in range(bg):
        o_ref[b] = jnp.swapaxes(a_scr[b], 0, 1).reshape(heads * n, tb).T


def _from_chain(y, bn, heads, kq):
    seq, n, c = y.shape
    bg = LANES // (kq * heads)
    tb = min(CHAIN_TB, seq)
    out = pl.pallas_call(
        functools.partial(_from_chain_kernel, heads=heads),
        grid=(c // LANES, seq // tb),
        in_specs=[pl.BlockSpec((tb, n, LANES), lambda g, t: (t, 0, g))],
        out_specs=pl.BlockSpec((bg, tb, heads * n), lambda g, t: (g, t, 0)),
        out_shape=jax.ShapeDtypeStruct((bn, seq, heads * n), F32),
        scratch_shapes=[pltpu.VMEM((n, tb, LANES), F32), pltpu.VMEM((bg, n, heads, tb), F32)],
        compiler_params=_cparams(("parallel", "parallel")),
        name="from_chain_layout",
    )(y)
    return out.reshape(bn * seq, heads * n)


def _attn_kernel(q_ref, k_ref, v_ref, o_ref, *, dh):
    k = k_ref[0]
    v = v_ref[0]
    for g in range(B_GROUP):
        q = q_ref[0, :, g * dh:(g + 1) * dh]
        s = lax.dot_general(q, k, (((1,), (1,)), ((), ())), preferred_element_type=F32)
        m = jnp.max(s, axis=-1, keepdims=True)
        p = jnp.exp(s - m)
        den = jnp.sum(p, axis=-1, keepdims=True)
        o = jnp.dot(p.astype(BF16), v, preferred_element_type=F32) / den
        o_ref[0, :, g * dh:(g + 1) * dh] = o.astype(o_ref.dtype)


def _attend(q, k, v, dh):
    bn, lq, qw = q.shape
    lk = k.shape[1]
    kvh = k.shape[2] // dh
    gw = B_GROUP * dh
    tq = _row_tile(lq, cap=256)
    return pl.pallas_call(
        functools.partial(_attn_kernel, dh=dh),
        grid=(bn, kvh, lq // tq),
        in_specs=[pl.BlockSpec((1, tq, gw), lambda b, h, i: (b, i, h)),
                  pl.BlockSpec((1, lk, dh), lambda b, h, i: (b, 0, h)),
                  pl.BlockSpec((1, lk, dh), lambda b, h, i: (b, 0, h))],
        out_specs=pl.BlockSpec((1, tq, gw), lambda b, h, i: (b, i, h)),
        out_shape=jax.ShapeDtypeStruct((bn, lq, qw), BF16),
        compiler_params=_cparams(("parallel", "parallel", "parallel")),
        name="attention",
    )(q, k, v)


G_GROUP = 16


def _gate_matrix_kernel(i1_ref, i2_ref, g_ref, o_ref, gtmp_scr, *, nkeys):
    iota = lax.broadcasted_iota(I32, (nkeys, nkeys), 0)

    def build(grp, carry):
        base = pl.multiple_of(grp * G_GROUP, G_GROUP)
        def token(tt, c2):
            t = base + tt
            a_t = jnp.where(iota == i1_ref[pl.ds(t, 1), :], 1.0, 0.0).astype(BF16)
            b_t = jnp.where(iota == i2_ref[pl.ds(t, 1), :], g_ref[pl.ds(t, 1), :], 0.0).astype(BF16)
            gtmp_scr[tt] = lax.dot_general(a_t, b_t, (((1,), (1,)), ((), ())), preferred_element_type=F32)
            return c2

        lax.fori_loop(0, G_GROUP, token, 0, unroll=G_GROUP)
        by_n1 = jnp.swapaxes(gtmp_scr[...], 0, 1).astype(BF16)
        for n1 in range(nkeys):
            o_ref[pl.ds(base, G_GROUP), n1 * nkeys:(n1 + 1) * nkeys] = by_n1[n1]
        return carry

    lax.fori_loop(0, o_ref.shape[0] // G_GROUP, build, 0)


def _gate_matrix(i1, i2, gate, nkeys):
    t, nj = i1.shape
    tb = LANES
    sel = pl.BlockSpec((tb, nj), lambda i: (i, 0))
    return pl.pallas_call(
        functools.partial(_gate_matrix_kernel, nkeys=nkeys),
        grid=(t // tb,),
        in_specs=[sel, sel, sel],
        out_specs=pl.BlockSpec((tb, nkeys * nkeys), lambda i: (i, 0)),
        out_shape=jax.ShapeDtypeStruct((t, nkeys * nkeys), BF16),
        scratch_shapes=[pltpu.VMEM((G_GROUP, nkeys, nkeys), F32)],
        compiler_params=_cparams(("parallel",)),
        name="peer_gate_matrix",
    )(i1, i2, gate)


def _to_bf16_kernel(x_ref, o_ref):
    o_ref[...] = x_ref[0].astype(BF16)


def _to_bf16(tables, layer):
    _, e, d = tables.shape
    te = _row_tile(e, cap=1024)
    return pl.pallas_call(
        _to_bf16_kernel,
        grid=(e // te,),
        in_specs=[pl.BlockSpec((1, te, d), lambda i: (layer, i, 0))],
        out_specs=pl.BlockSpec((te, d), lambda i: (i, 0)),
        out_shape=jax.ShapeDtypeStruct((e, d), BF16),
        compiler_params=_cparams(("parallel",)),
        name="to_bf16",
    )(tables)


def _peer_kernel(seg_ref, xb_ref, gm_ref, u_ref, v_ref, r_ref, gt_ref, o_ref, acc_scr):
    del seg_ref
    e = pl.program_id(1)

    @pl.when(e == 0)
    def _():
        acc_scr[...] = jnp.zeros_like(acc_scr)

    h = lax.dot_general(xb_ref[...], u_ref[...], (((1,), (1,)), ((), ())), preferred_element_type=F32)
    act = 0.5 * h * (1.0 + lax.erf(h * (1.0 / math.sqrt(2.0)))) * gm_ref[...].astype(F32)
    acc_scr[...] += jnp.dot(act.astype(BF16), v_ref[...], preferred_element_type=F32)

    @pl.when(e == pl.num_programs(1) - 1)
    def _():
        o_ref[...] = r_ref[...] + gt_ref[0] * acc_scr[...]


def _peer_experts(xb, gmat, u_bf16, v_bf16, res, mod3, gt_idx, seg, tm, row0=0, rows=None):
    t, d = xb.shape
    rows = t if rows is None else rows
    r0 = row0 // tm
    ne = u_bf16.shape[0]
    te = 1024
    grid_spec = pltpu.PrefetchScalarGridSpec(
        num_scalar_prefetch=1,
        grid=(rows // tm, ne // te),
        in_specs=[pl.BlockSpec((tm, d), lambda i, e, s: (r0 + i, 0)),
                  pl.BlockSpec((tm, te), lambda i, e, s: (r0 + i, e)),
                  pl.BlockSpec((te, d), lambda i, e, s: (e, 0)),
                  pl.BlockSpec((te, d), lambda i, e, s: (e, 0)),
                  pl.BlockSpec((tm, d), lambda i, e, s: (r0 + i, 0)),
                  pl.BlockSpec((1, 1, d), lambda i, e, s: (s[r0 + i] * 6 + gt_idx, 0, 0))],
        out_specs=pl.BlockSpec((tm, d), lambda i, e, s: (i, 0)),
        scratch_shapes=[pltpu.VMEM((tm, d), F32)])
    return pl.pallas_call(
        _peer_kernel,
        grid_spec=grid_spec,
        out_shape=jax.ShapeDtypeStruct((rows, d), F32),
        compiler_params=_cparams(("parallel", "arbitrary")),
        name="peer_experts",
    )(seg, xb, gmat, u_bf16, v_bf16, res, mod3)


def _topk_rows(vals, payload, rows_out):
    big = jnp.float32(2 ** 30)
    top_v = jnp.zeros(rows_out.shape, F32)
    top_p = jnp.zeros(rows_out.shape, F32)
    for it in range(PEER_TOPK):
        m = jnp.max(vals, axis=0, keepdims=True)
        sel = jnp.min(jnp.where(vals == m, payload, big), axis=0, keepdims=True)
        top_v = jnp.where(rows_out == it, m, top_v)
        top_p = jnp.where(rows_out == it, sel, top_p)
        vals = jnp.where(payload == sel, -jnp.inf, vals)
    return top_v, top_p


def _top_pair_sums(s1, s2, rows_out):
    k = PEER_TOPK
    half = k // 2
    tt = s1.shape[1]
    big = jnp.float32(2 ** 30)
    lists = [s1[:half] + s2[j:j + 1] for j in range(k)]
    singles = s1[half:] + s2[0:1]
    sub = lax.broadcasted_iota(I32, (half, tt), 0).astype(F32)
    head_id = sub * k
    single_id = (sub + half) * k
    top_v = jnp.zeros(rows_out.shape, F32)
    top_p = jnp.zeros(rows_out.shape, F32)
    for it in range(k):
        m = jnp.max(jnp.maximum(lists[0], singles), axis=0, keepdims=True)
        sel = jnp.min(jnp.minimum(jnp.where(lists[0] == m, head_id, big), jnp.where(singles == m, single_id, big)),
                      axis=0, keepdims=True)
        top_v = jnp.where(rows_out == it, m, top_v)
        top_p = jnp.where(rows_out == it, sel, top_p)
        pop = head_id == sel
        lists = [jnp.where(pop, lists[j + 1], lists[j]) for j in range(k - 1)] + [jnp.where(pop, -jnp.inf, lists[-1])]
        head_id = jnp.where(pop, head_id + 1.0, head_id)
        singles = jnp.where(single_id == sel, -jnp.inf, singles)
    return top_v, top_p


def _gather_rows(table, sel):
    out = jnp.zeros(sel.shape, table.dtype)
    for i in range(PEER_TOPK):
        out = jnp.where(sel == i, table[i:i + 1, :], out)
    return out


def _route_kernel(q_ref, keys_ref, i1_ref, i2_ref, g_ref, n1_scr, n2_scr, gate_scr, *, heads, nkeys, dq):
    tt = q_ref.shape[0]
    k = PEER_TOPK
    n_iota = lax.broadcasted_iota(I32, (nkeys, tt), 0).astype(F32)
    rows_out = lax.broadcasted_iota(I32, (k, tt), 0)

    def head(h, carry):
        tops = []
        for c in range(2):
            col = pl.multiple_of((h * 2 + c) * dq, dq)
            qhc = q_ref[:, pl.ds(col, dq)].astype(BF16)
            khc = keys_ref[h, c].astype(BF16)
            s = lax.dot_general(khc, qhc, (((1,), (1,)), ((), ())), preferred_element_type=F32)
            tops.append(_topk_rows(s, n_iota, rows_out))
        (s1, i1), (s2, i2) = tops
        top, ci = _top_pair_sums(s1, s2, rows_out)
        ci = ci.astype(I32)
        n1 = _gather_rows(i1, lax.shift_right_logical(ci, 4)).astype(I32)
        n2 = _gather_rows(i2, lax.bitwise_and(ci, k - 1)).astype(I32)
        ex = jnp.exp(top - top[0:1, :])
        gate = ex / jnp.sum(ex, axis=0, keepdims=True)
        row = pl.multiple_of(h * k, k)
        n1_scr[pl.ds(row, k), :] = n1
        n2_scr[pl.ds(row, k), :] = n2
        gate_scr[pl.ds(row, k), :] = gate
        return carry

    lax.fori_loop(0, heads, head, 0)
    i1_ref[...] = n1_scr[...].T
    i2_ref[...] = n2_scr[...].T
    g_ref[...] = gate_scr[...].T


def _peer_route(q, sub_keys):
    t, qw = q.shape
    heads, _, nkeys, dq = sub_keys.shape
    hk = heads * PEER_TOPK
    tt = LANES
    out = pl.BlockSpec((tt, hk), lambda i: (i, 0))
    return pl.pallas_call(
        functools.partial(_route_kernel, heads=heads, nkeys=nkeys, dq=dq),
        grid=(t // tt,),
        in_specs=[pl.BlockSpec((tt, qw), lambda i: (i, 0)),
                  pl.BlockSpec((heads, 2, nkeys, dq), lambda i: (0, 0, 0, 0))],
        out_specs=[out, out, out],
        out_shape=[jax.ShapeDtypeStruct((t, hk), I32), jax.ShapeDtypeStruct((t, hk), I32),
                   jax.ShapeDtypeStruct((t, hk), F32)],
        scratch_shapes=[pltpu.VMEM((hk, tt), I32), pltpu.VMEM((hk, tt), I32), pltpu.VMEM((hk, tt), F32)],
        compiler_params=_cparams(("parallel",)),
        name="peer_route",
    )(q, sub_keys)


def _split(x):
    hi = x.astype(BF16)
    return hi, (x - hi.astype(F32)).astype(BF16)


def _mm(a, b):
    return jnp.dot(a, b, preferred_element_type=F32)


def _head_sum(x, ones_blockdiag):
    hi, lo = _split(x)
    return _mm(hi, ones_blockdiag) + _mm(lo, ones_blockdiag)


def _shift_rows(x, first_row, last_row):
    n = x.shape[0]
    rows = lax.broadcasted_iota(I32, x.shape, 0)
    prev = jnp.where(rows == 0, first_row, pltpu.roll(x, 1, 0))
    nxt = jnp.where(rows == n - 1, last_row, pltpu.roll(x, n - 1, 0))
    return prev, nxt


def _conv3(x, first_row, last_row, taps):
    prev, nxt = _shift_rows(x, first_row, last_row)
    return prev * taps[0:1] + x * taps[1:2] + nxt * taps[2:3]


def _rwkv_pre_kernel(r_ref, k_ref, v_ref, low_ref, pr_ref, pk_ref, pv_ref, plow_ref, nr_ref, nk_ref, nv_ref,
                     nlow_ref, cr_ref, ck_ref, cv_ref, clow_ref, w0_ref, wu_ref, a0_ref, au_ref, gu_ref,
                     kkg_ref, ka_ref, rk_ref, ones_ref,
                     ro_ref, kko_ref, vo_ref, w0o_ref, w1o_ref, b0o_ref, b1o_ref, kt0o_ref, kt1o_ref,
                     go_ref, bonus_ref, *, rw, ra):
    r = _conv3(r_ref[...], pr_ref[0], nr_ref[0], cr_ref[...])
    k = _conv3(k_ref[...], pk_ref[0], nk_ref[0], ck_ref[...])
    v = _conv3(v_ref[...], pv_ref[0], nv_ref[0], cv_ref[...])
    low = _conv3(low_ref[...], plow_ref[0], nlow_ref[0], clow_ref[...])
    ones = ones_ref[...]
    kk = k * kkg_ref[...]
    kk = kk * lax.rsqrt(_head_sum(kk * kk, ones) + 1e-12)
    ro_ref[...] = r
    kko_ref[...] = kk
    vo_ref[...] = v
    bonus_ref[...] = _head_sum(r * k * rk_ref[...], ones) * v
    gd = low[:, 2 * rw + 2 * ra:]
    go_ref[...] = _mm(jax.nn.sigmoid(gd).astype(BF16), gu_ref[...])
    for d_, (wo, bo, kto) in enumerate(((w0o_ref, b0o_ref, kt0o_ref), (w1o_ref, b1o_ref, kt1o_ref))):
        wd = low[:, d_ * rw:(d_ + 1) * rw]
        ad = low[:, 2 * rw + d_ * ra:2 * rw + (d_ + 1) * ra]
        lw = w0_ref[d_:d_ + 1, :] + _mm(jnp.tanh(wd).astype(BF16), wu_ref[d_])
        softplus = jnp.maximum(-lw, 0.0) + jnp.log1p(jnp.exp(-jnp.abs(lw)))
        wo[...] = jnp.exp(-jnp.exp(-softplus - 0.5))
        a = jax.nn.sigmoid(a0_ref[d_:d_ + 1, :] + _mm(ad.astype(BF16), au_ref[d_]))
        bo[...] = kk * a
        kto[...] = k * (1.0 + (a - 1.0) * ka_ref[...])


def _rwkv_pre(z, prev_rows, next_rows, P, tm, offs):
    t = z.shape[0]
    w_ = P['a_w0'].shape[-1]
    rw, ra = P['a_wu'].shape[1], P['a_au'].shape[1]
    lw = offs['low_w']
    li = offs['low'] // lw
    row = lambda c, wd: pl.BlockSpec((tm, wd), lambda i: (i, c))
    edge = lambda c, wd: pl.BlockSpec((1, 1, wd), lambda i: (i, 0, c))
    full = lambda a: pl.BlockSpec(a.shape, lambda i: (0,) * a.ndim)
    conv = P['a_conv']
    consts = [conv[:, :w_], conv[:, w_:2 * w_], conv[:, 2 * w_:3 * w_], conv[:, 3 * w_:],
              P['a_w0'], P['a_wu'].astype(BF16), P['a_a0'], P['a_au'].astype(BF16), P['a_gu'].astype(BF16),
              P['a_kk'].reshape(1, w_), P['a_ka'].reshape(1, w_), P['a_rk'].reshape(1, w_),
              jnp.asarray(_blockdiag_ones(w_), BF16)]
    out = pl.BlockSpec((tm, w_), lambda i: (i, 0))
    return pl.pallas_call(
        functools.partial(_rwkv_pre_kernel, rw=rw, ra=ra),
        grid=(t // tm,),
        in_specs=[row(0, w_), row(1, w_), row(2, w_), row(li, lw),
                  edge(0, w_), edge(1, w_), edge(2, w_), edge(li, lw),
                  edge(0, w_), edge(1, w_), edge(2, w_), edge(li, lw)] + [full(a) for a in consts],
        out_specs=[out] * 11,
        out_shape=[jax.ShapeDtypeStruct((t, w_), F32)] * 11,
        compiler_params=_cparams(("parallel",)),
        name="rwkv7_pre",
    )(z, z, z, z, prev_rows, prev_rows, prev_rows, prev_rows, next_rows, next_rows, next_rows, next_rows, *consts)


def _blockdiag_ones(width):
    idx = np.arange(width) // A_HEAD_DIM
    return (idx[:, None] == idx[None, :]).astype(np.float32)


def _rwkv_post_kernel(yf_ref, yb_ref, bonus_ref, g_ref, lnw_ref, lnb_ref, ones_ref, o_ref):
    ones = ones_ref[...]
    y = yf_ref[...] + yb_ref[...]
    mu = _head_sum(y, ones) * (1.0 / A_HEAD_DIM)
    dlt = y - mu
    var = _head_sum(dlt * dlt, ones) * (1.0 / A_HEAD_DIM)
    yn = dlt * lax.rsqrt(var + A_GN_EPS) * lnw_ref[...] + lnb_ref[...]
    o_ref[...] = ((yn + bonus_ref[...]) * g_ref[...]).astype(o_ref.dtype)


def _rwkv_post(yf, yb, bonus, g, row0, P, tm):
    t, w_ = yf.shape
    r0 = row0 // tm
    row = pl.BlockSpec((tm, w_), lambda i: (i, 0))
    off = pl.BlockSpec((tm, w_), lambda i: (r0 + i, 0))
    vec = pl.BlockSpec((1, w_), lambda i: (0, 0))
    return pl.pallas_call(
        _rwkv_post_kernel,
        grid=(t // tm,),
        in_specs=[row, row, off, off, vec, vec, pl.BlockSpec((w_, w_), lambda i: (0, 0))],
        out_specs=row,
        out_shape=jax.ShapeDtypeStruct((t, w_), BF16),
        compiler_params=_cparams(("parallel",)),
        name="rwkv7_post",
    )(yf, yb, bonus, g, P['a_ln_w'].reshape(1, w_), P['a_ln_b'].reshape(1, w_),
      jnp.asarray(_blockdiag_ones(w_), BF16))


def _attn_pre_kernel(q_ref, kv_ref, qn_ref, kn_ref, cos_ref, sin_ref, qo_ref, ko_ref, vo_ref, kf_ref, vf_ref,
                     *, dh, latent):
    def rms(x, g):
        return x * lax.rsqrt(jnp.mean(x * x, axis=-1, keepdims=True) + NORM_EPS) * g

    def rope(x):
        if not latent:
            return x
        lanes = lax.broadcasted_iota(I32, x.shape, 1)
        quarter = dh // 4
        partner = jnp.where(lanes % (2 * quarter) < quarter, pltpu.roll(x, dh - quarter, 1), pltpu.roll(x, quarter, 1))
        return x * cos_ref[...] + partner * sin_ref[...]

    nq = q_ref.shape[1] // dh
    nkv = kv_ref.shape[1] // (2 * dh)
    for h in range(nq):
        q = rope(rms(q_ref[:, h * dh:(h + 1) * dh], qn_ref[...]))
        qo_ref[:, h * dh:(h + 1) * dh] = (q * dh ** -0.5).astype(BF16)
    for h in range(nkv):
        k = rms(kv_ref[:, h * dh:(h + 1) * dh], kn_ref[...])
        v = kv_ref[:, (nkv + h) * dh:(nkv + h + 1) * dh]
        kf_ref[:, h * dh:(h + 1) * dh] = k
        vf_ref[:, h * dh:(h + 1) * dh] = v
        ko_ref[:, h * dh:(h + 1) * dh] = rope(k).astype(BF16)
        vo_ref[:, h * dh:(h + 1) * dh] = v.astype(BF16)


def _rope_tables(seq, dh):
    quarter = dh // 4
    inv = ROPE_THETA ** (-np.arange(quarter, dtype=np.float64) / quarter)
    pos = np.arange(seq)
    ang_r = (pos // GRID_W)[:, None] * inv[None, :]
    ang_c = (pos % GRID_W)[:, None] * inv[None, :]
    cos = np.concatenate([np.cos(ang_r)] * 2 + [np.cos(ang_c)] * 2, axis=1)
    sin = np.concatenate([-np.sin(ang_r), np.sin(ang_r), -np.sin(ang_c), np.sin(ang_c)], axis=1)
    return jnp.asarray(cos, F32), jnp.asarray(sin, F32)


def _attn_pre(z, qnorm, knorm, offs, row0, rows, seq, tm, latent):
    dh = qnorm.shape[-1]
    bw, kvw2 = offs['q_w'], offs['kv_w']
    r0 = row0 // tm
    per_seq = seq // tm
    cos, sin = _rope_tables(seq, dh) if latent else (jnp.zeros((tm, dh), F32), jnp.zeros((tm, dh), F32))
    tab = pl.BlockSpec((tm, dh), (lambda i: (i % per_seq, 0)) if latent else (lambda i: (0, 0)))
    vec = pl.BlockSpec((1, dh), lambda i: (0, 0))
    kvo = pl.BlockSpec((tm, kvw2 // 2), lambda i: (i, 0))
    return pl.pallas_call(
        functools.partial(_attn_pre_kernel, dh=dh, latent=latent),
        grid=(rows // tm,),
        in_specs=[pl.BlockSpec((tm, bw), lambda i: (r0 + i, offs['q'] // bw)),
                  pl.BlockSpec((tm, kvw2), lambda i: (r0 + i, offs['kv'] // kvw2)),
                  vec, vec, tab, tab],
        out_specs=[pl.BlockSpec((tm, bw), lambda i: (i, 0)), kvo, kvo, kvo, kvo],
        out_shape=[jax.ShapeDtypeStruct((rows, bw), BF16), jax.ShapeDtypeStruct((rows, kvw2 // 2), BF16),
                   jax.ShapeDtypeStruct((rows, kvw2 // 2), BF16), jax.ShapeDtypeStruct((rows, kvw2 // 2), F32),
                   jax.ShapeDtypeStruct((rows, kvw2 // 2), F32)],
        compiler_params=_cparams(("parallel",)),
        name="attention_pre",
    )(z, z, qnorm.reshape(1, dh), knorm.reshape(1, dh), cos, sin)


HY_BLK = 256
HY_TILE_ELEMS = 128 * 1024


def _dft_consts():
    n = 2 * HY_BLK
    k = np.arange(HY_BLK, dtype=np.float64)[:, None] + 0.5
    s = np.arange(HY_BLK, dtype=np.float64)[None, :]
    th = 2.0 * np.pi * k * s / n
    fwd = np.concatenate([np.cos(th), -np.sin(th)], axis=0)
    tau = np.arange(n, dtype=np.float64)[:, None]
    ph = 2.0 * np.pi * tau * (np.arange(HY_BLK, dtype=np.float64)[None, :] + 0.5) / n
    inv = np.concatenate([np.cos(ph), -np.sin(ph)], axis=1) * (2.0 / n)
    inv_cat = np.concatenate([inv[:HY_BLK], inv[HY_BLK:]], axis=1)

    def hl(a):
        a32 = jnp.asarray(a, F32)
        hi = a32.astype(BF16)
        return hi, (a32 - hi.astype(F32)).astype(BF16)

    return hl(fwd), hl(inv_cat)


def _lag_features(seq, emb):
    bands = (emb - 1) // 2
    t = np.linspace(0.0, 1.0, seq)
    wpos = 2.0 * np.pi * np.arange(seq) / seq
    f = np.linspace(1e-4, bands - 1, bands)
    z = np.concatenate([t[:, None], np.cos(f[None, :] * wpos[:, None]), -np.sin(f[None, :] * wpos[:, None])], axis=1)
    lag = np.concatenate([np.zeros(1, np.int64), np.arange(seq - 1, 0, -1), np.arange(seq)])
    return jnp.asarray(z[lag], F32)


def _hyena_filter_kernel(z_ref, fw1_ref, fb1_ref, freq_ref, fw2_ref, fb2_ref, w3b_ref, w3f_ref, dl_ref,
                         fh_ref, fl_ref, g_ref, hdn_scr, f_scr, *, seq):
    @pl.when((pl.program_id(0) == 0) & (pl.program_id(1) == 0))
    def _():
        h1 = jnp.sin(freq_ref[...] * (_mm(z_ref[...].astype(BF16), fw1_ref[...]) + fb1_ref[...]))
        hdn_scr[...] = jnp.sin(freq_ref[...] * (_mm(h1.astype(BF16), fw2_ref[...]) + fb2_ref[...])).astype(BF16)

    decay = jnp.exp(-z_ref[:, 0:1] * dl_ref[...])
    f_scr[0:seq, :] = _mm(hdn_scr[0:seq, :], w3b_ref[...]) * decay[0:seq, :]
    f_scr[seq:2 * seq, :] = _mm(hdn_scr[seq:2 * seq, :], w3f_ref[...]) * decay[seq:2 * seq, :]
    f = f_scr[...]
    scale = lax.rsqrt(jnp.sum(f * f, axis=0, keepdims=True) + 1e-12)
    rows = lax.broadcasted_iota(I32, f.shape, 0)
    f_scr[...] = jnp.where(rows == 0, 0.0, f * scale)
    for m in range(2 * seq // HY_BLK):
        hi, lo = _split(f_scr[m * HY_BLK:(m + 1) * HY_BLK, :])
        g_ref[0, m] = _mm(fh_ref[...], hi) + (_mm(fh_ref[...], lo) + _mm(fl_ref[...], hi))


def _hyena_spectra(seq, P, c_width, tc):
    emb, hid = P['c_fw1'].shape
    z = _lag_features(seq, emb)
    embp = 64
    z = jnp.pad(z, ((0, 0), (0, embp - emb)))
    fw1 = jnp.pad(P['c_fw1'], ((0, embp - emb), (0, 0))).astype(BF16)
    deltas = jnp.asarray(np.abs(np.linspace(math.log(DECAY_TARGET) / SLOW_DECAY_PCT,
                                            math.log(DECAY_TARGET) / FAST_DECAY_PCT, c_width)), F32).reshape(1, c_width)
    (fh, fl), _ = _dft_consts()
    nct = c_width // tc
    nseg = 2 * seq // HY_BLK
    full = lambda a: pl.BlockSpec(a.shape, lambda o, j: (0,) * a.ndim)
    w3 = P['c_fw3'].astype(BF16)
    consts = [z, fw1, P['c_fb1'].reshape(1, hid), P['c_freq'].reshape(1, hid), P['c_fw2'].astype(BF16),
              P['c_fb2'].reshape(1, hid)]
    return pl.pallas_call(
        functools.partial(_hyena_filter_kernel, seq=seq),
        grid=(HYENA_ORDER, nct),
        in_specs=[full(a) for a in consts] + [
            pl.BlockSpec((hid, tc), lambda o, j: (0, (o * 2 + 1) * nct + j)),
            pl.BlockSpec((hid, tc), lambda o, j: (0, (o * 2) * nct + j)),
            pl.BlockSpec((1, tc), lambda o, j: (0, j)), full(fh), full(fl)],
        out_specs=pl.BlockSpec((1, nseg, 2 * HY_BLK, tc), lambda o, j: (o, 0, 0, j)),
        out_shape=jax.ShapeDtypeStruct((HYENA_ORDER, nseg, 2 * HY_BLK, c_width), F32),
        scratch_shapes=[pltpu.VMEM((2 * seq, hid), BF16), pltpu.VMEM((2 * seq, tc), F32)],
        compiler_params=_cparams(("arbitrary", "arbitrary")),
        name="hyena_spectra",
    )(*consts, w3, w3, deltas, fh, fl)


def _hyena_conv_kernel(zin_ref, gate_ref, tz_ref, bz_ref, tg_ref, bg_ref, bias_ref, g_ref, fh_ref, fl_ref,
                       ih_ref, il_ref, o_ref, z_scr, gate_scr, u_scr, y_scr, *, nb, conv_in):
    half = HY_BLK
    zero = jnp.zeros((1, zin_ref.shape[-1]), F32)
    z = zin_ref[0]
    if conv_in:
        z = _conv3(z, zero, zero, tz_ref[...]) + bz_ref[...]
    z_scr[...] = z
    for j in range(nb):
        hi, lo = _split(z_scr[j * half:(j + 1) * half, :])
        u_scr[j] = _mm(fh_ref[...], hi) + (_mm(fh_ref[...], lo) + _mm(fl_ref[...], hi))

    width = zin_ref.shape[-1]
    chunk = max(8, 32 * LANES // width)

    def spectra(pieces):
        for c in range(half // chunk):
            re = pl.ds(c * chunk, chunk)
            im = pl.ds(half + c * chunk, chunk)

            def add_block(j, acc):
                ur, ui = u_scr[j, re, :], u_scr[j, im, :]
                out = []
                for a, ii in enumerate(pieces):
                    m = ii - 1 - j + nb
                    gr, gi = g_ref[0, m, re, :], g_ref[0, m, im, :]
                    out.append((acc[a][0] + (ur * gr - ui * gi), acc[a][1] + (ur * gi + ui * gr)))
                return tuple(out)

            zero = jnp.zeros((chunk, width), F32)
            acc = lax.fori_loop(0, nb, add_block, tuple((zero, zero) for _ in pieces), unroll=min(4, nb))
            for a, ii in enumerate(pieces):
                y_scr[ii, re, :] = acc[a][0]
                y_scr[ii, im, :] = acc[a][1]

    def pair(p, carry):
        spectra([2 * p, 2 * p + 1])
        return carry

    lax.fori_loop(0, (nb + 1) // 2, pair, 0)
    if (nb + 1) % 2:
        spectra([nb])

    gate_scr[...] = _conv3(gate_ref[0], zero, zero, tg_ref[...]) + bg_ref[...]

    def block(i, carry):
        rows = pl.ds(pl.multiple_of(i * half, half), half)
        ycat = jnp.concatenate([y_scr[i + 1], y_scr[i]], axis=0)
        hi, lo = _split(ycat)
        conv = _mm(ih_ref[...], hi) + (_mm(ih_ref[...], lo) + _mm(il_ref[...], hi))
        o_ref[0, rows, :] = gate_scr[rows, :] * (conv + bias_ref[...] * z_scr[rows, :])
        return carry

    lax.fori_loop(0, nb, block, 0)


def _hyena_conv(zin, zin_col0, gate_src, gate_col0, taps, tap_bias, bias, spectra, order, row0, nseq, seq, tc, conv_in):
    c_width = bias.shape[-1]
    nb = seq // HY_BLK
    nct = c_width // tc
    (fh, fl), (ih, il) = _dft_consts()
    s0 = row0 // seq
    zc, gc = zin_col0 // tc, gate_col0 // tc
    tapc = (zc if conv_in else gc)
    full = lambda a: pl.BlockSpec(a.shape, lambda j, b: (0,) * a.ndim)
    zin3 = zin.reshape(-1, seq, zin.shape[-1])
    gate3 = gate_src.reshape(-1, seq, gate_src.shape[-1])
    zs0 = s0 if conv_in else 0
    return pl.pallas_call(
        functools.partial(_hyena_conv_kernel, nb=nb, conv_in=conv_in),
        grid=(nct, nseq),
        in_specs=[pl.BlockSpec((1, seq, tc), lambda j, b: (zs0 + b, 0, zc + j)),
                  pl.BlockSpec((1, seq, tc), lambda j, b: (s0 + b, 0, gc + j)),
                  pl.BlockSpec((3, tc), lambda j, b: (0, tapc + j)),
                  pl.BlockSpec((1, tc), lambda j, b: (0, tapc + j)),
                  pl.BlockSpec((3, tc), lambda j, b: (0, gc + j)),
                  pl.BlockSpec((1, tc), lambda j, b: (0, gc + j)),
                  pl.BlockSpec((1, tc), lambda j, b: (0, j)),
                  pl.BlockSpec((1, 2 * nb, 2 * HY_BLK, tc), lambda j, b: (order, 0, 0, j)),
                  full(fh), full(fl), full(ih), full(il)],
        out_specs=pl.BlockSpec((1, seq, tc), lambda j, b: (b, 0, j)),
        out_shape=jax.ShapeDtypeStruct((nseq, seq, c_width), F32),
        scratch_shapes=[pltpu.VMEM((seq, tc), F32), pltpu.VMEM((seq, tc), F32),
                        pltpu.VMEM((nb, 2 * HY_BLK, tc), F32), pltpu.VMEM((nb + 1, 2 * HY_BLK, tc), F32)],
        compiler_params=_cparams(("arbitrary", "arbitrary")),
        name="hyena_conv",
    )(zin3, gate3, taps, tap_bias, taps, tap_bias, bias, spectra, fh, fl, ih, il)


def _rwkv_scan_pass(r, kk, v, w2, b2, kt2, s0, row0, bn, seq):
    w_ = r.shape[-1]
    heads = w_ // A_HEAD_DIM
    n = A_HEAD_DIM
    bh = bn * heads
    kq = max(1, LANES // bh)
    nk = n // kq
    c = kq * bh
    rows = lambda x: _to_chain(x, row0, bn, seq, kq, False)
    if s0 is None:
        s0r = jnp.zeros((2, nk, n, c), F32)
    else:
        s0r = jnp.transpose(s0.reshape(bn, 2, heads, n, kq, nk), (1, 5, 3, 4, 0, 2)).reshape(2, nk, n, c)
    yf, yb, sfin = _scan(rows(r), rows(kk), _to_chain(v, row0, bn, seq, kq, True),
                         (rows(w2[0]), rows(w2[1])), (rows(b2[0]), rows(b2[1])), (rows(kt2[0]), rows(kt2[1])),
                         s0r, kq)
    sfin = jnp.transpose(sfin.reshape(2, nk, n, kq, bn, heads), (4, 0, 5, 2, 3, 1)).reshape(bn, 2, heads, n, n)
    return _from_chain(yf, bn, heads, kq), _from_chain(yb, bn, heads, kq), sfin


def kernel(x_prompt, x_sample, cache_b_k, cache_b_v, state_a, c, c_ctx, mod_w, mod_b, norm1, norm2,
           even_w_in, even_a_conv, even_a_w0, even_a_wu, even_a_a0, even_a_au, even_a_gu, even_a_kk,
           even_a_ka, even_a_rk, even_a_ln_w, even_a_ln_b, even_b_qnorm, even_b_knorm, even_w_out,
           odd_w_in, odd_c_conv, odd_c_conv_b, odd_c_fw1, odd_c_fb1, odd_c_freq, odd_c_fw2, odd_c_fb2,
           odd_c_fw3, odd_c_bias, odd_w_out, peer_wq, peer_keys, peer_u, peer_v):
    bp, sp, d = x_prompt.shape
    bs, ss, _ = x_sample.shape
    depth = mod_w.shape[0]
    tp, ts = bp * sp, bs * ss
    t_all = tp + ts
    a_width = even_a_w0.shape[-1]
    a_cols = even_a_conv.shape[-1]
    dh = even_b_qnorm.shape[-1]
    b_width = d // 2
    kv_width = b_width // B_GROUP
    kvh = kv_width // dh
    c_width = odd_c_bias.shape[-1]
    nkeys = peer_keys.shape[3]
    assert bs + 1 <= 8 and nkeys == LANES and peer_keys.shape[1] * PEER_TOPK == LANES

    tm = _row_tile(tp, ss, cap=512)
    seg = jnp.concatenate([jnp.zeros((tp // tm,), I32),
                           1 + jnp.arange(ts // tm, dtype=I32) // (ss // tm)])

    cond8 = jnp.zeros((8, d), F32).at[0].set(c_ctx).at[1:1 + bs].set(c)
    mods = _mod_table(cond8, mod_w, mod_b)

    x = jnp.concatenate([x_prompt.reshape(tp, d), x_sample.reshape(ts, d)], axis=0)
    new_k, new_v, new_s = [], [], []
    for layer in range(depth):
        j = layer // 2
        mod3 = mods[layer].reshape(8 * 6, 1, d)
        if layer % 2 == 0:
            PA = dict(a_conv=even_a_conv[j], a_w0=even_a_w0[j], a_wu=even_a_wu[j], a_a0=even_a_a0[j],
                      a_au=even_a_au[j], a_gu=even_a_gu[j], a_kk=even_a_kk[j], a_ka=even_a_ka[j],
                      a_rk=even_a_rk[j], a_ln_w=even_a_ln_w[j], a_ln_b=even_a_ln_b[j])
            w_in = even_w_in[j]
            w_perm = jnp.concatenate([w_in[:, :3 * a_width], w_in[:, a_cols:], w_in[:, 3 * a_width:a_cols]],
                                     axis=1).astype(BF16)
            offs = dict(q=3 * a_width, q_w=b_width, kv=3 * a_width + b_width, kv_w=2 * kv_width,
                        low=3 * a_width + b_width + 2 * kv_width, low_w=a_cols - 3 * a_width)
            assert a_width == b_width and offs['kv'] % offs['kv_w'] == 0 and offs['low'] % offs['low_w'] == 0
            z, _ = _nm_matmul(x, norm1[layer], mod3, 1, 0, seg, w_perm, tm)
            tmr = _row_tile(sp, ss, cap=256)
            nt = t_all // tmr
            starts = np.concatenate([np.arange(0, tp, sp), tp + np.arange(0, ts, ss), [t_all]])
            tile0 = np.arange(nt) * tmr
            keep_prev = jnp.asarray(~np.isin(tile0, starts), F32)[:, None]
            keep_next = jnp.asarray(~np.isin(tile0 + tmr, starts), F32)[:, None]
            zt = z.reshape(nt, tmr, z.shape[-1])
            zero_row = jnp.zeros((1, z.shape[-1]), F32)
            prev_rows = (jnp.concatenate([zero_row, zt[:-1, tmr - 1]], axis=0) * keep_prev)[:, None, :]
            next_rows = (jnp.concatenate([zt[1:, 0], zero_row], axis=0) * keep_next)[:, None, :]
            r, kk, vv, w0, w1, b0, b1, kt0, kt1, g, bonus = _rwkv_pre(z, prev_rows, next_rows, PA, tmr, offs)
            outs = []
            for (row0, bn, seq, latent) in ((0, bp, sp, False), (tp, bs, ss, True)):
                rows = bn * seq
                q, k, v, k_f32, v_f32 = _attn_pre(z, even_b_qnorm[j], even_b_knorm[j], offs, row0, rows, seq,
                                                  tmr, latent)
                k = k.reshape(bn, seq, kv_width)
                v = v.reshape(bn, seq, kv_width)
                if latent:
                    past = cache_b_k.shape[2]
                    k = jnp.concatenate([k, cache_b_k[:, j].astype(BF16).reshape(bn, past, kv_width)], axis=1)
                    v = jnp.concatenate([v, cache_b_v[:, j].astype(BF16).reshape(bn, past, kv_width)], axis=1)
                    s0 = state_a[:, j]
                else:
                    s0 = None
                    new_k.append(k_f32.reshape(bn, seq, kvh, dh))
                    new_v.append(v_f32.reshape(bn, seq, kvh, dh))
                y_b = _attend(q.reshape(bn, seq, b_width), k, v, dh).reshape(rows, b_width)
                yf, yb, s_fin = _rwkv_scan_pass(r, kk, vv, (w0, w1), (b0, b1), (kt0, kt1), s0, row0, bn, seq)
                if not latent:
                    new_s.append(s_fin)
                y_a = _rwkv_post(yf, yb, bonus, g, row0, PA, tmr)
                outs.append(jnp.concatenate([y_a, y_b], axis=-1))
            mix_in = jnp.concatenate(outs, axis=0)
            x = _res_matmul(mix_in, even_w_out[j].astype(BF16), x, mod3, 2, seg, tm)
        else:
            PC = dict(c_fw1=odd_c_fw1[j], c_fb1=odd_c_fb1[j], c_freq=odd_c_freq[j], c_fw2=odd_c_fw2[j],
                      c_fb2=odd_c_fb2[j], c_fw3=odd_c_fw3[j])
            u_pre, _ = _nm_matmul(x, norm1[layer], mod3, 1, 0, seg, odd_w_in[j].astype(BF16), tm)
            taps = odd_c_conv[j]
            tap_bias = odd_c_conv_b[j].reshape(1, 3 * c_width)
            zs = []
            for (row0, bn, seq) in ((0, bp, sp), (tp, bs, ss)):
                tc = _col_tile(c_width, max(LANES, HY_TILE_ELEMS // seq))
                spectra = _hyena_spectra(seq, PC, c_width, tc)
                z1 = _hyena_conv(u_pre, 2 * c_width, u_pre, 0, taps, tap_bias, odd_c_bias[j, 0:1], spectra, 0,
                                 row0, bn, seq, tc, True)
                z2 = _hyena_conv(z1.reshape(bn * seq, c_width), 0, u_pre, c_width, taps, tap_bias,
                                 odd_c_bias[j, 1:2], spectra, 1, row0, bn, seq, tc, False)
                zs.append(z2.reshape(bn * seq, c_width))
            x = _res_matmul(jnp.concatenate(zs, axis=0).astype(BF16), odd_w_out[j].astype(BF16), x, mod3, 2, seg, tm)
        q, hm = _nm_matmul(x, norm2[layer], mod3, 4, 3, seg, peer_wq[layer].astype(BF16), tm)
        i1, i2, gate = _peer_route(q, peer_keys[layer])
        gmat = _gate_matrix(i1, i2, gate, nkeys)
        u_bf16, v_bf16 = _to_bf16(peer_u, layer), _to_bf16(peer_v, layer)
        if layer < depth - 1:
            x = _peer_experts(hm, gmat, u_bf16, v_bf16, x, mod3, 5, seg, tm)
        else:
            y_prompt = _peer_experts(hm, gmat, u_bf16, v_bf16, x, mod3, 5, seg, tm, 0, tp).reshape(bp, sp, d)
            y_sample = _peer_experts(hm, gmat, u_bf16, v_bf16, x, mod3, 5, seg, tm, tp, ts).reshape(bs, ss, d)
    return (y_prompt, y_sample, jnp.stack(new_k, axis=1), jnp.stack(new_v, axis=1), jnp.stack(new_s, axis=1))
```

```python
import functools
import math

import numpy as np
import jax
import jax.numpy as jnp
from jax import lax
from jax.experimental import pallas as pl
from jax.experimental.pallas import tpu as pltpu

F32 = jnp.float32
BF16 = jnp.bfloat16
I32 = jnp.int32

NORM_EPS = 1e-6
A_HEAD_DIM = 64
A_GN_EPS = 64e-5
B_GROUP = 4
GRID_W = 64
ROPE_THETA = 10000.0
HYENA_ORDER = 2
DECAY_TARGET = 1e-2
FAST_DECAY_PCT = 0.3
SLOW_DECAY_PCT = 1.5
PEER_TOPK = 16
LANES = 128
VMEM_LIMIT = 56 * 1024 * 1024


def _cparams(sem):
    return pltpu.CompilerParams(dimension_semantics=sem, vmem_limit_bytes=VMEM_LIMIT)


def _row_tile(*lengths, cap=512):
    t = cap
    while any(n % t for n in lengths):
        t //= 2
    return t


def _col_tile(n, cap):
    return max(t for t in range(LANES, cap + 1, LANES) if n % t == 0)


def _mod_kernel(c_ref, w_ref, b_ref, o_ref):
    c = c_ref[...]
    s = (c * jax.nn.sigmoid(c)).astype(BF16)
    o_ref[0] = jnp.dot(s, w_ref[0].astype(BF16), preferred_element_type=F32) + b_ref[0]


def _mod_table(cond8, mod_w, mod_b):
    depth, d, n = mod_w.shape
    tn = _row_tile(n, cap=1024)
    return pl.pallas_call(
        _mod_kernel,
        grid=(depth, n // tn),
        in_specs=[pl.BlockSpec((8, d), lambda l, j: (0, 0)),
                  pl.BlockSpec((1, d, tn), lambda l, j: (l, 0, j)),
                  pl.BlockSpec((1, 1, tn), lambda l, j: (l, 0, j))],
        out_specs=pl.BlockSpec((1, 8, tn), lambda l, j: (l, 0, j)),
        out_shape=jax.ShapeDtypeStruct((depth, 8, n), F32),
        compiler_params=_cparams(("parallel", "parallel")),
        name="mod_table",
    )(cond8, mod_w, mod_b.reshape(depth, 1, n))


def _nm_matmul_kernel(seg_ref, x_ref, g_ref, sc_ref, sh_ref, w_ref, o_ref, h_ref, h_scr):
    del seg_ref

    @pl.when(pl.program_id(1) == 0)
    def _():
        x = x_ref[...]
        y = x * lax.rsqrt(jnp.mean(x * x, axis=-1, keepdims=True) + NORM_EPS) * g_ref[...]
        h = (y * (1.0 + sc_ref[0]) + sh_ref[0]).astype(BF16)
        h_scr[...] = h
        h_ref[...] = h

    o_ref[...] = jnp.dot(h_scr[...], w_ref[...], preferred_element_type=F32).astype(o_ref.dtype)


def _nm_matmul(x, g, mod3, sc_idx, sh_idx, seg, w_bf16, tm, out_dtype=F32):
    t, d = x.shape
    n = w_bf16.shape[1]
    tn = _col_tile(n, 1664)
    grid_spec = pltpu.PrefetchScalarGridSpec(
        num_scalar_prefetch=1,
        grid=(t // tm, n // tn),
        in_specs=[pl.BlockSpec((tm, d), lambda i, j, s: (i, 0)),
                  pl.BlockSpec((1, d), lambda i, j, s: (0, 0)),
                  pl.BlockSpec((1, 1, d), lambda i, j, s: (s[i] * 6 + sc_idx, 0, 0)),
                  pl.BlockSpec((1, 1, d), lambda i, j, s: (s[i] * 6 + sh_idx, 0, 0)),
                  pl.BlockSpec((d, tn), lambda i, j, s: (0, j))],
        out_specs=[pl.BlockSpec((tm, tn), lambda i, j, s: (i, j)),
                   pl.BlockSpec((tm, d), lambda i, j, s: (i, 0))],
        scratch_shapes=[pltpu.VMEM((tm, d), BF16)])
    return pl.pallas_call(
        _nm_matmul_kernel,
        grid_spec=grid_spec,
        out_shape=[jax.ShapeDtypeStruct((t, n), out_dtype), jax.ShapeDtypeStruct((t, d), BF16)],
        compiler_params=_cparams(("parallel", "arbitrary")),
        name="norm_mod_matmul",
    )(seg, x, g.reshape(1, d), mod3, mod3, w_bf16)


def _res_matmul_kernel(seg_ref, a_ref, w_ref, r_ref, gt_ref, o_ref):
    del seg_ref
    mm = jnp.dot(a_ref[...], w_ref[...], preferred_element_type=F32)
    o_ref[...] = r_ref[...] + gt_ref[0] * mm


def _res_matmul(a_bf16, w_bf16, res, mod3, gt_idx, seg, tm):
    t, k = a_bf16.shape
    n = w_bf16.shape[1]
    tn = _col_tile(n, 1024)
    grid_spec = pltpu.PrefetchScalarGridSpec(
        num_scalar_prefetch=1,
        grid=(t // tm, n // tn),
        in_specs=[pl.BlockSpec((tm, k), lambda i, j, s: (i, 0)),
                  pl.BlockSpec((k, tn), lambda i, j, s: (0, j)),
                  pl.BlockSpec((tm, tn), lambda i, j, s: (i, j)),
                  pl.BlockSpec((1, 1, tn), lambda i, j, s: (s[i] * 6 + gt_idx, 0, j))],
        out_specs=pl.BlockSpec((tm, tn), lambda i, j, s: (i, j)))
    return pl.pallas_call(
        _res_matmul_kernel,
        grid_spec=grid_spec,
        out_shape=jax.ShapeDtypeStruct((t, n), F32),
        compiler_params=_cparams(("parallel", "parallel")),
        name="res_matmul",
    )(seg, a_bf16, w_bf16, res, mod3)


def _scan_kernel(rf_ref, kkf_ref, vf_ref, wf_ref, bf_ref, ktf_ref, rb_ref, kkb_ref, vb_ref, wb_ref, bb_ref,
                 ktb_ref, s0_ref, yf_ref, yb_ref, sfin_ref, s_scr, *, tb_steps, nk, kq):
    ti = pl.program_id(1)

    @pl.when(ti == 0)
    def _():
        s_scr[...] = s0_ref[...]

    def tree(parts):
        while len(parts) > 1:
            parts = [parts[i] + parts[i + 1] for i in range(0, len(parts), 2)]
        return parts[0]

    def all_parts(p):
        part = LANES // kq
        return tree([p] + [pltpu.roll(p, i * part, 1) for i in range(1, kq)])

    nacc = 4

    def advance(d, t, r_ref, kk_ref, v_ref, w_ref, b_ref, kt_ref, y_ref):
        row = pl.ds(t, 1)
        accs = [None] * nacc
        for k in range(nk):
            p = s_scr[d, k] * kk_ref[k, row, :]
            accs[k % nacc] = p if accs[k % nacc] is None else accs[k % nacc] + p
        sa = all_parts(tree(accs))
        v = v_ref[t]
        yacc = [None] * nacc
        for k in range(nk):
            s_new = s_scr[d, k] * w_ref[k, row, :] - sa * b_ref[k, row, :] + v * kt_ref[k, row, :]
            s_scr[d, k] = s_new
            p = s_new * r_ref[k, row, :]
            yacc[k % nacc] = p if yacc[k % nacc] is None else yacc[k % nacc] + p
        y_ref[t] = all_parts(tree(yacc))

    def step(s, carry):
        advance(0, s, rf_ref, kkf_ref, vf_ref, wf_ref, bf_ref, ktf_ref, yf_ref)
        advance(1, tb_steps - 1 - s, rb_ref, kkb_ref, vb_ref, wb_ref, bb_ref, ktb_ref, yb_ref)
        return carry

    lax.fori_loop(0, tb_steps, step, 0, unroll=2)

    @pl.when(ti == pl.num_programs(1) - 1)
    def _():
        sfin_ref[...] = s_scr[...]


def _scan(r, kk, v, w2, b2, kt2, s0, kq):
    nk, seq, c = r.shape
    nv = v.shape[1]
    tb_steps = _row_tile(seq, cap=16)
    nt = seq // tb_steps
    rowf = pl.BlockSpec((nk, tb_steps, LANES), lambda gi, ti: (0, ti, gi))
    rowb = pl.BlockSpec((nk, tb_steps, LANES), lambda gi, ti: (0, nt - 1 - ti, gi))
    valf = pl.BlockSpec((tb_steps, nv, LANES), lambda gi, ti: (ti, 0, gi))
    valb = pl.BlockSpec((tb_steps, nv, LANES), lambda gi, ti: (nt - 1 - ti, 0, gi))
    st = pl.BlockSpec((2, nk, nv, LANES), lambda gi, ti: (0, 0, 0, gi))
    return pl.pallas_call(
        functools.partial(_scan_kernel, tb_steps=tb_steps, nk=nk, kq=kq),
        grid=(c // LANES, nt),
        in_specs=[rowf, rowf, valf, rowf, rowf, rowf, rowb, rowb, valb, rowb, rowb, rowb, st],
        out_specs=[valf, valb, st],
        out_shape=[jax.ShapeDtypeStruct((seq, nv, c), F32), jax.ShapeDtypeStruct((seq, nv, c), F32),
                   jax.ShapeDtypeStruct((2, nk, nv, c), F32)],
        scratch_shapes=[pltpu.VMEM((2, nk, nv, LANES), F32)],
        compiler_params=_cparams(("parallel", "arbitrary")),
        name="rwkv7_scan",
    )(r, kk, v, w2[0], b2[0], kt2[0], r, kk, v, w2[1], b2[1], kt2[1], s0)


CHAIN_TB = 128


def _to_chain_kernel(x_ref, o_ref, a_scr, *stage, heads, nk, kq, values):
    bg, tb = x_ref.shape[0], x_ref.shape[1]
    n = A_HEAD_DIM
    for b in range(bg):
        xt = x_ref[b].T
        a_scr[b] = jnp.swapaxes(xt.reshape(heads, n, tb), 0, 1)
    for p in range(n if values else nk):
        pieces = [a_scr[b, p if values else q * nk + p] for q in range(kq) for b in range(bg)]
        tile = jnp.concatenate(pieces, axis=0).T
        if values:
            stage[0][p] = tile
        else:
            o_ref[p] = tile
    if values:
        o_ref[...] = jnp.swapaxes(stage[0][...], 0, 1)


def _to_chain(x, row0, bn, seq, kq, values):
    w_ = x.shape[-1]
    heads = w_ // A_HEAD_DIM
    n = A_HEAD_DIM
    nk = n // kq
    bg = LANES // (kq * heads)
    ng = bn // bg
    c = kq * bn * heads
    assert kq * bg * heads == LANES and c == ng * LANES and row0 % (bg * seq) == 0
    tb = min(CHAIN_TB, seq)
    g0 = row0 // (bg * seq)
    scratch = [pltpu.VMEM((bg, n, heads, tb), F32)]
    if values:
        out_spec = pl.BlockSpec((tb, n, LANES), lambda g, t: (t, 0, g))
        out_shape = jax.ShapeDtypeStruct((seq, n, c), F32)
        scratch.append(pltpu.VMEM((n, tb, LANES), F32))
    else:
        out_spec = pl.BlockSpec((nk, tb, LANES), lambda g, t: (0, t, g))
        out_shape = jax.ShapeDtypeStruct((nk, seq, c), F32)
    return pl.pallas_call(
        functools.partial(_to_chain_kernel, heads=heads, nk=nk, kq=kq, values=values),
        grid=(ng, seq // tb),
        in_specs=[pl.BlockSpec((bg, tb, w_), lambda g, t: (g0 + g, t, 0))],
        out_specs=out_spec,
        out_shape=out_shape,
        scratch_shapes=scratch,
        compiler_params=_cparams(("parallel", "parallel")),
        name="to_chain_layout",
    )(x.reshape(-1, seq, w_))


def _from_chain_kernel(y_ref, o_ref, stage, a_scr, *, heads):
    bg = o_ref.shape[0]
    n = A_HEAD_DIM
    tb = y_ref.shape[0]
    stage[...] = jnp.swapaxes(y_ref[...], 0, 1)
    for v in range(n):
        rows = stage[v].T
        for b in range(bg):
            a_scr[b, v] = rows[b * heads:(b + 1) * heads]
    for b in range(bg):
        o_ref[b] = jnp.swapaxes(a_scr[b], 0, 1).reshape(heads * n, tb).T


def _from_chain(y, bn, heads, kq):
    seq, n, c = y.shape
    bg = LANES // (kq * heads)
    tb = min(CHAIN_TB, seq)
    out = pl.pallas_call(
        functools.partial(_from_chain_kernel, heads=heads),
        grid=(c // LANES, seq // tb),
        in_specs=[pl.BlockSpec((tb, n, LANES), lambda g, t: (t, 0, g))],
        out_specs=pl.BlockSpec((bg, tb, heads * n), lambda g, t: (g, t, 0)),
        out_shape=jax.ShapeDtypeStruct((bn, seq, heads * n), F32),
        scratch_shapes=[pltpu.VMEM((n, tb, LANES), F32), pltpu.VMEM((bg, n, heads, tb), F32)],
        compiler_params=_cparams(("parallel", "parallel")),
        name="from_chain_layout",
    )(y)
    return out.reshape(bn * seq, heads * n)


def _attn_kernel(q_ref, k_ref, v_ref, o_ref, *, dh):
    k = k_ref[0]
    v = v_ref[0]
    for g in range(B_GROUP):
        q = q_ref[0, :, g * dh:(g + 1) * dh]
        s = lax.dot_general(q, k, (((1,), (1,)), ((), ())), preferred_element_type=F32)
        m = jnp.max(s, axis=-1, keepdims=True)
        p = jnp.exp(s - m)
        den = jnp.sum(p, axis=-1, keepdims=True)
        o = jnp.dot(p.astype(BF16), v, preferred_element_type=F32) / den
        o_ref[0, :, g * dh:(g + 1) * dh] = o.astype(o_ref.dtype)


def _attend(q, k, v, dh):
    bn, lq, qw = q.shape
    lk = k.shape[1]
    kvh = k.shape[2] // dh
    gw = B_GROUP * dh
    tq = _row_tile(lq, cap=256)
    return pl.pallas_call(
        functools.partial(_attn_kernel, dh=dh),
        grid=(bn, kvh, lq // tq),
        in_specs=[pl.BlockSpec((1, tq, gw), lambda b, h, i: (b, i, h)),
                  pl.BlockSpec((1, lk, dh), lambda b, h, i: (b, 0, h)),
                  pl.BlockSpec((1, lk, dh), lambda b, h, i: (b, 0, h))],
        out_specs=pl.BlockSpec((1, tq, gw), lambda b, h, i: (b, i, h)),
        out_shape=jax.ShapeDtypeStruct((bn, lq, qw), BF16),
        compiler_params=_cparams(("parallel", "parallel", "parallel")),
        name="attention",
    )(q, k, v)


G_GROUP = 16


def _gate_matrix_kernel(i1_ref, i2_ref, g_ref, o_ref, gtmp_scr, *, nkeys):
    iota = lax.broadcasted_iota(I32, (nkeys, nkeys), 0)

    def build(grp, carry):
        base = pl.multiple_of(grp * G_GROUP, G_GROUP)
        def token(tt, c2):
            t = base + tt
            a_t = jnp.where(iota == i1_ref[pl.ds(t, 1), :], 1.0, 0.0).astype(BF16)
            b_t = jnp.where(iota == i2_ref[pl.ds(t, 1), :], g_ref[pl.ds(t, 1), :], 0.0).astype(BF16)
            gtmp_scr[tt] = lax.dot_general(a_t, b_t, (((1,), (1,)), ((), ())), preferred_element_type=F32)
            return c2

        lax.fori_loop(0, G_GROUP, token, 0, unroll=G_GROUP)
        by_n1 = jnp.swapaxes(gtmp_scr[...], 0, 1).astype(BF16)
        for n1 in range(nkeys):
            o_ref[pl.ds(base, G_GROUP), n1 * nkeys:(n1 + 1) * nkeys] = by_n1[n1]
        return carry

    lax.fori_loop(0, o_ref.shape[0] // G_GROUP, build, 0)


def _gate_matrix(i1, i2, gate, nkeys):
    t, nj = i1.shape
    tb = LANES
    sel = pl.BlockSpec((tb, nj), lambda i: (i, 0))
    return pl.pallas_call(
        functools.partial(_gate_matrix_kernel, nkeys=nkeys),
        grid=(t // tb,),
        in_specs=[sel, sel, sel],
        out_specs=pl.BlockSpec((tb, nkeys * nkeys), lambda i: (i, 0)),
        out_shape=jax.ShapeDtypeStruct((t, nkeys * nkeys), BF16),
        scratch_shapes=[pltpu.VMEM((G_GROUP, nkeys, nkeys), F32)],
        compiler_params=_cparams(("parallel",)),
        name="peer_gate_matrix",
    )(i1, i2, gate)


def _to_bf16_kernel(x_ref, o_ref):
    o_ref[...] = x_ref[0].astype(BF16)


def _to_bf16(tables, layer):
    _, e, d = tables.shape
    te = _row_tile(e, cap=1024)
    return pl.pallas_call(
        _to_bf16_kernel,
        grid=(e // te,),
        in_specs=[pl.BlockSpec((1, te, d), lambda i: (layer, i, 0))],
        out_specs=pl.BlockSpec((te, d), lambda i: (i, 0)),
        out_shape=jax.ShapeDtypeStruct((e, d), BF16),
        compiler_params=_cparams(("parallel",)),
        name="to_bf16",
    )(tables)


def _peer_kernel(seg_ref, xb_ref, gm_ref, u_ref, v_ref, r_ref, gt_ref, o_ref, acc_scr):
    del seg_ref
    e = pl.program_id(1)

    @pl.when(e == 0)
    def _():
        acc_scr[...] = jnp.zeros_like(acc_scr)

    h = lax.dot_general(xb_ref[...], u_ref[...], (((1,), (1,)), ((), ())), preferred_element_type=F32)
    act = 0.5 * h * (1.0 + lax.erf(h * (1.0 / math.sqrt(2.0)))) * gm_ref[...].astype(F32)
    acc_scr[...] += jnp.dot(act.astype(BF16), v_ref[...], preferred_element_type=F32)

    @pl.when(e == pl.num_programs(1) - 1)
    def _():
        o_ref[...] = r_ref[...] + gt_ref[0] * acc_scr[...]


def _peer_experts(xb, gmat, u_bf16, v_bf16, res, mod3, gt_idx, seg, tm, row0=0, rows=None):
    t, d = xb.shape
    rows = t if rows is None else rows
    r0 = row0 // tm
    ne = u_bf16.shape[0]
    te = 1024
    grid_spec = pltpu.PrefetchScalarGridSpec(
        num_scalar_prefetch=1,
        grid=(rows // tm, ne // te),
        in_specs=[pl.BlockSpec((tm, d), lambda i, e, s: (r0 + i, 0)),
                  pl.BlockSpec((tm, te), lambda i, e, s: (r0 + i, e)),
                  pl.BlockSpec((te, d), lambda i, e, s: (e, 0)),
                  pl.BlockSpec((te, d), lambda i, e, s: (e, 0)),
                  pl.BlockSpec((tm, d), lambda i, e, s: (r0 + i, 0)),
                  pl.BlockSpec((1, 1, d), lambda i, e, s: (s[r0 + i] * 6 + gt_idx, 0, 0))],
        out_specs=pl.BlockSpec((tm, d), lambda i, e, s: (i, 0)),
        scratch_shapes=[pltpu.VMEM((tm, d), F32)])
    return pl.pallas_call(
        _peer_kernel,
        grid_spec=grid_spec,
        out_shape=jax.ShapeDtypeStruct((rows, d), F32),
        compiler_params=_cparams(("parallel", "arbitrary")),
        name="peer_experts",
    )(seg, xb, gmat, u_bf16, v_bf16, res, mod3)


def _topk_rows(vals, payload, rows_out):
    big = jnp.float32(2 ** 30)
    top_v = jnp.zeros(rows_out.shape, F32)
    top_p = jnp.zeros(rows_out.shape, F32)
    for it in range(PEER_TOPK):
        m = jnp.max(vals, axis=0, keepdims=True)
        sel = jnp.min(jnp.where(vals == m, payload, big), axis=0, keepdims=True)
        top_v = jnp.where(rows_out == it, m, top_v)
        top_p = jnp.where(rows_out == it, sel, top_p)
        vals = jnp.where(payload == sel, -jnp.inf, vals)
    return top_v, top_p


def _top_pair_sums(s1, s2, rows_out):
    k = PEER_TOPK
    half = k // 2
    tt = s1.shape[1]
    big = jnp.float32(2 ** 30)
    lists = [s1[:half] + s2[j:j + 1] for j in range(k)]
    singles = s1[half:] + s2[0:1]
    sub = lax.broadcasted_iota(I32, (half, tt), 0).astype(F32)
    head_id = sub * k
    single_id = (sub + half) * k
    top_v = jnp.zeros(rows_out.shape, F32)
    top_p = jnp.zeros(rows_out.shape, F32)
    for it in range(k):
        m = jnp.max(jnp.maximum(lists[0], singles), axis=0, keepdims=True)
        sel = jnp.min(jnp.minimum(jnp.where(lists[0] == m, head_id, big), jnp.where(singles == m, single_id, big)),
                      axis=0, keepdims=True)
        top_v = jnp.where(rows_out == it, m, top_v)
        top_p = jnp.where(rows_out == it, sel, top_p)
        pop = head_id == sel
        lists = [jnp.where(pop, lists[j + 1], lists[j]) for j in range(k - 1)] + [jnp.where(pop, -jnp.inf, lists[-1])]
        head_id = jnp.where(pop, head_id + 1.0, head_id)
        singles = jnp.where(single_id == sel, -jnp.inf, singles)
    return top_v, top_p


def _gather_rows(table, sel):
    out = jnp.zeros(sel.shape, table.dtype)
    for i in range(PEER_TOPK):
        out = jnp.where(sel == i, table[i:i + 1, :], out)
    return out


def _route_kernel(q_ref, keys_ref, i1_ref, i2_ref, g_ref, n1_scr, n2_scr, gate_scr, *, heads, nkeys, dq):
    tt = q_ref.shape[0]
    k = PEER_TOPK
    n_iota = lax.broadcasted_iota(I32, (nkeys, tt), 0).astype(F32)
    rows_out = lax.broadcasted_iota(I32, (k, tt), 0)

    def head(h, carry):
        tops = []
        for c in range(2):
            col = pl.multiple_of((h * 2 + c) * dq, dq)
            qhc = q_ref[:, pl.ds(col, dq)].astype(BF16)
            khc = keys_ref[h, c].astype(BF16)
            s = lax.dot_general(khc, qhc, (((1,), (1,)), ((), ())), preferred_element_type=F32)
            tops.append(_topk_rows(s, n_iota, rows_out))
        (s1, i1), (s2, i2) = tops
        top, ci = _top_pair_sums(s1, s2, rows_out)
        ci = ci.astype(I32)
        n1 = _gather_rows(i1, lax.shift_right_logical(ci, 4)).astype(I32)
        n2 = _gather_rows(i2, lax.bitwise_and(ci, k - 1)).astype(I32)
        ex = jnp.exp(top - top[0:1, :])
        gate = ex / jnp.sum(ex, axis=0, keepdims=True)
        row = pl.multiple_of(h * k, k)
        n1_scr[pl.ds(row, k), :] = n1
        n2_scr[pl.ds(row, k), :] = n2
        gate_scr[pl.ds(row, k), :] = gate
        return carry

    lax.fori_loop(0, heads, head, 0)
    i1_ref[...] = n1_scr[...].T
    i2_ref[...] = n2_scr[...].T
    g_ref[...] = gate_scr[...].T


def _peer_route(q, sub_keys):
    t, qw = q.shape
    heads, _, nkeys, dq = sub_keys.shape
    hk = heads * PEER_TOPK
    tt = LANES
    out = pl.BlockSpec((tt, hk), lambda i: (i, 0))
    return pl.pallas_call(
        functools.partial(_route_kernel, heads=heads, nkeys=nkeys, dq=dq),
        grid=(t // tt,),
        in_specs=[pl.BlockSpec((tt, qw), lambda i: (i, 0)),
                  pl.BlockSpec((heads, 2, nkeys, dq), lambda i: (0, 0, 0, 0))],
        out_specs=[out, out, out],
        out_shape=[jax.ShapeDtypeStruct((t, hk), I32), jax.ShapeDtypeStruct((t, hk), I32),
                   jax.ShapeDtypeStruct((t, hk), F32)],
        scratch_shapes=[pltpu.VMEM((hk, tt), I32), pltpu.VMEM((hk, tt), I32), pltpu.VMEM((hk, tt), F32)],
        compiler_params=_cparams(("parallel",)),
        name="peer_route",
    )(q, sub_keys)


def _split(x):
    hi = x.astype(BF16)
    return hi, (x - hi.astype(F32)).astype(BF16)


def _mm(a, b):
    return jnp.dot(a, b, preferred_element_type=F32)


def _head_sum(x, ones_blockdiag):
    hi, lo = _split(x)
    return _mm(hi, ones_blockdiag) + _mm(lo, ones_blockdiag)


def _shift_rows(x, first_row, last_row):
    n = x.shape[0]
    rows = lax.broadcasted_iota(I32, x.shape, 0)
    prev = jnp.where(rows == 0, first_row, pltpu.roll(x, 1, 0))
    nxt = jnp.where(rows == n - 1, last_row, pltpu.roll(x, n - 1, 0))
    return prev, nxt


def _conv3(x, first_row, last_row, taps):
    prev, nxt = _shift_rows(x, first_row, last_row)
    return prev * taps[0:1] + x * taps[1:2] + nxt * taps[2:3]


def _rwkv_pre_kernel(r_ref, k_ref, v_ref, low_ref, pr_ref, pk_ref, pv_ref, plow_ref, nr_ref, nk_ref, nv_ref,
                     nlow_ref, cr_ref, ck_ref, cv_ref, clow_ref, w0_ref, wu_ref, a0_ref, au_ref, gu_ref,
                     kkg_ref, ka_ref, rk_ref, ones_ref,
                     ro_ref, kko_ref, vo_ref, w0o_ref, w1o_ref, b0o_ref, b1o_ref, kt0o_ref, kt1o_ref,
                     go_ref, bonus_ref, *, rw, ra):
    r = _conv3(r_ref[...], pr_ref[0], nr_ref[0], cr_ref[...])
    k = _conv3(k_ref[...], pk_ref[0], nk_ref[0], ck_ref[...])
    v = _conv3(v_ref[...], pv_ref[0], nv_ref[0], cv_ref[...])
    low = _conv3(low_ref[...], plow_ref[0], nlow_ref[0], clow_ref[...])
    ones = ones_ref[...]
    kk = k * kkg_ref[...]
    kk = kk * lax.rsqrt(_head_sum(kk * kk, ones) + 1e-12)
    ro_ref[...] = r
    kko_ref[...] = kk
    vo_ref[...] = v
    bonus_ref[...] = _head_sum(r * k * rk_ref[...], ones) * v
    gd = low[:, 2 * rw + 2 * ra:]
    go_ref[...] = _mm(jax.nn.sigmoid(gd).astype(BF16), gu_ref[...])
    for d_, (wo, bo, kto) in enumerate(((w0o_ref, b0o_ref, kt0o_ref), (w1o_ref, b1o_ref, kt1o_ref))):
        wd = low[:, d_ * rw:(d_ + 1) * rw]
        ad = low[:, 2 * rw + d_ * ra:2 * rw + (d_ + 1) * ra]
        lw = w0_ref[d_:d_ + 1, :] + _mm(jnp.tanh(wd).astype(BF16), wu_ref[d_])
        softplus = jnp.maximum(-lw, 0.0) + jnp.log1p(jnp.exp(-jnp.abs(lw)))
        wo[...] = jnp.exp(-jnp.exp(-softplus - 0.5))
        a = jax.nn.sigmoid(a0_ref[d_:d_ + 1, :] + _mm(ad.astype(BF16), au_ref[d_]))
        bo[...] = kk * a
        kto[...] = k * (1.0 + (a - 1.0) * ka_ref[...])


def _rwkv_pre(z, prev_rows, next_rows, P, tm, offs):
    t = z.shape[0]
    w_ = P['a_w0'].shape[-1]
    rw, ra = P['a_wu'].shape[1], P['a_au'].shape[1]
    lw = offs['low_w']
    li = offs['low'] // lw
    row = lambda c, wd: pl.BlockSpec((tm, wd), lambda i: (i, c))
    edge = lambda c, wd: pl.BlockSpec((1, 1, wd), lambda i: (i, 0, c))
    full = lambda a: pl.BlockSpec(a.shape, lambda i: (0,) * a.ndim)
    conv = P['a_conv']
    consts = [conv[:, :w_], conv[:, w_:2 * w_], conv[:, 2 * w_:3 * w_], conv[:, 3 * w_:],
              P['a_w0'], P['a_wu'].astype(BF16), P['a_a0'], P['a_au'].astype(BF16), P['a_gu'].astype(BF16),
              P['a_kk'].reshape(1, w_), P['a_ka'].reshape(1, w_), P['a_rk'].reshape(1, w_),
              jnp.asarray(_blockdiag_ones(w_), BF16)]
    out = pl.BlockSpec((tm, w_), lambda i: (i, 0))
    return pl.pallas_call(
        functools.partial(_rwkv_pre_kernel, rw=rw, ra=ra),
        grid=(t // tm,),
        in_specs=[row(0, w_), row(1, w_), row(2, w_), row(li, lw),
                  edge(0, w_), edge(1, w_), edge(2, w_), edge(li, lw),
                  edge(0, w_), edge(1, w_), edge(2, w_), edge(li, lw)] + [full(a) for a in consts],
        out_specs=[out] * 11,
        out_shape=[jax.ShapeDtypeStruct((t, w_), F32)] * 11,
        compiler_params=_cparams(("parallel",)),
        name="rwkv7_pre",
    )(z, z, z, z, prev_rows, prev_rows, prev_rows, prev_rows, next_rows, next_rows, next_rows, next_rows, *consts)


def _blockdiag_ones(width):
    idx = np.arange(width) // A_HEAD_DIM
    return (idx[:, None] == idx[None, :]).astype(np.float32)


def _rwkv_post_kernel(yf_ref, yb_ref, bonus_ref, g_ref, lnw_ref, lnb_ref, ones_ref, o_ref):
    ones = ones_ref[...]
    y = yf_ref[...] + yb_ref[...]
    mu = _head_sum(y, ones) * (1.0 / A_HEAD_DIM)
    dlt = y - mu
    var = _head_sum(dlt * dlt, ones) * (1.0 / A_HEAD_DIM)
    yn = dlt * lax.rsqrt(var + A_GN_EPS) * lnw_ref[...] + lnb_ref[...]
    o_ref[...] = ((yn + bonus_ref[...]) * g_ref[...]).astype(o_ref.dtype)


def _rwkv_post(yf, yb, bonus, g, row0, P, tm):
    t, w_ = yf.shape
    r0 = row0 // tm
    row = pl.BlockSpec((tm, w_), lambda i: (i, 0))
    off = pl.BlockSpec((tm, w_), lambda i: (r0 + i, 0))
    vec = pl.BlockSpec((1, w_), lambda i: (0, 0))
    return pl.pallas_call(
        _rwkv_post_kernel,
        grid=(t // tm,),
        in_specs=[row, row, off, off, vec, vec, pl.BlockSpec((w_, w_), lambda i: (0, 0))],
        out_specs=row,
        out_shape=jax.ShapeDtypeStruct((t, w_), BF16),
        compiler_params=_cparams(("parallel",)),
        name="rwkv7_post",
    )(yf, yb, bonus, g, P['a_ln_w'].reshape(1, w_), P['a_ln_b'].reshape(1, w_),
      jnp.asarray(_blockdiag_ones(w_), BF16))


def _attn_pre_kernel(q_ref, kv_ref, qn_ref, kn_ref, cos_ref, sin_ref, qo_ref, ko_ref, vo_ref, kf_ref, vf_ref,
                     *, dh, latent):
    def rms(x, g):
        return x * lax.rsqrt(jnp.mean(x * x, axis=-1, keepdims=True) + NORM_EPS) * g

    def rope(x):
        if not latent:
            return x
        lanes = lax.broadcasted_iota(I32, x.shape, 1)
        quarter = dh // 4
        partner = jnp.where(lanes % (2 * quarter) < quarter, pltpu.roll(x, dh - quarter, 1), pltpu.roll(x, quarter, 1))
        return x * cos_ref[...] + partner * sin_ref[...]

    nq = q_ref.shape[1] // dh
    nkv = kv_ref.shape[1] // (2 * dh)
    for h in range(nq):
        q = rope(rms(q_ref[:, h * dh:(h + 1) * dh], qn_ref[...]))
        qo_ref[:, h * dh:(h + 1) * dh] = (q * dh ** -0.5).astype(BF16)
    for h in range(nkv):
        k = rms(kv_ref[:, h * dh:(h + 1) * dh], kn_ref[...])
        v = kv_ref[:, (nkv + h) * dh:(nkv + h + 1) * dh]
        kf_ref[:, h * dh:(h + 1) * dh] = k
        vf_ref[:, h * dh:(h + 1) * dh] = v
        ko_ref[:, h * dh:(h + 1) * dh] = rope(k).astype(BF16)
        vo_ref[:, h * dh:(h + 1) * dh] = v.astype(BF16)


def _rope_tables(seq, dh):
    quarter = dh // 4
    inv = ROPE_THETA ** (-np.arange(quarter, dtype=np.float64) / quarter)
    pos = np.arange(seq)
    ang_r = (pos // GRID_W)[:, None] * inv[None, :]
    ang_c = (pos % GRID_W)[:, None] * inv[None, :]
    cos = np.concatenate([np.cos(ang_r)] * 2 + [np.cos(ang_c)] * 2, axis=1)
    sin = np.concatenate([-np.sin(ang_r), np.sin(ang_r), -np.sin(ang_c), np.sin(ang_c)], axis=1)
    return jnp.asarray(cos, F32), jnp.asarray(sin, F32)


def _attn_pre(z, qnorm, knorm, offs, row0, rows, seq, tm, latent):
    dh = qnorm.shape[-1]
    bw, kvw2 = offs['q_w'], offs['kv_w']
    r0 = row0 // tm
    per_seq = seq // tm
    cos, sin = _rope_tables(seq, dh) if latent else (jnp.zeros((tm, dh), F32), jnp.zeros((tm, dh), F32))
    tab = pl.BlockSpec((tm, dh), (lambda i: (i % per_seq, 0)) if latent else (lambda i: (0, 0)))
    vec = pl.BlockSpec((1, dh), lambda i: (0, 0))
    kvo = pl.BlockSpec((tm, kvw2 // 2), lambda i: (i, 0))
    return pl.pallas_call(
        functools.partial(_attn_pre_kernel, dh=dh, latent=latent),
        grid=(rows // tm,),
        in_specs=[pl.BlockSpec((tm, bw), lambda i: (r0 + i, offs['q'] // bw)),
                  pl.BlockSpec((tm, kvw2), lambda i: (r0 + i, offs['kv'] // kvw2)),
                  vec, vec, tab, tab],
        out_specs=[pl.BlockSpec((tm, bw), lambda i: (i, 0)), kvo, kvo, kvo, kvo],
        out_shape=[jax.ShapeDtypeStruct((rows, bw), BF16), jax.ShapeDtypeStruct((rows, kvw2 // 2), BF16),
                   jax.ShapeDtypeStruct((rows, kvw2 // 2), BF16), jax.ShapeDtypeStruct((rows, kvw2 // 2), F32),
                   jax.ShapeDtypeStruct((rows, kvw2 // 2), F32)],
        compiler_params=_cparams(("parallel",)),
        name="attention_pre",
    )(z, z, qnorm.reshape(1, dh), knorm.reshape(1, dh), cos, sin)


HY_BLK = 256
HY_TILE_ELEMS = 128 * 1024


def _dft_consts():
    n = 2 * HY_BLK
    k = np.arange(HY_BLK, dtype=np.float64)[:, None] + 0.5
    s = np.arange(HY_BLK, dtype=np.float64)[None, :]
    th = 2.0 * np.pi * k * s / n
    fwd = np.concatenate([np.cos(th), -np.sin(th)], axis=0)
    tau = np.arange(n, dtype=np.float64)[:, None]
    ph = 2.0 * np.pi * tau * (np.arange(HY_BLK, dtype=np.float64)[None, :] + 0.5) / n
    inv = np.concatenate([np.cos(ph), -np.sin(ph)], axis=1) * (2.0 / n)
    inv_cat = np.concatenate([inv[:HY_BLK], inv[HY_BLK:]], axis=1)

    def hl(a):
        a32 = jnp.asarray(a, F32)
        hi = a32.astype(BF16)
        return hi, (a32 - hi.astype(F32)).astype(BF16)

    return hl(fwd), hl(inv_cat)


def _lag_features(seq, emb):
    bands = (emb - 1) // 2
    t = np.linspace(0.0, 1.0, seq)
    wpos = 2.0 * np.pi * np.arange(seq) / seq
    f = np.linspace(1e-4, bands - 1, bands)
    z = np.concatenate([t[:, None], np.cos(f[None, :] * wpos[:, None]), -np.sin(f[None, :] * wpos[:, None])], axis=1)
    lag = np.concatenate([np.zeros(1, np.int64), np.arange(seq - 1, 0, -1), np.arange(seq)])
    return jnp.asarray(z[lag], F32)


def _hyena_filter_kernel(z_ref, fw1_ref, fb1_ref, freq_ref, fw2_ref, fb2_ref, w3b_ref, w3f_ref, dl_ref,
                         fh_ref, fl_ref, g_ref, hdn_scr, f_scr, *, seq):
    @pl.when((pl.program_id(0) == 0) & (pl.program_id(1) == 0))
    def _():
        h1 = jnp.sin(freq_ref[...] * (_mm(z_ref[...].astype(BF16), fw1_ref[...]) + fb1_ref[...]))
        hdn_scr[...] = jnp.sin(freq_ref[...] * (_mm(h1.astype(BF16), fw2_ref[...]) + fb2_ref[...])).astype(BF16)

    decay = jnp.exp(-z_ref[:, 0:1] * dl_ref[...])
    f_scr[0:seq, :] = _mm(hdn_scr[0:seq, :], w3b_ref[...]) * decay[0:seq, :]
    f_scr[seq:2 * seq, :] = _mm(hdn_scr[seq:2 * seq, :], w3f_ref[...]) * decay[seq:2 * seq, :]
    f = f_scr[...]
    scale = lax.rsqrt(jnp.sum(f * f, axis=0, keepdims=True) + 1e-12)
    rows = lax.broadcasted_iota(I32, f.shape, 0)
    f_scr[...] = jnp.where(rows == 0, 0.0, f * scale)
    for m in range(2 * seq // HY_BLK):
        hi, lo = _split(f_scr[m * HY_BLK:(m + 1) * HY_BLK, :])
        g_ref[0, m] = _mm(fh_ref[...], hi) + (_mm(fh_ref[...], lo) + _mm(fl_ref[...], hi))


def _hyena_spectra(seq, P, c_width, tc):
    emb, hid = P['c_fw1'].shape
    z = _lag_features(seq, emb)
    embp = 64
    z = jnp.pad(z, ((0, 0), (0, embp - emb)))
    fw1 = jnp.pad(P['c_fw1'], ((0, embp - emb), (0, 0))).astype(BF16)
    deltas = jnp.asarray(np.abs(np.linspace(math.log(DECAY_TARGET) / SLOW_DECAY_PCT,
                                            math.log(DECAY_TARGET) / FAST_DECAY_PCT, c_width)), F32).reshape(1, c_width)
    (fh, fl), _ = _dft_consts()
    nct = c_width // tc
    nseg = 2 * seq // HY_BLK
    full = lambda a: pl.BlockSpec(a.shape, lambda o, j: (0,) * a.ndim)
    w3 = P['c_fw3'].astype(BF16)
    consts = [z, fw1, P['c_fb1'].reshape(1, hid), P['c_freq'].reshape(1, hid), P['c_fw2'].astype(BF16),
              P['c_fb2'].reshape(1, hid)]
    return pl.pallas_call(
        functools.partial(_hyena_filter_kernel, seq=seq),
        grid=(HYENA_ORDER, nct),
        in_specs=[full(a) for a in consts] + [
            pl.BlockSpec((hid, tc), lambda o, j: (0, (o * 2 + 1) * nct + j)),
            pl.BlockSpec((hid, tc), lambda o, j: (0, (o * 2) * nct + j)),
            pl.BlockSpec((1, tc), lambda o, j: (0, j)), full(fh), full(fl)],
        out_specs=pl.BlockSpec((1, nseg, 2 * HY_BLK, tc), lambda o, j: (o, 0, 0, j)),
        out_shape=jax.ShapeDtypeStruct((HYENA_ORDER, nseg, 2 * HY_BLK, c_width), F32),
        scratch_shapes=[pltpu.VMEM((2 * seq, hid), BF16), pltpu.VMEM((2 * seq, tc), F32)],
        compiler_params=_cparams(("arbitrary", "arbitrary")),
        name="hyena_spectra",
    )(*consts, w3, w3, deltas, fh, fl)


def _hyena_conv_kernel(zin_ref, gate_ref, tz_ref, bz_ref, tg_ref, bg_ref, bias_ref, g_ref, fh_ref, ih_ref,
                       o_ref, z_scr, gate_scr, u_scr, y_scr, *, nb, conv_in):
    half = HY_BLK
    zero = jnp.zeros((1, zin_ref.shape[-1]), F32)
    z = zin_ref[0]
    if conv_in:
        z = _conv3(z, zero, zero, tz_ref[...]) + bz_ref[...]
    z_scr[...] = z
    for j in range(nb):
        u_scr[j] = _mm(fh_ref[...], z_scr[j * half:(j + 1) * half, :].astype(BF16))

    width = zin_ref.shape[-1]
    chunk = max(8, 32 * LANES // width)

    def spectra(pieces):
        for c in range(half // chunk):
            re = pl.ds(c * chunk, chunk)
            im = pl.ds(half + c * chunk, chunk)

            def add_block(j, acc):
                ur, ui = u_scr[j, re, :], u_scr[j, im, :]
                out = []
                for a, ii in enumerate(pieces):
                    m = ii - 1 - j + nb
                    gr, gi = g_ref[0, m, re, :], g_ref[0, m, im, :]
                    out.append((acc[a][0] + (ur * gr - ui * gi), acc[a][1] + (ur * gi + ui * gr)))
                return tuple(out)

            zero = jnp.zeros((chunk, width), F32)
            acc = lax.fori_loop(0, nb, add_block, tuple((zero, zero) for _ in pieces), unroll=min(4, nb))
            for a, ii in enumerate(pieces):
                y_scr[ii, re, :] = acc[a][0]
                y_scr[ii, im, :] = acc[a][1]

    def pair(p, carry):
        spectra([2 * p, 2 * p + 1])
        return carry

    lax.fori_loop(0, (nb + 1) // 2, pair, 0)
    if (nb + 1) % 2:
        spectra([nb])

    gate_scr[...] = _conv3(gate_ref[0], zero, zero, tg_ref[...]) + bg_ref[...]

    def block(i, carry):
        rows = pl.ds(pl.multiple_of(i * half, half), half)
        ycat = jnp.concatenate([y_scr[i + 1], y_scr[i]], axis=0)
        conv = _mm(ih_ref[...], ycat.astype(BF16))
        o_ref[0, rows, :] = gate_scr[rows, :] * (conv + bias_ref[...] * z_scr[rows, :])
        return carry

    lax.fori_loop(0, nb, block, 0)


def _hyena_conv(zin, zin_col0, gate_src, gate_col0, taps, tap_bias, bias, spectra, order, row0, nseq, seq, tc, conv_in):
    c_width = bias.shape[-1]
    nb = seq // HY_BLK
    nct = c_width // tc
    (fh, _), (ih, _) = _dft_consts()
    s0 = row0 // seq
    zc, gc = zin_col0 // tc, gate_col0 // tc
    tapc = (zc if conv_in else gc)
    full = lambda a: pl.BlockSpec(a.shape, lambda j, b: (0,) * a.ndim)
    zin3 = zin.reshape(-1, seq, zin.shape[-1])
    gate3 = gate_src.reshape(-1, seq, gate_src.shape[-1])
    zs0 = s0 if conv_in else 0
    return pl.pallas_call(
        functools.partial(_hyena_conv_kernel, nb=nb, conv_in=conv_in),
        grid=(nct, nseq),
        in_specs=[pl.BlockSpec((1, seq, tc), lambda j, b: (zs0 + b, 0, zc + j)),
                  pl.BlockSpec((1, seq, tc), lambda j, b: (s0 + b, 0, gc + j)),
                  pl.BlockSpec((3, tc), lambda j, b: (0, tapc + j)),
                  pl.BlockSpec((1, tc), lambda j, b: (0, tapc + j)),
                  pl.BlockSpec((3, tc), lambda j, b: (0, gc + j)),
                  pl.BlockSpec((1, tc), lambda j, b: (0, gc + j)),
                  pl.BlockSpec((1, tc), lambda j, b: (0, j)),
                  pl.BlockSpec((1, 2 * nb, 2 * HY_BLK, tc), lambda j, b: (order, 0, 0, j)),
                  full(fh), full(ih)],
        out_specs=pl.BlockSpec((1, seq, tc), lambda j, b: (b, 0, j)),
        out_shape=jax.ShapeDtypeStruct((nseq, seq, c_width), F32),
        scratch_shapes=[pltpu.VMEM((seq, tc), F32), pltpu.VMEM((seq, tc), F32),
                        pltpu.VMEM((nb, 2 * HY_BLK, tc), F32), pltpu.VMEM((nb + 1, 2 * HY_BLK, tc), F32)],
        compiler_params=_cparams(("arbitrary", "arbitrary")),
        name="hyena_conv",
    )(zin3, gate3, taps, tap_bias, taps, tap_bias, bias, spectra, fh, ih)


def _rwkv_scan_pass(r, kk, v, w2, b2, kt2, s0, row0, bn, seq):
    w_ = r.shape[-1]
    heads = w_ // A_HEAD_DIM
    n = A_HEAD_DIM
    bh = bn * heads
    kq = max(1, LANES // bh)
    nk = n // kq
    c = kq * bh
    rows = lambda x: _to_chain(x, row0, bn, seq, kq, False)
    if s0 is None:
        s0r = jnp.zeros((2, nk, n, c), F32)
    else:
        s0r = jnp.transpose(s0.reshape(bn, 2, heads, n, kq, nk), (1, 5, 3, 4, 0, 2)).reshape(2, nk, n, c)
    yf, yb, sfin = _scan(rows(r), rows(kk), _to_chain(v, row0, bn, seq, kq, True),
                         (rows(w2[0]), rows(w2[1])), (rows(b2[0]), rows(b2[1])), (rows(kt2[0]), rows(kt2[1])),
                         s0r, kq)
    sfin = jnp.transpose(sfin.reshape(2, nk, n, kq, bn, heads), (4, 0, 5, 2, 3, 1)).reshape(bn, 2, heads, n, n)
    return _from_chain(yf, bn, heads, kq), _from_chain(yb, bn, heads, kq), sfin


def kernel(x_prompt, x_sample, cache_b_k, cache_b_v, state_a, c, c_ctx, mod_w, mod_b, norm1, norm2,
           even_w_in, even_a_conv, even_a_w0, even_a_wu, even_a_a0, even_a_au, even_a_gu, even_a_kk,
           even_a_ka, even_a_rk, even_a_ln_w, even_a_ln_b, even_b_qnorm, even_b_knorm, even_w_out,
           odd_w_in, odd_c_conv, odd_c_conv_b, odd_c_fw1, odd_c_fb1, odd_c_freq, odd_c_fw2, odd_c_fb2,
           odd_c_fw3, odd_c_bias, odd_w_out, peer_wq, peer_keys, peer_u, peer_v):
    bp, sp, d = x_prompt.shape
    bs, ss, _ = x_sample.shape
    depth = mod_w.shape[0]
    tp, ts = bp * sp, bs * ss
    t_all = tp + ts
    a_width = even_a_w0.shape[-1]
    a_cols = even_a_conv.shape[-1]
    dh = even_b_qnorm.shape[-1]
    b_width = d // 2
    kv_width = b_width // B_GROUP
    kvh = kv_width // dh
    c_width = odd_c_bias.shape[-1]
    nkeys = peer_keys.shape[3]
    assert bs + 1 <= 8 and nkeys == LANES and peer_keys.shape[1] * PEER_TOPK == LANES

    tm = _row_tile(tp, ss, cap=512)
    seg = jnp.concatenate([jnp.zeros((tp // tm,), I32),
                           1 + jnp.arange(ts // tm, dtype=I32) // (ss // tm)])

    cond8 = jnp.zeros((8, d), F32).at[0].set(c_ctx).at[1:1 + bs].set(c)
    mods = _mod_table(cond8, mod_w, mod_b)

    x = jnp.concatenate([x_prompt.reshape(tp, d), x_sample.reshape(ts, d)], axis=0)
    new_k, new_v, new_s = [], [], []
    for layer in range(depth):
        j = layer // 2
        mod3 = mods[layer].reshape(8 * 6, 1, d)
        if layer % 2 == 0:
            PA = dict(a_conv=even_a_conv[j], a_w0=even_a_w0[j], a_wu=even_a_wu[j], a_a0=even_a_a0[j],
                      a_au=even_a_au[j], a_gu=even_a_gu[j], a_kk=even_a_kk[j], a_ka=even_a_ka[j],
                      a_rk=even_a_rk[j], a_ln_w=even_a_ln_w[j], a_ln_b=even_a_ln_b[j])
            w_in = even_w_in[j]
            w_perm = jnp.concatenate([w_in[:, :3 * a_width], w_in[:, a_cols:], w_in[:, 3 * a_width:a_cols]],
                                     axis=1).astype(BF16)
            offs = dict(q=3 * a_width, q_w=b_width, kv=3 * a_width + b_width, kv_w=2 * kv_width,
                        low=3 * a_width + b_width + 2 * kv_width, low_w=a_cols - 3 * a_width)
            assert a_width == b_width and offs['kv'] % offs['kv_w'] == 0 and offs['low'] % offs['low_w'] == 0
            z, _ = _nm_matmul(x, norm1[layer], mod3, 1, 0, seg, w_perm, tm)
            tmr = _row_tile(sp, ss, cap=256)
            nt = t_all // tmr
            starts = np.concatenate([np.arange(0, tp, sp), tp + np.arange(0, ts, ss), [t_all]])
            tile0 = np.arange(nt) * tmr
            keep_prev = jnp.asarray(~np.isin(tile0, starts), F32)[:, None]
            keep_next = jnp.asarray(~np.isin(tile0 + tmr, starts), F32)[:, None]
            zt = z.reshape(nt, tmr, z.shape[-1])
            zero_row = jnp.zeros((1, z.shape[-1]), F32)
            prev_rows = (jnp.concatenate([zero_row, zt[:-1, tmr - 1]], axis=0) * keep_prev)[:, None, :]
            next_rows = (jnp.concatenate([zt[1:, 0], zero_row], axis=0) * keep_next)[:, None, :]
            r, kk, vv, w0, w1, b0, b1, kt0, kt1, g, bonus = _rwkv_pre(z, prev_rows, next_rows, PA, tmr, offs)
            outs = []
            for (row0, bn, seq, latent) in ((0, bp, sp, False), (tp, bs, ss, True)):
                rows = bn * seq
                q, k, v, k_f32, v_f32 = _attn_pre(z, even_b_qnorm[j], even_b_knorm[j], offs, row0, rows, seq,
                                                  tmr, latent)
                k = k.reshape(bn, seq, kv_width)
                v = v.reshape(bn, seq, kv_width)
                if latent:
                    past = cache_b_k.shape[2]
                    k = jnp.concatenate([k, cache_b_k[:, j].astype(BF16).reshape(bn, past, kv_width)], axis=1)
                    v = jnp.concatenate([v, cache_b_v[:, j].astype(BF16).reshape(bn, past, kv_width)], axis=1)
                    s0 = state_a[:, j]
                else:
                    s0 = None
                    new_k.append(k_f32.reshape(bn, seq, kvh, dh))
                    new_v.append(v_f32.reshape(bn, seq, kvh, dh))
                y_b = _attend(q.reshape(bn, seq, b_width), k, v, dh).reshape(rows, b_width)
                yf, yb, s_fin = _rwkv_scan_pass(r, kk, vv, (w0, w1), (b0, b1), (kt0, kt1), s0, row0, bn, seq)
                if not latent:
                    new_s.append(s_fin)
                y_a = _rwkv_post(yf, yb, bonus, g, row0, PA, tmr)
                outs.append(jnp.concatenate([y_a, y_b], axis=-1))
            mix_in = jnp.concatenate(outs, axis=0)
            x = _res_matmul(mix_in, even_w_out[j].astype(BF16), x, mod3, 2, seg, tm)
        else:
            PC = dict(c_fw1=odd_c_fw1[j], c_fb1=odd_c_fb1[j], c_freq=odd_c_freq[j], c_fw2=odd_c_fw2[j],
                      c_fb2=odd_c_fb2[j], c_fw3=odd_c_fw3[j])
            u_pre, _ = _nm_matmul(x, norm1[layer], mod3, 1, 0, seg, odd_w_in[j].astype(BF16), tm)
            taps = odd_c_conv[j]
            tap_bias = odd_c_conv_b[j].reshape(1, 3 * c_width)
            zs = []
            for (row0, bn, seq) in ((0, bp, sp), (tp, bs, ss)):
                tc = _col_tile(c_width, max(LANES, HY_TILE_ELEMS // seq))
                spectra = _hyena_spectra(seq, PC, c_width, tc)
                z1 = _hyena_conv(u_pre, 2 * c_width, u_pre, 0, taps, tap_bias, odd_c_bias[j, 0:1], spectra, 0,
                                 row0, bn, seq, tc, True)
                z2 = _hyena_conv(z1.reshape(bn * seq, c_width), 0, u_pre, c_width, taps, tap_bias,
                                 odd_c_bias[j, 1:2], spectra, 1, row0, bn, seq, tc, False)
                zs.append(z2.reshape(bn * seq, c_width))
            x = _res_matmul(jnp.concatenate(zs, axis=0).astype(BF16), odd_w_out[j].astype(BF16), x, mod3, 2, seg, tm)
        q, hm = _nm_matmul(x, norm2[layer], mod3, 4, 3, seg, peer_wq[layer].astype(BF16), tm)
        i1, i2, gate = _peer_route(q, peer_keys[layer])
        gmat = _gate_matrix(i1, i2, gate, nkeys)
        u_bf16, v_bf16 = _to_bf16(peer_u, layer), _to_bf16(peer_v, layer)
        if layer < depth - 1:
            x = _peer_experts(hm, gmat, u_bf16, v_bf16, x, mod3, 5, seg, tm)
        else:
            y_prompt = _peer_experts(hm, gmat, u_bf16, v_bf16, x, mod3, 5, seg, tm, 0, tp).reshape(bp, sp, d)
            y_sample = _peer_experts(hm, gmat, u_bf16, v_bf16, x, mod3, 5, seg, tm, tp, ts).reshape(bs, ss, d)
    return (y_prompt, y_sample, jnp.stack(new_k, axis=1), jnp.stack(new_v, axis=1), jnp.stack(new_s, axis=1))
```

```python
import functools
import math

import numpy as np
import jax
import jax.numpy as jnp
from jax import lax
from jax.experimental import pallas as pl
from jax.experimental.pallas import tpu as pltpu

F32 = jnp.float32
BF16 = jnp.bfloat16
I32 = jnp.int32

NORM_EPS = 1e-6
A_HEAD_DIM = 64
A_GN_EPS = 64e-5
B_GROUP = 4
GRID_W = 64
ROPE_THETA = 10000.0
HYENA_ORDER = 2
DECAY_TARGET = 1e-2
FAST_DECAY_PCT = 0.3
SLOW_DECAY_PCT = 1.5
PEER_TOPK = 16
LANES = 128
VMEM_LIMIT = 56 * 1024 * 1024


def _cparams(sem):
    return pltpu.CompilerParams(dimension_semantics=sem, vmem_limit_bytes=VMEM_LIMIT)


def _row_tile(*lengths, cap=512):
    t = cap
    while any(n % t for n in lengths):
        t //= 2
    return t


def _col_tile(n, cap):
    return max(t for t in range(LANES, cap + 1, LANES) if n % t == 0)


def _mod_kernel(c_ref, w_ref, b_ref, o_ref):
    c = c_ref[...]
    s = (c * jax.nn.sigmoid(c)).astype(BF16)
    o_ref[0] = jnp.dot(s, w_ref[0].astype(BF16), preferred_element_type=F32) + b_ref[0]


def _mod_table(cond8, mod_w, mod_b):
    depth, d, n = mod_w.shape
    tn = _row_tile(n, cap=1024)
    return pl.pallas_call(
        _mod_kernel,
        grid=(depth, n // tn),
        in_specs=[pl.BlockSpec((8, d), lambda l, j: (0, 0)),
                  pl.BlockSpec((1, d, tn), lambda l, j: (l, 0, j)),
                  pl.BlockSpec((1, 1, tn), lambda l, j: (l, 0, j))],
        out_specs=pl.BlockSpec((1, 8, tn), lambda l, j: (l, 0, j)),
        out_shape=jax.ShapeDtypeStruct((depth, 8, n), F32),
        compiler_params=_cparams(("parallel", "parallel")),
        name="mod_table",
    )(cond8, mod_w, mod_b.reshape(depth, 1, n))


def _nm_matmul_kernel(seg_ref, x_ref, g_ref, sc_ref, sh_ref, w_ref, o_ref, h_ref, h_scr, *, one_col_tile):
    del seg_ref

    def modulated():
        x = x_ref[...]
        y = x * lax.rsqrt(jnp.mean(x * x, axis=-1, keepdims=True) + NORM_EPS) * g_ref[...]
        return (y * (1.0 + sc_ref[0]) + sh_ref[0]).astype(BF16)

    if one_col_tile:
        h = modulated()
        h_ref[...] = h
        o_ref[...] = jnp.dot(h, w_ref[...], preferred_element_type=F32).astype(o_ref.dtype)
        return

    @pl.when(pl.program_id(1) == 0)
    def _():
        h = modulated()
        h_scr[...] = h
        h_ref[...] = h

    o_ref[...] = jnp.dot(h_scr[...], w_ref[...], preferred_element_type=F32).astype(o_ref.dtype)


W_RESIDENT_BYTES = 8 * 1024 * 1024


def _nm_matmul(x, g, mod3, sc_idx, sh_idx, segs, w_bf16, out_dtype=F32):
    t, d = x.shape
    n = w_bf16.shape[1]
    tm = min(segs)
    if d * n * 2 <= W_RESIDENT_BYTES:
        tn = n
    elif n % 1024 == 0 and len(segs) > 1:
        tm, tn = max(segs), 1024
    else:
        tn = _col_tile(n, 1664)
    seg = segs[tm]
    grid_spec = pltpu.PrefetchScalarGridSpec(
        num_scalar_prefetch=1,
        grid=(t // tm, n // tn),
        in_specs=[pl.BlockSpec((tm, d), lambda i, j, s: (i, 0)),
                  pl.BlockSpec((1, d), lambda i, j, s: (0, 0)),
                  pl.BlockSpec((1, 1, d), lambda i, j, s: (s[i] * 6 + sc_idx, 0, 0)),
                  pl.BlockSpec((1, 1, d), lambda i, j, s: (s[i] * 6 + sh_idx, 0, 0)),
                  pl.BlockSpec((d, tn), lambda i, j, s: (0, j))],
        out_specs=[pl.BlockSpec((tm, tn), lambda i, j, s: (i, j)),
                   pl.BlockSpec((tm, d), lambda i, j, s: (i, 0))],
        scratch_shapes=[pltpu.VMEM((tm, d), BF16)])
    return pl.pallas_call(
        functools.partial(_nm_matmul_kernel, one_col_tile=(tn == n)),
        grid_spec=grid_spec,
        out_shape=[jax.ShapeDtypeStruct((t, n), out_dtype), jax.ShapeDtypeStruct((t, d), BF16)],
        compiler_params=_cparams(("parallel", "arbitrary")),
        name="norm_mod_matmul",
    )(seg, x, g.reshape(1, d), mod3, mod3, w_bf16)


def _res_matmul_kernel(seg_ref, a_ref, w_ref, r_ref, gt_ref, o_ref):
    del seg_ref
    mm = jnp.dot(a_ref[...], w_ref[...], preferred_element_type=F32)
    o_ref[...] = r_ref[...] + gt_ref[0] * mm


def _res_matmul(a_bf16, w_bf16, res, mod3, gt_idx, seg, tm):
    t, k = a_bf16.shape
    n = w_bf16.shape[1]
    tn = n if k * n * 2 <= W_RESIDENT_BYTES else _col_tile(n, 1024)
    grid_spec = pltpu.PrefetchScalarGridSpec(
        num_scalar_prefetch=1,
        grid=(t // tm, n // tn),
        in_specs=[pl.BlockSpec((tm, k), lambda i, j, s: (i, 0)),
                  pl.BlockSpec((k, tn), lambda i, j, s: (0, j)),
                  pl.BlockSpec((tm, tn), lambda i, j, s: (i, j)),
                  pl.BlockSpec((1, 1, tn), lambda i, j, s: (s[i] * 6 + gt_idx, 0, j))],
        out_specs=pl.BlockSpec((tm, tn), lambda i, j, s: (i, j)))
    return pl.pallas_call(
        _res_matmul_kernel,
        grid_spec=grid_spec,
        out_shape=jax.ShapeDtypeStruct((t, n), F32),
        compiler_params=_cparams(("parallel", "parallel")),
        name="res_matmul",
    )(seg, a_bf16, w_bf16, res, mod3)


def _scan_kernel(rf_ref, kkf_ref, vf_ref, wf_ref, bf_ref, ktf_ref, rb_ref, kkb_ref, vb_ref, wb_ref, bb_ref,
                 ktb_ref, s0_ref, yf_ref, yb_ref, sfin_ref, s_scr, *, tb_steps, nk, kq):
    ti = pl.program_id(1)

    @pl.when(ti == 0)
    def _():
        s_scr[...] = s0_ref[...]

    def tree(parts):
        while len(parts) > 1:
            parts = [parts[i] + parts[i + 1] for i in range(0, len(parts), 2)]
        return parts[0]

    def all_parts(p):
        part = LANES // kq
        return tree([p] + [pltpu.roll(p, i * part, 1) for i in range(1, kq)])

    nacc = 4

    def advance(d, t, r_ref, kk_ref, v_ref, w_ref, b_ref, kt_ref, y_ref):
        row = pl.ds(t, 1)
        accs = [None] * nacc
        for k in range(nk):
            p = s_scr[d, k] * kk_ref[k, row, :]
            accs[k % nacc] = p if accs[k % nacc] is None else accs[k % nacc] + p
        sa = all_parts(tree(accs))
        v = v_ref[t]
        yacc = [None] * nacc
        for k in range(nk):
            s_new = s_scr[d, k] * w_ref[k, row, :] - sa * b_ref[k, row, :] + v * kt_ref[k, row, :]
            s_scr[d, k] = s_new
            p = s_new * r_ref[k, row, :]
            yacc[k % nacc] = p if yacc[k % nacc] is None else yacc[k % nacc] + p
        y_ref[t] = all_parts(tree(yacc))

    def step(s, carry):
        advance(0, s, rf_ref, kkf_ref, vf_ref, wf_ref, bf_ref, ktf_ref, yf_ref)
        advance(1, tb_steps - 1 - s, rb_ref, kkb_ref, vb_ref, wb_ref, bb_ref, ktb_ref, yb_ref)
        return carry

    lax.fori_loop(0, tb_steps, step, 0, unroll=2)

    @pl.when(ti == pl.num_programs(1) - 1)
    def _():
        sfin_ref[...] = s_scr[...]


def _scan(r, kk, v, w2, b2, kt2, s0, kq):
    nk, seq, c = r.shape
    nv = v.shape[1]
    tb_steps = _row_tile(seq, cap=16)
    nt = seq // tb_steps
    rowf = pl.BlockSpec((nk, tb_steps, LANES), lambda gi, ti: (0, ti, gi))
    rowb = pl.BlockSpec((nk, tb_steps, LANES), lambda gi, ti: (0, nt - 1 - ti, gi))
    valf = pl.BlockSpec((tb_steps, nv, LANES), lambda gi, ti: (ti, 0, gi))
    valb = pl.BlockSpec((tb_steps, nv, LANES), lambda gi, ti: (nt - 1 - ti, 0, gi))
    st = pl.BlockSpec((2, nk, nv, LANES), lambda gi, ti: (0, 0, 0, gi))
    return pl.pallas_call(
        functools.partial(_scan_kernel, tb_steps=tb_steps, nk=nk, kq=kq),
        grid=(c // LANES, nt),
        in_specs=[rowf, rowf, valf, rowf, rowf, rowf, rowb, rowb, valb, rowb, rowb, rowb, st],
        out_specs=[valf, valb, st],
        out_shape=[jax.ShapeDtypeStruct((seq, nv, c), F32), jax.ShapeDtypeStruct((seq, nv, c), F32),
                   jax.ShapeDtypeStruct((2, nk, nv, c), F32)],
        scratch_shapes=[pltpu.VMEM((2, nk, nv, LANES), F32)],
        compiler_params=_cparams(("parallel", "arbitrary")),
        name="rwkv7_scan",
    )(r, kk, v, w2[0], b2[0], kt2[0], r, kk, v, w2[1], b2[1], kt2[1], s0)


CHAIN_TB = 128


def _to_chain_kernel(x_ref, o_ref, a_scr, *stage, heads, nk, kq, values):
    bg, tb = x_ref.shape[0], x_ref.shape[1]
    n = A_HEAD_DIM
    for b in range(bg):
        xt = x_ref[b].T
        a_scr[b] = jnp.swapaxes(xt.reshape(heads, n, tb), 0, 1)
    for p in range(n if values else nk):
        pieces = [a_scr[b, p if values else q * nk + p] for q in range(kq) for b in range(bg)]
        tile = jnp.concatenate(pieces, axis=0).T
        if values:
            stage[0][p] = tile
        else:
            o_ref[p] = tile
    if values:
        o_ref[...] = jnp.swapaxes(stage[0][...], 0, 1)


def _to_chain(x, row0, bn, seq, kq, values):
    w_ = x.shape[-1]
    heads = w_ // A_HEAD_DIM
    n = A_HEAD_DIM
    nk = n // kq
    bg = LANES // (kq * heads)
    ng = bn // bg
    c = kq * bn * heads
    assert kq * bg * heads == LANES and c == ng * LANES and row0 % (bg * seq) == 0
    tb = min(CHAIN_TB, seq)
    g0 = row0 // (bg * seq)
    scratch = [pltpu.VMEM((bg, n, heads, tb), F32)]
    if values:
        out_spec = pl.BlockSpec((tb, n, LANES), lambda g, t: (t, 0, g))
        out_shape = jax.ShapeDtypeStruct((seq, n, c), F32)
        scratch.append(pltpu.VMEM((n, tb, LANES), F32))
    else:
        out_spec = pl.BlockSpec((nk, tb, LANES), lambda g, t: (0, t, g))
        out_shape = jax.ShapeDtypeStruct((nk, seq, c), F32)
    return pl.pallas_call(
        functools.partial(_to_chain_kernel, heads=heads, nk=nk, kq=kq, values=values),
        grid=(ng, seq // tb),
        in_specs=[pl.BlockSpec((bg, tb, w_), lambda g, t: (g0 + g, t, 0))],
        out_specs=out_spec,
        out_shape=out_shape,
        scratch_shapes=scratch,
        compiler_params=_cparams(("parallel", "parallel")),
        name="to_chain_layout",
    )(x.reshape(-1, seq, w_))


def _from_chain_kernel(y_ref, o_ref, stage, a_scr, *, heads):
    bg = o_ref.shape[0]
    n = A_HEAD_DIM
    tb = y_ref.shape[0]
    stage[...] = jnp.swapaxes(y_ref[...], 0, 1)
    for v in range(n):
        rows = stage[v].T
        for b in range(bg):
            a_scr[b, v] = rows[b * heads:(b + 1) * heads]
    for b in range(bg):
        o_ref[b] = jnp.swapaxes(a_scr[b], 0, 1).reshape(heads * n, tb).T


def _from_chain(y, bn, heads, kq):
    seq, n, c = y.shape
    bg = LANES // (kq * heads)
    tb = min(CHAIN_TB, seq)
    out = pl.pallas_call(
        functools.partial(_from_chain_kernel, heads=heads),
        grid=(c // LANES, seq // tb),
        in_specs=[pl.BlockSpec((tb, n, LANES), lambda g, t: (t, 0, g))],
        out_specs=pl.BlockSpec((bg, tb, heads * n), lambda g, t: (g, t, 0)),
        out_shape=jax.ShapeDtypeStruct((bn, seq, heads * n), F32),
        scratch_shapes=[pltpu.VMEM((n, tb, LANES), F32), pltpu.VMEM((bg, n, heads, tb), F32)],
        compiler_params=_cparams(("parallel", "parallel")),
        name="from_chain_layout",
    )(y)
    return out.reshape(bn * seq, heads * n)


def _attn_kernel(q_ref, k_ref, v_ref, o_ref, *, dh):
    k = k_ref[0]
    v = v_ref[0]
    for g in range(B_GROUP):
        q = q_ref[0, :, g * dh:(g + 1) * dh]
        s = lax.dot_general(q, k, (((1,), (1,)), ((), ())), preferred_element_type=F32)
        m = jnp.max(s, axis=-1, keepdims=True)
        p = jnp.exp(s - m)
        den = jnp.sum(p, axis=-1, keepdims=True)
        o = jnp.dot(p.astype(BF16), v, preferred_element_type=F32) / den
        o_ref[0, :, g * dh:(g + 1) * dh] = o.astype(o_ref.dtype)


def _attend(q, k, v, dh):
    bn, lq, qw = q.shape
    lk = k.shape[1]
    kvh = k.shape[2] // dh
    gw = B_GROUP * dh
    tq = _row_tile(lq, cap=256)
    return pl.pallas_call(
        functools.partial(_attn_kernel, dh=dh),
        grid=(bn, kvh, lq // tq),
        in_specs=[pl.BlockSpec((1, tq, gw), lambda b, h, i: (b, i, h)),
                  pl.BlockSpec((1, lk, dh), lambda b, h, i: (b, 0, h)),
                  pl.BlockSpec((1, lk, dh), lambda b, h, i: (b, 0, h))],
        out_specs=pl.BlockSpec((1, tq, gw), lambda b, h, i: (b, i, h)),
        out_shape=jax.ShapeDtypeStruct((bn, lq, qw), BF16),
        compiler_params=_cparams(("parallel", "parallel", "parallel")),
        name="attention",
    )(q, k, v)


G_GROUP = 16


def _gate_matrix_kernel(i1_ref, i2_ref, g_ref, o_ref, gtmp_scr, *, nkeys):
    iota = lax.broadcasted_iota(I32, (nkeys, nkeys), 0)

    def build(grp, carry):
        base = pl.multiple_of(grp * G_GROUP, G_GROUP)
        def token(tt, c2):
            t = base + tt
            a_t = jnp.where(iota == i1_ref[pl.ds(t, 1), :], 1.0, 0.0).astype(BF16)
            b_t = jnp.where(iota == i2_ref[pl.ds(t, 1), :], g_ref[pl.ds(t, 1), :], 0.0).astype(BF16)
            gtmp_scr[tt] = lax.dot_general(a_t, b_t, (((1,), (1,)), ((), ())), preferred_element_type=F32)
            return c2

        lax.fori_loop(0, G_GROUP, token, 0, unroll=G_GROUP)
        by_n1 = jnp.swapaxes(gtmp_scr[...], 0, 1).astype(BF16)
        for n1 in range(nkeys):
            o_ref[pl.ds(base, G_GROUP), n1 * nkeys:(n1 + 1) * nkeys] = by_n1[n1]
        return carry

    lax.fori_loop(0, o_ref.shape[0] // G_GROUP, build, 0)


def _gate_matrix(i1, i2, gate, nkeys):
    t, nj = i1.shape
    tb = LANES
    sel = pl.BlockSpec((tb, nj), lambda i: (i, 0))
    return pl.pallas_call(
        functools.partial(_gate_matrix_kernel, nkeys=nkeys),
        grid=(t // tb,),
        in_specs=[sel, sel, sel],
        out_specs=pl.BlockSpec((tb, nkeys * nkeys), lambda i: (i, 0)),
        out_shape=jax.ShapeDtypeStruct((t, nkeys * nkeys), BF16),
        scratch_shapes=[pltpu.VMEM((G_GROUP, nkeys, nkeys), F32)],
        compiler_params=_cparams(("parallel",)),
        name="peer_gate_matrix",
    )(i1, i2, gate)


def _to_bf16_kernel(x_ref, o_ref):
    o_ref[...] = x_ref[0].astype(BF16)


def _to_bf16(tables, layer):
    _, e, d = tables.shape
    te = _row_tile(e, cap=1024)
    return pl.pallas_call(
        _to_bf16_kernel,
        grid=(e // te,),
        in_specs=[pl.BlockSpec((1, te, d), lambda i: (layer, i, 0))],
        out_specs=pl.BlockSpec((te, d), lambda i: (i, 0)),
        out_shape=jax.ShapeDtypeStruct((e, d), BF16),
        compiler_params=_cparams(("parallel",)),
        name="to_bf16",
    )(tables)


def _peer_kernel(seg_ref, xb_ref, gm_ref, u_ref, v_ref, r_ref, gt_ref, o_ref, acc_scr):
    del seg_ref
    e = pl.program_id(1)

    @pl.when(e == 0)
    def _():
        acc_scr[...] = jnp.zeros_like(acc_scr)

    h = lax.dot_general(xb_ref[...], u_ref[...], (((1,), (1,)), ((), ())), preferred_element_type=F32)
    act = 0.5 * h * (1.0 + lax.erf(h * (1.0 / math.sqrt(2.0)))) * gm_ref[...].astype(F32)
    acc_scr[...] += jnp.dot(act.astype(BF16), v_ref[...], preferred_element_type=F32)

    @pl.when(e == pl.num_programs(1) - 1)
    def _():
        o_ref[...] = r_ref[...] + gt_ref[0] * acc_scr[...]


def _peer_experts(xb, gmat, u_bf16, v_bf16, res, mod3, gt_idx, seg, tm, row0=0, rows=None):
    t, d = xb.shape
    rows = t if rows is None else rows
    r0 = row0 // tm
    ne = u_bf16.shape[0]
    te = 1024
    grid_spec = pltpu.PrefetchScalarGridSpec(
        num_scalar_prefetch=1,
        grid=(rows // tm, ne // te),
        in_specs=[pl.BlockSpec((tm, d), lambda i, e, s: (r0 + i, 0)),
                  pl.BlockSpec((tm, te), lambda i, e, s: (r0 + i, e)),
                  pl.BlockSpec((te, d), lambda i, e, s: (e, 0)),
                  pl.BlockSpec((te, d), lambda i, e, s: (e, 0)),
                  pl.BlockSpec((tm, d), lambda i, e, s: (r0 + i, 0)),
                  pl.BlockSpec((1, 1, d), lambda i, e, s: (s[r0 + i] * 6 + gt_idx, 0, 0))],
        out_specs=pl.BlockSpec((tm, d), lambda i, e, s: (i, 0)),
        scratch_shapes=[pltpu.VMEM((tm, d), F32)])
    return pl.pallas_call(
        _peer_kernel,
        grid_spec=grid_spec,
        out_shape=jax.ShapeDtypeStruct((rows, d), F32),
        compiler_params=_cparams(("parallel", "arbitrary")),
        name="peer_experts",
    )(seg, xb, gmat, u_bf16, v_bf16, res, mod3)


def _topk_rows(vals, payload, rows_out):
    big = jnp.float32(2 ** 30)
    top_v = jnp.zeros(rows_out.shape, F32)
    top_p = jnp.zeros(rows_out.shape, F32)
    for it in range(PEER_TOPK):
        m = jnp.max(vals, axis=0, keepdims=True)
        sel = jnp.min(jnp.where(vals == m, payload, big), axis=0, keepdims=True)
        top_v = jnp.where(rows_out == it, m, top_v)
        top_p = jnp.where(rows_out == it, sel, top_p)
        vals = jnp.where(payload == sel, -jnp.inf, vals)
    return top_v, top_p


def _top_pair_sums(s1, s2, rows_out):
    k = PEER_TOPK
    half = k // 2
    tt = s1.shape[1]
    big = jnp.float32(2 ** 30)
    lists = [s1[:half] + s2[j:j + 1] for j in range(k)]
    singles = s1[half:] + s2[0:1]
    sub = lax.broadcasted_iota(I32, (half, tt), 0).astype(F32)
    head_id = sub * k
    single_id = (sub + half) * k
    top_v = jnp.zeros(rows_out.shape, F32)
    top_p = jnp.zeros(rows_out.shape, F32)
    for it in range(k):
        m = jnp.max(jnp.maximum(lists[0], singles), axis=0, keepdims=True)
        sel = jnp.min(jnp.minimum(jnp.where(lists[0] == m, head_id, big), jnp.where(singles == m, single_id, big)),
                      axis=0, keepdims=True)
        top_v = jnp.where(rows_out == it, m, top_v)
        top_p = jnp.where(rows_out == it, sel, top_p)
        pop = head_id == sel
        lists = [jnp.where(pop, lists[j + 1], lists[j]) for j in range(k - 1)] + [jnp.where(pop, -jnp.inf, lists[-1])]
        head_id = jnp.where(pop, head_id + 1.0, head_id)
        singles = jnp.where(single_id == sel, -jnp.inf, singles)
    return top_v, top_p


def _gather_rows(table, sel):
    out = jnp.zeros(sel.shape, table.dtype)
    for i in range(PEER_TOPK):
        out = jnp.where(sel == i, table[i:i + 1, :], out)
    return out


def _route_kernel(q_ref, keys_ref, i1_ref, i2_ref, g_ref, n1_scr, n2_scr, gate_scr, *, heads, nkeys, dq):
    tt = q_ref.shape[0]
    k = PEER_TOPK
    n_iota = lax.broadcasted_iota(I32, (nkeys, tt), 0).astype(F32)
    rows_out = lax.broadcasted_iota(I32, (k, tt), 0)

    def head(h, carry):
        tops = []
        for c in range(2):
            col = pl.multiple_of((h * 2 + c) * dq, dq)
            qhc = q_ref[:, pl.ds(col, dq)].astype(BF16)
            khc = keys_ref[h, c].astype(BF16)
            s = lax.dot_general(khc, qhc, (((1,), (1,)), ((), ())), preferred_element_type=F32)
            tops.append(_topk_rows(s, n_iota, rows_out))
        (s1, i1), (s2, i2) = tops
        top, ci = _top_pair_sums(s1, s2, rows_out)
        ci = ci.astype(I32)
        n1 = _gather_rows(i1, lax.shift_right_logical(ci, 4)).astype(I32)
        n2 = _gather_rows(i2, lax.bitwise_and(ci, k - 1)).astype(I32)
        ex = jnp.exp(top - top[0:1, :])
        gate = ex / jnp.sum(ex, axis=0, keepdims=True)
        row = pl.multiple_of(h * k, k)
        n1_scr[pl.ds(row, k), :] = n1
        n2_scr[pl.ds(row, k), :] = n2
        gate_scr[pl.ds(row, k), :] = gate
        return carry

    lax.fori_loop(0, heads, head, 0)
    i1_ref[...] = n1_scr[...].T
    i2_ref[...] = n2_scr[...].T
    g_ref[...] = gate_scr[...].T


def _peer_route(q, sub_keys):
    t, qw = q.shape
    heads, _, nkeys, dq = sub_keys.shape
    hk = heads * PEER_TOPK
    tt = LANES
    out = pl.BlockSpec((tt, hk), lambda i: (i, 0))
    return pl.pallas_call(
        functools.partial(_route_kernel, heads=heads, nkeys=nkeys, dq=dq),
        grid=(t // tt,),
        in_specs=[pl.BlockSpec((tt, qw), lambda i: (i, 0)),
                  pl.BlockSpec((heads, 2, nkeys, dq), lambda i: (0, 0, 0, 0))],
        out_specs=[out, out, out],
        out_shape=[jax.ShapeDtypeStruct((t, hk), I32), jax.ShapeDtypeStruct((t, hk), I32),
                   jax.ShapeDtypeStruct((t, hk), F32)],
        scratch_shapes=[pltpu.VMEM((hk, tt), I32), pltpu.VMEM((hk, tt), I32), pltpu.VMEM((hk, tt), F32)],
        compiler_params=_cparams(("parallel",)),
        name="peer_route",
    )(q, sub_keys)


def _split(x):
    hi = x.astype(BF16)
    return hi, (x - hi.astype(F32)).astype(BF16)


def _mm(a, b):
    return jnp.dot(a, b, preferred_element_type=F32)


def _head_sum(x, ones_blockdiag):
    hi, lo = _split(x)
    return _mm(hi, ones_blockdiag) + _mm(lo, ones_blockdiag)


def _shift_rows(x, first_row, last_row):
    n = x.shape[0]
    rows = lax.broadcasted_iota(I32, x.shape, 0)
    prev = jnp.where(rows == 0, first_row, pltpu.roll(x, 1, 0))
    nxt = jnp.where(rows == n - 1, last_row, pltpu.roll(x, n - 1, 0))
    return prev, nxt


def _conv3(x, first_row, last_row, taps):
    prev, nxt = _shift_rows(x, first_row, last_row)
    return prev * taps[0:1] + x * taps[1:2] + nxt * taps[2:3]


def _rwkv_pre_kernel(r_ref, k_ref, v_ref, low_ref, pr_ref, pk_ref, pv_ref, plow_ref, nr_ref, nk_ref, nv_ref,
                     nlow_ref, cr_ref, ck_ref, cv_ref, clow_ref, w0_ref, wu_ref, a0_ref, au_ref, gu_ref,
                     kkg_ref, ka_ref, rk_ref, ones_ref,
                     ro_ref, kko_ref, vo_ref, w0o_ref, w1o_ref, b0o_ref, b1o_ref, kt0o_ref, kt1o_ref,
                     go_ref, bonus_ref, *, rw, ra):
    r = _conv3(r_ref[...], pr_ref[0], nr_ref[0], cr_ref[...])
    k = _conv3(k_ref[...], pk_ref[0], nk_ref[0], ck_ref[...])
    v = _conv3(v_ref[...], pv_ref[0], nv_ref[0], cv_ref[...])
    low = _conv3(low_ref[...], plow_ref[0], nlow_ref[0], clow_ref[...])
    ones = ones_ref[...]
    kk = k * kkg_ref[...]
    kk = kk * lax.rsqrt(_head_sum(kk * kk, ones) + 1e-12)
    ro_ref[...] = r
    kko_ref[...] = kk
    vo_ref[...] = v
    bonus_ref[...] = _head_sum(r * k * rk_ref[...], ones) * v
    gd = low[:, 2 * rw + 2 * ra:]
    go_ref[...] = _mm(jax.nn.sigmoid(gd).astype(BF16), gu_ref[...])
    for d_, (wo, bo, kto) in enumerate(((w0o_ref, b0o_ref, kt0o_ref), (w1o_ref, b1o_ref, kt1o_ref))):
        wd = low[:, d_ * rw:(d_ + 1) * rw]
        ad = low[:, 2 * rw + d_ * ra:2 * rw + (d_ + 1) * ra]
        lw = w0_ref[d_:d_ + 1, :] + _mm(jnp.tanh(wd).astype(BF16), wu_ref[d_])
        softplus = jnp.maximum(-lw, 0.0) + jnp.log1p(jnp.exp(-jnp.abs(lw)))
        wo[...] = jnp.exp(-jnp.exp(-softplus - 0.5))
        a = jax.nn.sigmoid(a0_ref[d_:d_ + 1, :] + _mm(ad.astype(BF16), au_ref[d_]))
        bo[...] = kk * a
        kto[...] = k * (1.0 + (a - 1.0) * ka_ref[...])


def _rwkv_pre(z, prev_rows, next_rows, P, tm, offs):
    t = z.shape[0]
    w_ = P['a_w0'].shape[-1]
    rw, ra = P['a_wu'].shape[1], P['a_au'].shape[1]
    lw = offs['low_w']
    li = offs['low'] // lw
    row = lambda c, wd: pl.BlockSpec((tm, wd), lambda i: (i, c))
    edge = lambda c, wd: pl.BlockSpec((1, 1, wd), lambda i: (i, 0, c))
    full = lambda a: pl.BlockSpec(a.shape, lambda i: (0,) * a.ndim)
    conv = P['a_conv']
    consts = [conv[:, :w_], conv[:, w_:2 * w_], conv[:, 2 * w_:3 * w_], conv[:, 3 * w_:],
              P['a_w0'], P['a_wu'].astype(BF16), P['a_a0'], P['a_au'].astype(BF16), P['a_gu'].astype(BF16),
              P['a_kk'].reshape(1, w_), P['a_ka'].reshape(1, w_), P['a_rk'].reshape(1, w_),
              jnp.asarray(_blockdiag_ones(w_), BF16)]
    out = pl.BlockSpec((tm, w_), lambda i: (i, 0))
    return pl.pallas_call(
        functools.partial(_rwkv_pre_kernel, rw=rw, ra=ra),
        grid=(t // tm,),
        in_specs=[row(0, w_), row(1, w_), row(2, w_), row(li, lw),
                  edge(0, w_), edge(1, w_), edge(2, w_), edge(li, lw),
                  edge(0, w_), edge(1, w_), edge(2, w_), edge(li, lw)] + [full(a) for a in consts],
        out_specs=[out] * 11,
        out_shape=[jax.ShapeDtypeStruct((t, w_), F32)] * 11,
        compiler_params=_cparams(("parallel",)),
        name="rwkv7_pre",
    )(z, z, z, z, prev_rows, prev_rows, prev_rows, prev_rows, next_rows, next_rows, next_rows, next_rows, *consts)


def _blockdiag_ones(width):
    idx = np.arange(width) // A_HEAD_DIM
    return (idx[:, None] == idx[None, :]).astype(np.float32)


def _rwkv_post_kernel(yf_ref, yb_ref, bonus_ref, g_ref, lnw_ref, lnb_ref, ones_ref, o_ref):
    ones = ones_ref[...]
    y = yf_ref[...] + yb_ref[...]
    mu = _head_sum(y, ones) * (1.0 / A_HEAD_DIM)
    dlt = y - mu
    var = _head_sum(dlt * dlt, ones) * (1.0 / A_HEAD_DIM)
    yn = dlt * lax.rsqrt(var + A_GN_EPS) * lnw_ref[...] + lnb_ref[...]
    o_ref[...] = ((yn + bonus_ref[...]) * g_ref[...]).astype(o_ref.dtype)


def _rwkv_post(yf, yb, bonus, g, row0, P, tm):
    t, w_ = yf.shape
    r0 = row0 // tm
    row = pl.BlockSpec((tm, w_), lambda i: (i, 0))
    off = pl.BlockSpec((tm, w_), lambda i: (r0 + i, 0))
    vec = pl.BlockSpec((1, w_), lambda i: (0, 0))
    return pl.pallas_call(
        _rwkv_post_kernel,
        grid=(t // tm,),
        in_specs=[row, row, off, off, vec, vec, pl.BlockSpec((w_, w_), lambda i: (0, 0))],
        out_specs=row,
        out_shape=jax.ShapeDtypeStruct((t, w_), BF16),
        compiler_params=_cparams(("parallel",)),
        name="rwkv7_post",
    )(yf, yb, bonus, g, P['a_ln_w'].reshape(1, w_), P['a_ln_b'].reshape(1, w_),
      jnp.asarray(_blockdiag_ones(w_), BF16))


def _attn_pre_kernel(q_ref, kv_ref, qn_ref, kn_ref, cos_ref, sin_ref, qo_ref, ko_ref, vo_ref, kf_ref, vf_ref,
                     *, dh, latent):
    def rms(x, g):
        return x * lax.rsqrt(jnp.mean(x * x, axis=-1, keepdims=True) + NORM_EPS) * g

    def rope(x):
        if not latent:
            return x
        lanes = lax.broadcasted_iota(I32, x.shape, 1)
        quarter = dh // 4
        partner = jnp.where(lanes % (2 * quarter) < quarter, pltpu.roll(x, dh - quarter, 1), pltpu.roll(x, quarter, 1))
        return x * cos_ref[...] + partner * sin_ref[...]

    nq = q_ref.shape[1] // dh
    nkv = kv_ref.shape[1] // (2 * dh)
    for h in range(nq):
        q = rope(rms(q_ref[:, h * dh:(h + 1) * dh], qn_ref[...]))
        qo_ref[:, h * dh:(h + 1) * dh] = (q * dh ** -0.5).astype(BF16)
    for h in range(nkv):
        k = rms(kv_ref[:, h * dh:(h + 1) * dh], kn_ref[...])
        v = kv_ref[:, (nkv + h) * dh:(nkv + h + 1) * dh]
        kf_ref[:, h * dh:(h + 1) * dh] = k
        vf_ref[:, h * dh:(h + 1) * dh] = v
        ko_ref[:, h * dh:(h + 1) * dh] = rope(k).astype(BF16)
        vo_ref[:, h * dh:(h + 1) * dh] = v.astype(BF16)


def _rope_tables(seq, dh):
    quarter = dh // 4
    inv = ROPE_THETA ** (-np.arange(quarter, dtype=np.float64) / quarter)
    pos = np.arange(seq)
    ang_r = (pos // GRID_W)[:, None] * inv[None, :]
    ang_c = (pos % GRID_W)[:, None] * inv[None, :]
    cos = np.concatenate([np.cos(ang_r)] * 2 + [np.cos(ang_c)] * 2, axis=1)
    sin = np.concatenate([-np.sin(ang_r), np.sin(ang_r), -np.sin(ang_c), np.sin(ang_c)], axis=1)
    return jnp.asarray(cos, F32), jnp.asarray(sin, F32)


def _attn_pre(z, qnorm, knorm, offs, row0, rows, seq, tm, latent):
    dh = qnorm.shape[-1]
    bw, kvw2 = offs['q_w'], offs['kv_w']
    r0 = row0 // tm
    per_seq = seq // tm
    cos, sin = _rope_tables(seq, dh) if latent else (jnp.zeros((tm, dh), F32), jnp.zeros((tm, dh), F32))
    tab = pl.BlockSpec((tm, dh), (lambda i: (i % per_seq, 0)) if latent else (lambda i: (0, 0)))
    vec = pl.BlockSpec((1, dh), lambda i: (0, 0))
    kvo = pl.BlockSpec((tm, kvw2 // 2), lambda i: (i, 0))
    return pl.pallas_call(
        functools.partial(_attn_pre_kernel, dh=dh, latent=latent),
        grid=(rows // tm,),
        in_specs=[pl.BlockSpec((tm, bw), lambda i: (r0 + i, offs['q'] // bw)),
                  pl.BlockSpec((tm, kvw2), lambda i: (r0 + i, offs['kv'] // kvw2)),
                  vec, vec, tab, tab],
        out_specs=[pl.BlockSpec((tm, bw), lambda i: (i, 0)), kvo, kvo, kvo, kvo],
        out_shape=[jax.ShapeDtypeStruct((rows, bw), BF16), jax.ShapeDtypeStruct((rows, kvw2 // 2), BF16),
                   jax.ShapeDtypeStruct((rows, kvw2 // 2), BF16), jax.ShapeDtypeStruct((rows, kvw2 // 2), F32),
                   jax.ShapeDtypeStruct((rows, kvw2 // 2), F32)],
        compiler_params=_cparams(("parallel",)),
        name="attention_pre",
    )(z, z, qnorm.reshape(1, dh), knorm.reshape(1, dh), cos, sin)


HY_BLK = 256
HY_TILE_ELEMS = 128 * 1024


def _dft_consts():
    n = 2 * HY_BLK
    k = np.arange(HY_BLK, dtype=np.float64)[:, None] + 0.5
    s = np.arange(HY_BLK, dtype=np.float64)[None, :]
    th = 2.0 * np.pi * k * s / n
    fwd = np.concatenate([np.cos(th), -np.sin(th)], axis=0)
    tau = np.arange(n, dtype=np.float64)[:, None]
    ph = 2.0 * np.pi * tau * (np.arange(HY_BLK, dtype=np.float64)[None, :] + 0.5) / n
    inv = np.concatenate([np.cos(ph), -np.sin(ph)], axis=1) * (2.0 / n)
    inv_cat = np.concatenate([inv[:HY_BLK], inv[HY_BLK:]], axis=1)

    def hl(a):
        a32 = jnp.asarray(a, F32)
        hi = a32.astype(BF16)
        return hi, (a32 - hi.astype(F32)).astype(BF16)

    return hl(fwd), hl(inv_cat)


def _lag_features(seq, emb):
    bands = (emb - 1) // 2
    t = np.linspace(0.0, 1.0, seq)
    wpos = 2.0 * np.pi * np.arange(seq) / seq
    f = np.linspace(1e-4, bands - 1, bands)
    z = np.concatenate([t[:, None], np.cos(f[None, :] * wpos[:, None]), -np.sin(f[None, :] * wpos[:, None])], axis=1)
    lag = np.concatenate([np.zeros(1, np.int64), np.arange(seq - 1, 0, -1), np.arange(seq)])
    return jnp.asarray(z[lag], F32)


def _hyena_filter_kernel(z_ref, fw1_ref, fb1_ref, freq_ref, fw2_ref, fb2_ref, w3b_ref, w3f_ref, dl_ref,
                         fh_ref, fl_ref, g_ref, hdn_scr, f_scr, *, seq):
    @pl.when((pl.program_id(0) == 0) & (pl.program_id(1) == 0))
    def _():
        h1 = jnp.sin(freq_ref[...] * (_mm(z_ref[...].astype(BF16), fw1_ref[...]) + fb1_ref[...]))
        hdn_scr[...] = jnp.sin(freq_ref[...] * (_mm(h1.astype(BF16), fw2_ref[...]) + fb2_ref[...])).astype(BF16)

    decay = jnp.exp(-z_ref[:, 0:1] * dl_ref[...])
    f_scr[0:seq, :] = _mm(hdn_scr[0:seq, :], w3b_ref[...]) * decay[0:seq, :]
    f_scr[seq:2 * seq, :] = _mm(hdn_scr[seq:2 * seq, :], w3f_ref[...]) * decay[seq:2 * seq, :]
    f = f_scr[...]
    scale = lax.rsqrt(jnp.sum(f * f, axis=0, keepdims=True) + 1e-12)
    rows = lax.broadcasted_iota(I32, f.shape, 0)
    f_scr[...] = jnp.where(rows == 0, 0.0, f * scale)
    for m in range(2 * seq // HY_BLK):
        hi, lo = _split(f_scr[m * HY_BLK:(m + 1) * HY_BLK, :])
        g_ref[0, m] = _mm(fh_ref[...], hi) + (_mm(fh_ref[...], lo) + _mm(fl_ref[...], hi))


def _hyena_spectra(seq, P, c_width, tc):
    emb, hid = P['c_fw1'].shape
    z = _lag_features(seq, emb)
    embp = 64
    z = jnp.pad(z, ((0, 0), (0, embp - emb)))
    fw1 = jnp.pad(P['c_fw1'], ((0, embp - emb), (0, 0))).astype(BF16)
    deltas = jnp.asarray(np.abs(np.linspace(math.log(DECAY_TARGET) / SLOW_DECAY_PCT,
                                            math.log(DECAY_TARGET) / FAST_DECAY_PCT, c_width)), F32).reshape(1, c_width)
    (fh, fl), _ = _dft_consts()
    nct = c_width // tc
    nseg = 2 * seq // HY_BLK
    full = lambda a: pl.BlockSpec(a.shape, lambda o, j: (0,) * a.ndim)
    w3 = P['c_fw3'].astype(BF16)
    consts = [z, fw1, P['c_fb1'].reshape(1, hid), P['c_freq'].reshape(1, hid), P['c_fw2'].astype(BF16),
              P['c_fb2'].reshape(1, hid)]
    return pl.pallas_call(
        functools.partial(_hyena_filter_kernel, seq=seq),
        grid=(HYENA_ORDER, nct),
        in_specs=[full(a) for a in consts] + [
            pl.BlockSpec((hid, tc), lambda o, j: (0, (o * 2 + 1) * nct + j)),
            pl.BlockSpec((hid, tc), lambda o, j: (0, (o * 2) * nct + j)),
            pl.BlockSpec((1, tc), lambda o, j: (0, j)), full(fh), full(fl)],
        out_specs=pl.BlockSpec((1, nseg, 2 * HY_BLK, tc), lambda o, j: (o, 0, 0, j)),
        out_shape=jax.ShapeDtypeStruct((HYENA_ORDER, nseg, 2 * HY_BLK, c_width), F32),
        scratch_shapes=[pltpu.VMEM((2 * seq, hid), BF16), pltpu.VMEM((2 * seq, tc), F32)],
        compiler_params=_cparams(("arbitrary", "arbitrary")),
        name="hyena_spectra",
    )(*consts, w3, w3, deltas, fh, fl)


def _hyena_conv_kernel(zin_ref, gate_ref, tz_ref, bz_ref, tg_ref, bg_ref, bias_ref, g_ref, fh_ref, ih_ref,
                       o_ref, z_scr, gate_scr, u_scr, y_scr, *, nb, conv_in):
    half = HY_BLK
    zero = jnp.zeros((1, zin_ref.shape[-1]), F32)
    z = zin_ref[0]
    if conv_in:
        z = _conv3(z, zero, zero, tz_ref[...]) + bz_ref[...]
    z_scr[...] = z
    for j in range(nb):
        u_scr[j] = _mm(fh_ref[...], z_scr[j * half:(j + 1) * half, :].astype(BF16))

    width = zin_ref.shape[-1]
    chunk = max(8, 32 * LANES // width)

    def spectra(pieces):
        for c in range(half // chunk):
            re = pl.ds(c * chunk, chunk)
            im = pl.ds(half + c * chunk, chunk)

            def add_block(j, acc):
                ur, ui = u_scr[j, re, :], u_scr[j, im, :]
                out = []
                for a, ii in enumerate(pieces):
                    m = ii - 1 - j + nb
                    gr, gi = g_ref[0, m, re, :], g_ref[0, m, im, :]
                    out.append((acc[a][0] + (ur * gr - ui * gi), acc[a][1] + (ur * gi + ui * gr)))
                return tuple(out)

            zero = jnp.zeros((chunk, width), F32)
            acc = lax.fori_loop(0, nb, add_block, tuple((zero, zero) for _ in pieces), unroll=min(4, nb))
            for a, ii in enumerate(pieces):
                y_scr[ii, re, :] = acc[a][0]
                y_scr[ii, im, :] = acc[a][1]

    def pair(p, carry):
        spectra([2 * p, 2 * p + 1])
        return carry

    lax.fori_loop(0, (nb + 1) // 2, pair, 0)
    if (nb + 1) % 2:
        spectra([nb])

    gate_scr[...] = _conv3(gate_ref[0], zero, zero, tg_ref[...]) + bg_ref[...]

    def block(i, carry):
        rows = pl.ds(pl.multiple_of(i * half, half), half)
        ycat = jnp.concatenate([y_scr[i + 1], y_scr[i]], axis=0)
        conv = _mm(ih_ref[...], ycat.astype(BF16))
        o_ref[0, rows, :] = gate_scr[rows, :] * (conv + bias_ref[...] * z_scr[rows, :])
        return carry

    lax.fori_loop(0, nb, block, 0)


def _hyena_conv(zin, zin_col0, gate_src, gate_col0, taps, tap_bias, bias, spectra, order, row0, nseq, seq, tc, conv_in):
    c_width = bias.shape[-1]
    nb = seq // HY_BLK
    nct = c_width // tc
    (fh, _), (ih, _) = _dft_consts()
    s0 = row0 // seq
    zc, gc = zin_col0 // tc, gate_col0 // tc
    tapc = (zc if conv_in else gc)
    full = lambda a: pl.BlockSpec(a.shape, lambda j, b: (0,) * a.ndim)
    zin3 = zin.reshape(-1, seq, zin.shape[-1])
    gate3 = gate_src.reshape(-1, seq, gate_src.shape[-1])
    zs0 = s0 if conv_in else 0
    return pl.pallas_call(
        functools.partial(_hyena_conv_kernel, nb=nb, conv_in=conv_in),
        grid=(nct, nseq),
        in_specs=[pl.BlockSpec((1, seq, tc), lambda j, b: (zs0 + b, 0, zc + j)),
                  pl.BlockSpec((1, seq, tc), lambda j, b: (s0 + b, 0, gc + j)),
                  pl.BlockSpec((3, tc), lambda j, b: (0, tapc + j)),
                  pl.BlockSpec((1, tc), lambda j, b: (0, tapc + j)),
                  pl.BlockSpec((3, tc), lambda j, b: (0, gc + j)),
                  pl.BlockSpec((1, tc), lambda j, b: (0, gc + j)),
                  pl.BlockSpec((1, tc), lambda j, b: (0, j)),
                  pl.BlockSpec((1, 2 * nb, 2 * HY_BLK, tc), lambda j, b: (order, 0, 0, j)),
                  full(fh), full(ih)],
        out_specs=pl.BlockSpec((1, seq, tc), lambda j, b: (b, 0, j)),
        out_shape=jax.ShapeDtypeStruct((nseq, seq, c_width), F32),
        scratch_shapes=[pltpu.VMEM((seq, tc), F32), pltpu.VMEM((seq, tc), F32),
                        pltpu.VMEM((nb, 2 * HY_BLK, tc), F32), pltpu.VMEM((nb + 1, 2 * HY_BLK, tc), F32)],
        compiler_params=_cparams(("arbitrary", "arbitrary")),
        name="hyena_conv",
    )(zin3, gate3, taps, tap_bias, taps, tap_bias, bias, spectra, fh, ih)


def _rwkv_scan_pass(r, kk, v, w2, b2, kt2, s0, row0, bn, seq):
    w_ = r.shape[-1]
    heads = w_ // A_HEAD_DIM
    n = A_HEAD_DIM
    bh = bn * heads
    kq = max(1, LANES // bh)
    nk = n // kq
    c = kq * bh
    rows = lambda x: _to_chain(x, row0, bn, seq, kq, False)
    if s0 is None:
        s0r = jnp.zeros((2, nk, n, c), F32)
    else:
        s0r = jnp.transpose(s0.reshape(bn, 2, heads, n, kq, nk), (1, 5, 3, 4, 0, 2)).reshape(2, nk, n, c)
    yf, yb, sfin = _scan(rows(r), rows(kk), _to_chain(v, row0, bn, seq, kq, True),
                         (rows(w2[0]), rows(w2[1])), (rows(b2[0]), rows(b2[1])), (rows(kt2[0]), rows(kt2[1])),
                         s0r, kq)
    sfin = jnp.transpose(sfin.reshape(2, nk, n, kq, bn, heads), (4, 0, 5, 2, 3, 1)).reshape(bn, 2, heads, n, n)
    return _from_chain(yf, bn, heads, kq), _from_chain(yb, bn, heads, kq), sfin


def kernel(x_prompt, x_sample, cache_b_k, cache_b_v, state_a, c, c_ctx, mod_w, mod_b, norm1, norm2,
           even_w_in, even_a_conv, even_a_w0, even_a_wu, even_a_a0, even_a_au, even_a_gu, even_a_kk,
           even_a_ka, even_a_rk, even_a_ln_w, even_a_ln_b, even_b_qnorm, even_b_knorm, even_w_out,
           odd_w_in, odd_c_conv, odd_c_conv_b, odd_c_fw1, odd_c_fb1, odd_c_freq, odd_c_fw2, odd_c_fb2,
           odd_c_fw3, odd_c_bias, odd_w_out, peer_wq, peer_keys, peer_u, peer_v):
    bp, sp, d = x_prompt.shape
    bs, ss, _ = x_sample.shape
    depth = mod_w.shape[0]
    tp, ts = bp * sp, bs * ss
    t_all = tp + ts
    a_width = even_a_w0.shape[-1]
    a_cols = even_a_conv.shape[-1]
    dh = even_b_qnorm.shape[-1]
    b_width = d // 2
    kv_width = b_width // B_GROUP
    kvh = kv_width // dh
    c_width = odd_c_bias.shape[-1]
    nkeys = peer_keys.shape[3]
    assert bs + 1 <= 8 and nkeys == LANES and peer_keys.shape[1] * PEER_TOPK == LANES

    def seg_ids(tile):
        return jnp.concatenate([jnp.zeros((tp // tile,), I32),
                                1 + jnp.arange(ts // tile, dtype=I32) // (ss // tile)])

    tm = _row_tile(tp, ss, cap=512)
    segs = {tile: seg_ids(tile) for tile in (tm, 2 * tm) if tp % tile == 0 and ss % tile == 0}
    seg = segs[tm]

    cond8 = jnp.zeros((8, d), F32).at[0].set(c_ctx).at[1:1 + bs].set(c)
    mods = _mod_table(cond8, mod_w, mod_b)

    x = jnp.concatenate([x_prompt.reshape(tp, d), x_sample.reshape(ts, d)], axis=0)
    new_k, new_v, new_s = [], [], []
    for layer in range(depth):
        j = layer // 2
        mod3 = mods[layer].reshape(8 * 6, 1, d)
        if layer % 2 == 0:
            PA = dict(a_conv=even_a_conv[j], a_w0=even_a_w0[j], a_wu=even_a_wu[j], a_a0=even_a_a0[j],
                      a_au=even_a_au[j], a_gu=even_a_gu[j], a_kk=even_a_kk[j], a_ka=even_a_ka[j],
                      a_rk=even_a_rk[j], a_ln_w=even_a_ln_w[j], a_ln_b=even_a_ln_b[j])
            w_in = even_w_in[j]
            w_perm = jnp.concatenate([w_in[:, :3 * a_width], w_in[:, a_cols:], w_in[:, 3 * a_width:a_cols]],
                                     axis=1).astype(BF16)
            offs = dict(q=3 * a_width, q_w=b_width, kv=3 * a_width + b_width, kv_w=2 * kv_width,
                        low=3 * a_width + b_width + 2 * kv_width, low_w=a_cols - 3 * a_width)
            assert a_width == b_width and offs['kv'] % offs['kv_w'] == 0 and offs['low'] % offs['low_w'] == 0
            z, _ = _nm_matmul(x, norm1[layer], mod3, 1, 0, segs, w_perm)
            tmr = _row_tile(sp, ss, cap=256)
            nt = t_all // tmr
            starts = np.concatenate([np.arange(0, tp, sp), tp + np.arange(0, ts, ss), [t_all]])
            tile0 = np.arange(nt) * tmr
            keep_prev = jnp.asarray(~np.isin(tile0, starts), F32)[:, None]
            keep_next = jnp.asarray(~np.isin(tile0 + tmr, starts), F32)[:, None]
            zt = z.reshape(nt, tmr, z.shape[-1])
            zero_row = jnp.zeros((1, z.shape[-1]), F32)
            prev_rows = (jnp.concatenate([zero_row, zt[:-1, tmr - 1]], axis=0) * keep_prev)[:, None, :]
            next_rows = (jnp.concatenate([zt[1:, 0], zero_row], axis=0) * keep_next)[:, None, :]
            r, kk, vv, w0, w1, b0, b1, kt0, kt1, g, bonus = _rwkv_pre(z, prev_rows, next_rows, PA, tmr, offs)
            outs = []
            for (row0, bn, seq, latent) in ((0, bp, sp, False), (tp, bs, ss, True)):
                rows = bn * seq
                q, k, v, k_f32, v_f32 = _attn_pre(z, even_b_qnorm[j], even_b_knorm[j], offs, row0, rows, seq,
                                                  tmr, latent)
                k = k.reshape(bn, seq, kv_width)
                v = v.reshape(bn, seq, kv_width)
                if latent:
                    past = cache_b_k.shape[2]
                    k = jnp.concatenate([k, cache_b_k[:, j].astype(BF16).reshape(bn, past, kv_width)], axis=1)
                    v = jnp.concatenate([v, cache_b_v[:, j].astype(BF16).reshape(bn, past, kv_width)], axis=1)
                    s0 = state_a[:, j]
                else:
                    s0 = None
                    new_k.append(k_f32.reshape(bn, seq, kvh, dh))
                    new_v.append(v_f32.reshape(bn, seq, kvh, dh))
                y_b = _attend(q.reshape(bn, seq, b_width), k, v, dh).reshape(rows, b_width)
                yf, yb, s_fin = _rwkv_scan_pass(r, kk, vv, (w0, w1), (b0, b1), (kt0, kt1), s0, row0, bn, seq)
                if not latent:
                    new_s.append(s_fin)
                y_a = _rwkv_post(yf, yb, bonus, g, row0, PA, tmr)
                outs.append(jnp.concatenate([y_a, y_b], axis=-1))
            mix_in = jnp.concatenate(outs, axis=0)
            x = _res_matmul(mix_in, even_w_out[j].astype(BF16), x, mod3, 2, seg, tm)
        else:
            PC = dict(c_fw1=odd_c_fw1[j], c_fb1=odd_c_fb1[j], c_freq=odd_c_freq[j], c_fw2=odd_c_fw2[j],
                      c_fb2=odd_c_fb2[j], c_fw3=odd_c_fw3[j])
            u_pre, _ = _nm_matmul(x, norm1[layer], mod3, 1, 0, segs, odd_w_in[j].astype(BF16))
            taps = odd_c_conv[j]
            tap_bias = odd_c_conv_b[j].reshape(1, 3 * c_width)
            zs = []
            for (row0, bn, seq) in ((0, bp, sp), (tp, bs, ss)):
                tc = _col_tile(c_width, max(LANES, HY_TILE_ELEMS // seq))
                spectra = _hyena_spectra(seq, PC, c_width, tc)
                z1 = _hyena_conv(u_pre, 2 * c_width, u_pre, 0, taps, tap_bias, odd_c_bias[j, 0:1], spectra, 0,
                                 row0, bn, seq, tc, True)
                z2 = _hyena_conv(z1.reshape(bn * seq, c_width), 0, u_pre, c_width, taps, tap_bias,
                                 odd_c_bias[j, 1:2], spectra, 1, row0, bn, seq, tc, False)
                zs.append(z2.reshape(bn * seq, c_width))
            x = _res_matmul(jnp.concatenate(zs, axis=0).astype(BF16), odd_w_out[j].astype(BF16), x, mod3, 2, seg, tm)
        q, hm = _nm_matmul(x, norm2[layer], mod3, 4, 3, segs, peer_wq[layer].astype(BF16))
        i1, i2, gate = _peer_route(q, peer_keys[layer])
        gmat = _gate_matrix(i1, i2, gate, nkeys)
        u_bf16, v_bf16 = _to_bf16(peer_u, layer), _to_bf16(peer_v, layer)
        if layer < depth - 1:
            x = _peer_experts(hm, gmat, u_bf16, v_bf16, x, mod3, 5, seg, tm)
        else:
            y_prompt = _peer_experts(hm, gmat, u_bf16, v_bf16, x, mod3, 5, seg, tm, 0, tp).reshape(bp, sp, d)
            y_sample = _peer_experts(hm, gmat, u_bf16, v_bf16, x, mod3, 5, seg, tm, tp, ts).reshape(bs, ss, d)
    return (y_prompt, y_sample, jnp.stack(new_k, axis=1), jnp.stack(new_v, axis=1), jnp.stack(new_s, axis=1))
```

```python
import functools
import math

import numpy as np
import jax
import jax.numpy as jnp
from jax import lax
from jax.experimental import pallas as pl
from jax.experimental.pallas import tpu as pltpu

F32 = jnp.float32
BF16 = jnp.bfloat16
I32 = jnp.int32

NORM_EPS = 1e-6
A_HEAD_DIM = 64
A_GN_EPS = 64e-5
B_GROUP = 4
GRID_W = 64
ROPE_THETA = 10000.0
HYENA_ORDER = 2
DECAY_TARGET = 1e-2
FAST_DECAY_PCT = 0.3
SLOW_DECAY_PCT = 1.5
PEER_TOPK = 16
LANES = 128
VMEM_LIMIT = 56 * 1024 * 1024


def _cparams(sem):
    return pltpu.CompilerParams(dimension_semantics=sem, vmem_limit_bytes=VMEM_LIMIT)


def _row_tile(*lengths, cap=512):
    t = cap
    while any(n % t for n in lengths):
        t //= 2
    return t


def _col_tile(n, cap):
    return max(t for t in range(LANES, cap + 1, LANES) if n % t == 0)


def _mod_kernel(c_ref, w_ref, b_ref, o_ref):
    c = c_ref[...]
    s = (c * jax.nn.sigmoid(c)).astype(BF16)
    o_ref[0] = jnp.dot(s, w_ref[0].astype(BF16), preferred_element_type=F32) + b_ref[0]


def _mod_table(cond8, mod_w, mod_b):
    depth, d, n = mod_w.shape
    tn = _row_tile(n, cap=1024)
    return pl.pallas_call(
        _mod_kernel,
        grid=(depth, n // tn),
        in_specs=[pl.BlockSpec((8, d), lambda l, j: (0, 0)),
                  pl.BlockSpec((1, d, tn), lambda l, j: (l, 0, j)),
                  pl.BlockSpec((1, 1, tn), lambda l, j: (l, 0, j))],
        out_specs=pl.BlockSpec((1, 8, tn), lambda l, j: (l, 0, j)),
        out_shape=jax.ShapeDtypeStruct((depth, 8, n), F32),
        compiler_params=_cparams(("parallel", "parallel")),
        name="mod_table",
    )(cond8, mod_w, mod_b.reshape(depth, 1, n))


def _nm_matmul_kernel(seg_ref, x_ref, g_ref, sc_ref, sh_ref, w_ref, o_ref, h_ref, h_scr, *, one_col_tile):
    del seg_ref

    def modulated():
        x = x_ref[...]
        y = x * lax.rsqrt(jnp.mean(x * x, axis=-1, keepdims=True) + NORM_EPS) * g_ref[...]
        return (y * (1.0 + sc_ref[0]) + sh_ref[0]).astype(BF16)

    if one_col_tile:
        h = modulated()
        h_ref[...] = h
        o_ref[...] = jnp.dot(h, w_ref[...], preferred_element_type=F32).astype(o_ref.dtype)
        return

    @pl.when(pl.program_id(1) == 0)
    def _():
        h = modulated()
        h_scr[...] = h
        h_ref[...] = h

    o_ref[...] = jnp.dot(h_scr[...], w_ref[...], preferred_element_type=F32).astype(o_ref.dtype)


W_RESIDENT_BYTES = 8 * 1024 * 1024


def _nm_matmul(x, g, mod3, sc_idx, sh_idx, segs, w_bf16, out_dtype=F32):
    t, d = x.shape
    n = w_bf16.shape[1]
    tm = min(segs)
    if d * n * 2 <= W_RESIDENT_BYTES:
        tn = n
    elif n % 1024 == 0 and len(segs) > 1:
        tm, tn = max(segs), 1024
    else:
        tn = _col_tile(n, 1664)
    seg = segs[tm]
    grid_spec = pltpu.PrefetchScalarGridSpec(
        num_scalar_prefetch=1,
        grid=(t // tm, n // tn),
        in_specs=[pl.BlockSpec((tm, d), lambda i, j, s: (i, 0)),
                  pl.BlockSpec((1, d), lambda i, j, s: (0, 0)),
                  pl.BlockSpec((1, 1, d), lambda i, j, s: (s[i] * 6 + sc_idx, 0, 0)),
                  pl.BlockSpec((1, 1, d), lambda i, j, s: (s[i] * 6 + sh_idx, 0, 0)),
                  pl.BlockSpec((d, tn), lambda i, j, s: (0, j))],
        out_specs=[pl.BlockSpec((tm, tn), lambda i, j, s: (i, j)),
                   pl.BlockSpec((tm, d), lambda i, j, s: (i, 0))],
        scratch_shapes=[pltpu.VMEM((tm, d), BF16)])
    return pl.pallas_call(
        functools.partial(_nm_matmul_kernel, one_col_tile=(tn == n)),
        grid_spec=grid_spec,
        out_shape=[jax.ShapeDtypeStruct((t, n), out_dtype), jax.ShapeDtypeStruct((t, d), BF16)],
        compiler_params=_cparams(("parallel", "arbitrary")),
        name="norm_mod_matmul",
    )(seg, x, g.reshape(1, d), mod3, mod3, w_bf16)


def _res_matmul_kernel(seg_ref, a_ref, w_ref, r_ref, gt_ref, o_ref):
    del seg_ref
    mm = jnp.dot(a_ref[...], w_ref[...], preferred_element_type=F32)
    o_ref[...] = r_ref[...] + gt_ref[0] * mm


def _res_matmul(a_bf16, w_bf16, res, mod3, gt_idx, seg, tm):
    t, k = a_bf16.shape
    n = w_bf16.shape[1]
    tn = n if k * n * 2 <= W_RESIDENT_BYTES else _col_tile(n, 1024)
    grid_spec = pltpu.PrefetchScalarGridSpec(
        num_scalar_prefetch=1,
        grid=(t // tm, n // tn),
        in_specs=[pl.BlockSpec((tm, k), lambda i, j, s: (i, 0)),
                  pl.BlockSpec((k, tn), lambda i, j, s: (0, j)),
                  pl.BlockSpec((tm, tn), lambda i, j, s: (i, j)),
                  pl.BlockSpec((1, 1, tn), lambda i, j, s: (s[i] * 6 + gt_idx, 0, j))],
        out_specs=pl.BlockSpec((tm, tn), lambda i, j, s: (i, j)))
    return pl.pallas_call(
        _res_matmul_kernel,
        grid_spec=grid_spec,
        out_shape=jax.ShapeDtypeStruct((t, n), F32),
        compiler_params=_cparams(("parallel", "parallel")),
        name="res_matmul",
    )(seg, a_bf16, w_bf16, res, mod3)


def _scan_kernel(rf_ref, kkf_ref, vf_ref, wf_ref, bf_ref, ktf_ref, rb_ref, kkb_ref, vb_ref, wb_ref, bb_ref,
                 ktb_ref, s0_ref, yf_ref, yb_ref, sfin_ref, s_scr, *, tb_steps, nk, kq):
    ti = pl.program_id(1)

    @pl.when(ti == 0)
    def _():
        s_scr[...] = s0_ref[...]

    def tree(parts):
        while len(parts) > 1:
            parts = [parts[i] + parts[i + 1] for i in range(0, len(parts), 2)]
        return parts[0]

    def all_parts(p):
        part = LANES // kq
        return tree([p] + [pltpu.roll(p, i * part, 1) for i in range(1, kq)])

    nacc = 4

    def advance(d, t, r_ref, kk_ref, v_ref, w_ref, b_ref, kt_ref, y_ref):
        row = pl.ds(t, 1)
        accs = [None] * nacc
        for k in range(nk):
            p = s_scr[d, k] * kk_ref[k, row, :]
            accs[k % nacc] = p if accs[k % nacc] is None else accs[k % nacc] + p
        sa = all_parts(tree(accs))
        v = v_ref[t]
        yacc = [None] * nacc
        for k in range(nk):
            s_new = s_scr[d, k] * w_ref[k, row, :] - sa * b_ref[k, row, :] + v * kt_ref[k, row, :]
            s_scr[d, k] = s_new
            p = s_new * r_ref[k, row, :]
            yacc[k % nacc] = p if yacc[k % nacc] is None else yacc[k % nacc] + p
        y_ref[t] = all_parts(tree(yacc))

    def step(s, carry):
        advance(0, s, rf_ref, kkf_ref, vf_ref, wf_ref, bf_ref, ktf_ref, yf_ref)
        advance(1, tb_steps - 1 - s, rb_ref, kkb_ref, vb_ref, wb_ref, bb_ref, ktb_ref, yb_ref)
        return carry

    lax.fori_loop(0, tb_steps, step, 0, unroll=2)

    @pl.when(ti == pl.num_programs(1) - 1)
    def _():
        sfin_ref[...] = s_scr[...]


def _scan(r, kk, v, w2, b2, kt2, s0, kq):
    nk, seq, c = r.shape
    nv = v.shape[1]
    tb_steps = _row_tile(seq, cap=16)
    nt = seq // tb_steps
    rowf = pl.BlockSpec((nk, tb_steps, LANES), lambda gi, ti: (0, ti, gi))
    rowb = pl.BlockSpec((nk, tb_steps, LANES), lambda gi, ti: (0, nt - 1 - ti, gi))
    valf = pl.BlockSpec((tb_steps, nv, LANES), lambda gi, ti: (ti, 0, gi))
    valb = pl.BlockSpec((tb_steps, nv, LANES), lambda gi, ti: (nt - 1 - ti, 0, gi))
    st = pl.BlockSpec((2, nk, nv, LANES), lambda gi, ti: (0, 0, 0, gi))
    return pl.pallas_call(
        functools.partial(_scan_kernel, tb_steps=tb_steps, nk=nk, kq=kq),
        grid=(c // LANES, nt),
        in_specs=[rowf, rowf, valf, rowf, rowf, rowf, rowb, rowb, valb, rowb, rowb, rowb, st],
        out_specs=[valf, valb, st],
        out_shape=[jax.ShapeDtypeStruct((seq, nv, c), F32), jax.ShapeDtypeStruct((seq, nv, c), F32),
                   jax.ShapeDtypeStruct((2, nk, nv, c), F32)],
        scratch_shapes=[pltpu.VMEM((2, nk, nv, LANES), F32)],
        compiler_params=_cparams(("parallel", "arbitrary")),
        name="rwkv7_scan",
    )(r, kk, v, w2[0], b2[0], kt2[0], r, kk, v, w2[1], b2[1], kt2[1], s0)


CHAIN_TB = 128


def _to_chain_kernel(x_ref, o_ref, a_scr, *stage, heads, nk, kq, values):
    bg, tb = x_ref.shape[0], x_ref.shape[1]
    n = A_HEAD_DIM
    for b in range(bg):
        xt = x_ref[b].T
        a_scr[b] = jnp.swapaxes(xt.reshape(heads, n, tb), 0, 1)
    for p in range(n if values else nk):
        pieces = [a_scr[b, p if values else q * nk + p] for q in range(kq) for b in range(bg)]
        tile = jnp.concatenate(pieces, axis=0).T
        if values:
            stage[0][p] = tile
        else:
            o_ref[p] = tile
    if values:
        o_ref[...] = jnp.swapaxes(stage[0][...], 0, 1)


def _to_chain(x, row0, bn, seq, kq, values):
    w_ = x.shape[-1]
    heads = w_ // A_HEAD_DIM
    n = A_HEAD_DIM
    nk = n // kq
    bg = LANES // (kq * heads)
    ng = bn // bg
    c = kq * bn * heads
    assert kq * bg * heads == LANES and c == ng * LANES and row0 % (bg * seq) == 0
    tb = min(CHAIN_TB, seq)
    g0 = row0 // (bg * seq)
    scratch = [pltpu.VMEM((bg, n, heads, tb), F32)]
    if values:
        out_spec = pl.BlockSpec((tb, n, LANES), lambda g, t: (t, 0, g))
        out_shape = jax.ShapeDtypeStruct((seq, n, c), F32)
        scratch.append(pltpu.VMEM((n, tb, LANES), F32))
    else:
        out_spec = pl.BlockSpec((nk, tb, LANES), lambda g, t: (0, t, g))
        out_shape = jax.ShapeDtypeStruct((nk, seq, c), F32)
    return pl.pallas_call(
        functools.partial(_to_chain_kernel, heads=heads, nk=nk, kq=kq, values=values),
        grid=(ng, seq // tb),
        in_specs=[pl.BlockSpec((bg, tb, w_), lambda g, t: (g0 + g, t, 0))],
        out_specs=out_spec,
        out_shape=out_shape,
        scratch_shapes=scratch,
        compiler_params=_cparams(("parallel", "parallel")),
        name="to_chain_layout",
    )(x.reshape(-1, seq, w_))


def _from_chain_kernel(y_ref, o_ref, stage, a_scr, *, heads):
    bg = o_ref.shape[0]
    n = A_HEAD_DIM
    tb = y_ref.shape[0]
    stage[...] = jnp.swapaxes(y_ref[...], 0, 1)
    for v in range(n):
        rows = stage[v].T
        for b in range(bg):
            a_scr[b, v] = rows[b * heads:(b + 1) * heads]
    for b in range(bg):
        o_ref[b] = jnp.swapaxes(a_scr[b], 0, 1).reshape(heads * n, tb).T


def _from_chain(y, bn, heads, kq):
    seq, n, c = y.shape
    bg = LANES // (kq * heads)
    tb = min(CHAIN_TB, seq)
    out = pl.pallas_call(
        functools.partial(_from_chain_kernel, heads=heads),
        grid=(c // LANES, seq // tb),
        in_specs=[pl.BlockSpec((tb, n, LANES), lambda g, t: (t, 0, g))],
        out_specs=pl.BlockSpec((bg, tb, heads * n), lambda g, t: (g, t, 0)),
        out_shape=jax.ShapeDtypeStruct((bn, seq, heads * n), F32),
        scratch_shapes=[pltpu.VMEM((n, tb, LANES), F32), pltpu.VMEM((bg, n, heads, tb), F32)],
        compiler_params=_cparams(("parallel", "parallel")),
        name="from_chain_layout",
    )(y)
    return out.reshape(bn * seq, heads * n)


def _attn_kernel(q_ref, k_ref, v_ref, o_ref, *, dh):
    k = k_ref[0]
    v = v_ref[0]
    for g in range(B_GROUP):
        q = q_ref[0, :, g * dh:(g + 1) * dh]
        s = lax.dot_general(q, k, (((1,), (1,)), ((), ())), preferred_element_type=F32)
        m = jnp.max(s, axis=-1, keepdims=True)
        p = jnp.exp(s - m)
        den = jnp.sum(p, axis=-1, keepdims=True)
        o = jnp.dot(p.astype(BF16), v, preferred_element_type=F32) / den
        o_ref[0, :, g * dh:(g + 1) * dh] = o.astype(o_ref.dtype)


def _attend(q, k, v, dh):
    bn, lq, qw = q.shape
    lk = k.shape[1]
    kvh = k.shape[2] // dh
    gw = B_GROUP * dh
    tq = _row_tile(lq, cap=256)
    return pl.pallas_call(
        functools.partial(_attn_kernel, dh=dh),
        grid=(bn, kvh, lq // tq),
        in_specs=[pl.BlockSpec((1, tq, gw), lambda b, h, i: (b, i, h)),
                  pl.BlockSpec((1, lk, dh), lambda b, h, i: (b, 0, h)),
                  pl.BlockSpec((1, lk, dh), lambda b, h, i: (b, 0, h))],
        out_specs=pl.BlockSpec((1, tq, gw), lambda b, h, i: (b, i, h)),
        out_shape=jax.ShapeDtypeStruct((bn, lq, qw), BF16),
        compiler_params=_cparams(("parallel", "parallel", "parallel")),
        name="attention",
    )(q, k, v)


G_GROUP = 16


def _gate_matrix_kernel(i1_ref, i2_ref, g_ref, o_ref, gtmp_scr, *, nkeys):
    iota = lax.broadcasted_iota(I32, (nkeys, nkeys), 0)

    def build(grp, carry):
        base = pl.multiple_of(grp * G_GROUP, G_GROUP)
        def token(tt, c2):
            t = base + tt
            a_t = jnp.where(iota == i1_ref[pl.ds(t, 1), :], 1.0, 0.0).astype(BF16)
            b_t = jnp.where(iota == i2_ref[pl.ds(t, 1), :], g_ref[pl.ds(t, 1), :], 0.0).astype(BF16)
            gtmp_scr[tt] = lax.dot_general(a_t, b_t, (((1,), (1,)), ((), ())), preferred_element_type=F32)
            return c2

        lax.fori_loop(0, G_GROUP, token, 0, unroll=G_GROUP)
        by_n1 = jnp.swapaxes(gtmp_scr[...], 0, 1).astype(BF16)
        for n1 in range(nkeys):
            o_ref[pl.ds(base, G_GROUP), n1 * nkeys:(n1 + 1) * nkeys] = by_n1[n1]
        return carry

    lax.fori_loop(0, o_ref.shape[0] // G_GROUP, build, 0)


def _gate_matrix(i1, i2, gate, nkeys):
    t, nj = i1.shape
    tb = LANES
    sel = pl.BlockSpec((tb, nj), lambda i: (i, 0))
    return pl.pallas_call(
        functools.partial(_gate_matrix_kernel, nkeys=nkeys),
        grid=(t // tb,),
        in_specs=[sel, sel, sel],
        out_specs=pl.BlockSpec((tb, nkeys * nkeys), lambda i: (i, 0)),
        out_shape=jax.ShapeDtypeStruct((t, nkeys * nkeys), BF16),
        scratch_shapes=[pltpu.VMEM((G_GROUP, nkeys, nkeys), F32)],
        compiler_params=_cparams(("parallel",)),
        name="peer_gate_matrix",
    )(i1, i2, gate)


def _to_bf16_kernel(x_ref, o_ref):
    o_ref[...] = x_ref[0].astype(BF16)


def _to_bf16(tables, layer):
    _, e, d = tables.shape
    te = _row_tile(e, cap=1024)
    return pl.pallas_call(
        _to_bf16_kernel,
        grid=(e // te,),
        in_specs=[pl.BlockSpec((1, te, d), lambda i: (layer, i, 0))],
        out_specs=pl.BlockSpec((te, d), lambda i: (i, 0)),
        out_shape=jax.ShapeDtypeStruct((e, d), BF16),
        compiler_params=_cparams(("parallel",)),
        name="to_bf16",
    )(tables)


def _peer_kernel(seg_ref, xb_ref, gm_ref, u_ref, v_ref, r_ref, gt_ref, o_ref, acc_scr):
    del seg_ref
    e = pl.program_id(1)

    @pl.when(e == 0)
    def _():
        acc_scr[...] = jnp.zeros_like(acc_scr)

    h = lax.dot_general(xb_ref[...], u_ref[...], (((1,), (1,)), ((), ())), preferred_element_type=F32)
    act = 0.5 * h * (1.0 + lax.erf(h * (1.0 / math.sqrt(2.0)))) * gm_ref[...].astype(F32)
    acc_scr[...] += jnp.dot(act.astype(BF16), v_ref[...], preferred_element_type=F32)

    @pl.when(e == pl.num_programs(1) - 1)
    def _():
        o_ref[...] = r_ref[...] + gt_ref[0] * acc_scr[...]


def _peer_experts(xb, gmat, u_bf16, v_bf16, res, mod3, gt_idx, seg, tm, row0=0, rows=None):
    t, d = xb.shape
    rows = t if rows is None else rows
    r0 = row0 // tm
    ne = u_bf16.shape[0]
    te = 1024
    grid_spec = pltpu.PrefetchScalarGridSpec(
        num_scalar_prefetch=1,
        grid=(rows // tm, ne // te),
        in_specs=[pl.BlockSpec((tm, d), lambda i, e, s: (r0 + i, 0)),
                  pl.BlockSpec((tm, te), lambda i, e, s: (r0 + i, e)),
                  pl.BlockSpec((te, d), lambda i, e, s: (e, 0)),
                  pl.BlockSpec((te, d), lambda i, e, s: (e, 0)),
                  pl.BlockSpec((tm, d), lambda i, e, s: (r0 + i, 0)),
                  pl.BlockSpec((1, 1, d), lambda i, e, s: (s[r0 + i] * 6 + gt_idx, 0, 0))],
        out_specs=pl.BlockSpec((tm, d), lambda i, e, s: (i, 0)),
        scratch_shapes=[pltpu.VMEM((tm, d), F32)])
    return pl.pallas_call(
        _peer_kernel,
        grid_spec=grid_spec,
        out_shape=jax.ShapeDtypeStruct((rows, d), F32),
        compiler_params=_cparams(("parallel", "arbitrary")),
        name="peer_experts",
    )(seg, xb, gmat, u_bf16, v_bf16, res, mod3)


def _topk_rows(vals, payload, rows_out):
    big = jnp.float32(2 ** 30)
    top_v = jnp.zeros(rows_out.shape, F32)
    top_p = jnp.zeros(rows_out.shape, F32)
    for it in range(PEER_TOPK):
        m = jnp.max(vals, axis=0, keepdims=True)
        sel = jnp.min(jnp.where(vals == m, payload, big), axis=0, keepdims=True)
        top_v = jnp.where(rows_out == it, m, top_v)
        top_p = jnp.where(rows_out == it, sel, top_p)
        vals = jnp.where(payload == sel, -jnp.inf, vals)
    return top_v, top_p


def _top_pair_sums(s1, s2, rows_out):
    k = PEER_TOPK
    half = k // 2
    tt = s1.shape[1]
    big = jnp.float32(2 ** 30)
    lists = [s1[:half] + s2[j:j + 1] for j in range(k)]
    singles = s1[half:] + s2[0:1]
    sub = lax.broadcasted_iota(I32, (half, tt), 0).astype(F32)
    head_id = sub * k
    single_id = (sub + half) * k
    top_v = jnp.zeros(rows_out.shape, F32)
    top_p = jnp.zeros(rows_out.shape, F32)
    for it in range(k):
        m = jnp.max(jnp.maximum(lists[0], singles), axis=0, keepdims=True)
        sel = jnp.min(jnp.minimum(jnp.where(lists[0] == m, head_id, big), jnp.where(singles == m, single_id, big)),
                      axis=0, keepdims=True)
        top_v = jnp.where(rows_out == it, m, top_v)
        top_p = jnp.where(rows_out == it, sel, top_p)
        pop = head_id == sel
        lists = [jnp.where(pop, lists[j + 1], lists[j]) for j in range(k - 1)] + [jnp.where(pop, -jnp.inf, lists[-1])]
        head_id = jnp.where(pop, head_id + 1.0, head_id)
        singles = jnp.where(single_id == sel, -jnp.inf, singles)
    return top_v, top_p


def _gather_rows(table, sel):
    out = jnp.zeros(sel.shape, table.dtype)
    for i in range(PEER_TOPK):
        out = jnp.where(sel == i, table[i:i + 1, :], out)
    return out


def _route_kernel(q_ref, keys_ref, i1_ref, i2_ref, g_ref, n1_scr, n2_scr, gate_scr, *, heads, nkeys, dq):
    tt = q_ref.shape[0]
    k = PEER_TOPK
    n_iota = lax.broadcasted_iota(I32, (nkeys, tt), 0).astype(F32)
    rows_out = lax.broadcasted_iota(I32, (k, tt), 0)

    def head(h, carry):
        tops = []
        for c in range(2):
            col = pl.multiple_of((h * 2 + c) * dq, dq)
            qhc = q_ref[:, pl.ds(col, dq)].astype(BF16)
            khc = keys_ref[h, c].astype(BF16)
            s = lax.dot_general(khc, qhc, (((1,), (1,)), ((), ())), preferred_element_type=F32)
            tops.append(_topk_rows(s, n_iota, rows_out))
        (s1, i1), (s2, i2) = tops
        top, ci = _top_pair_sums(s1, s2, rows_out)
        ci = ci.astype(I32)
        n1 = _gather_rows(i1, lax.shift_right_logical(ci, 4)).astype(I32)
        n2 = _gather_rows(i2, lax.bitwise_and(ci, k - 1)).astype(I32)
        ex = jnp.exp(top - top[0:1, :])
        gate = ex / jnp.sum(ex, axis=0, keepdims=True)
        row = pl.multiple_of(h * k, k)
        n1_scr[pl.ds(row, k), :] = n1
        n2_scr[pl.ds(row, k), :] = n2
        gate_scr[pl.ds(row, k), :] = gate
        return carry

    lax.fori_loop(0, heads, head, 0)
    i1_ref[...] = n1_scr[...].T
    i2_ref[...] = n2_scr[...].T
    g_ref[...] = gate_scr[...].T


def _peer_route(q, sub_keys):
    t, qw = q.shape
    heads, _, nkeys, dq = sub_keys.shape
    hk = heads * PEER_TOPK
    tt = LANES
    out = pl.BlockSpec((tt, hk), lambda i: (i, 0))
    return pl.pallas_call(
        functools.partial(_route_kernel, heads=heads, nkeys=nkeys, dq=dq),
        grid=(t // tt,),
        in_specs=[pl.BlockSpec((tt, qw), lambda i: (i, 0)),
                  pl.BlockSpec((heads, 2, nkeys, dq), lambda i: (0, 0, 0, 0))],
        out_specs=[out, out, out],
        out_shape=[jax.ShapeDtypeStruct((t, hk), I32), jax.ShapeDtypeStruct((t, hk), I32),
                   jax.ShapeDtypeStruct((t, hk), F32)],
        scratch_shapes=[pltpu.VMEM((hk, tt), I32), pltpu.VMEM((hk, tt), I32), pltpu.VMEM((hk, tt), F32)],
        compiler_params=_cparams(("parallel",)),
        name="peer_route",
    )(q, sub_keys)


def _split(x):
    hi = x.astype(BF16)
    return hi, (x - hi.astype(F32)).astype(BF16)


def _mm(a, b):
    return jnp.dot(a, b, preferred_element_type=F32)


def _head_sum(x, ones_blockdiag):
    hi, lo = _split(x)
    return _mm(hi, ones_blockdiag) + _mm(lo, ones_blockdiag)


def _shift_rows(x, first_row, last_row):
    n = x.shape[0]
    rows = lax.broadcasted_iota(I32, x.shape, 0)
    prev = jnp.where(rows == 0, first_row, pltpu.roll(x, 1, 0))
    nxt = jnp.where(rows == n - 1, last_row, pltpu.roll(x, n - 1, 0))
    return prev, nxt


def _conv3(x, first_row, last_row, taps):
    prev, nxt = _shift_rows(x, first_row, last_row)
    return prev * taps[0:1] + x * taps[1:2] + nxt * taps[2:3]


def _rwkv_pre_kernel(r_ref, k_ref, v_ref, low_ref, pr_ref, pk_ref, pv_ref, plow_ref, nr_ref, nk_ref, nv_ref,
                     nlow_ref, cr_ref, ck_ref, cv_ref, clow_ref, w0_ref, wu_ref, a0_ref, au_ref, gu_ref,
                     kkg_ref, ka_ref, rk_ref, ones_ref,
                     ro_ref, kko_ref, vo_ref, w0o_ref, w1o_ref, b0o_ref, b1o_ref, kt0o_ref, kt1o_ref,
                     go_ref, bonus_ref, *, rw, ra):
    r = _conv3(r_ref[...], pr_ref[0], nr_ref[0], cr_ref[...])
    k = _conv3(k_ref[...], pk_ref[0], nk_ref[0], ck_ref[...])
    v = _conv3(v_ref[...], pv_ref[0], nv_ref[0], cv_ref[...])
    low = _conv3(low_ref[...], plow_ref[0], nlow_ref[0], clow_ref[...])
    ones = ones_ref[...]
    kk = k * kkg_ref[...]
    kk = kk * lax.rsqrt(_head_sum(kk * kk, ones) + 1e-12)
    ro_ref[...] = r
    kko_ref[...] = kk
    vo_ref[...] = v
    bonus_ref[...] = _head_sum(r * k * rk_ref[...], ones) * v
    gd = low[:, 2 * rw + 2 * ra:]
    go_ref[...] = _mm(jax.nn.sigmoid(gd).astype(BF16), gu_ref[...])
    for d_, (wo, bo, kto) in enumerate(((w0o_ref, b0o_ref, kt0o_ref), (w1o_ref, b1o_ref, kt1o_ref))):
        wd = low[:, d_ * rw:(d_ + 1) * rw]
        ad = low[:, 2 * rw + d_ * ra:2 * rw + (d_ + 1) * ra]
        lw = w0_ref[d_:d_ + 1, :] + _mm(jnp.tanh(wd).astype(BF16), wu_ref[d_])
        softplus = jnp.maximum(-lw, 0.0) + jnp.log1p(jnp.exp(-jnp.abs(lw)))
        wo[...] = jnp.exp(-jnp.exp(-softplus - 0.5))
        a = jax.nn.sigmoid(a0_ref[d_:d_ + 1, :] + _mm(ad.astype(BF16), au_ref[d_]))
        bo[...] = kk * a
        kto[...] = k * (1.0 + (a - 1.0) * ka_ref[...])


def _rwkv_pre(z, prev_rows, next_rows, P, tm, offs):
    t = z.shape[0]
    w_ = P['a_w0'].shape[-1]
    rw, ra = P['a_wu'].shape[1], P['a_au'].shape[1]
    lw = offs['low_w']
    li = offs['low'] // lw
    row = lambda c, wd: pl.BlockSpec((tm, wd), lambda i: (i, c))
    edge = lambda c, wd: pl.BlockSpec((1, 1, wd), lambda i: (i, 0, c))
    full = lambda a: pl.BlockSpec(a.shape, lambda i: (0,) * a.ndim)
    conv = P['a_conv']
    consts = [conv[:, :w_], conv[:, w_:2 * w_], conv[:, 2 * w_:3 * w_], conv[:, 3 * w_:],
              P['a_w0'], P['a_wu'].astype(BF16), P['a_a0'], P['a_au'].astype(BF16), P['a_gu'].astype(BF16),
              P['a_kk'].reshape(1, w_), P['a_ka'].reshape(1, w_), P['a_rk'].reshape(1, w_),
              jnp.asarray(_blockdiag_ones(w_), BF16)]
    out = pl.BlockSpec((tm, w_), lambda i: (i, 0))
    return pl.pallas_call(
        functools.partial(_rwkv_pre_kernel, rw=rw, ra=ra),
        grid=(t // tm,),
        in_specs=[row(0, w_), row(1, w_), row(2, w_), row(li, lw),
                  edge(0, w_), edge(1, w_), edge(2, w_), edge(li, lw),
                  edge(0, w_), edge(1, w_), edge(2, w_), edge(li, lw)] + [full(a) for a in consts],
        out_specs=[out] * 11,
        out_shape=[jax.ShapeDtypeStruct((t, w_), F32)] * 11,
        compiler_params=_cparams(("parallel",)),
        name="rwkv7_pre",
    )(z, z, z, z, prev_rows, prev_rows, prev_rows, prev_rows, next_rows, next_rows, next_rows, next_rows, *consts)


def _blockdiag_ones(width):
    idx = np.arange(width) // A_HEAD_DIM
    return (idx[:, None] == idx[None, :]).astype(np.float32)


def _rwkv_post_kernel(yf_ref, yb_ref, bonus_ref, g_ref, lnw_ref, lnb_ref, ones_ref, o_ref):
    ones = ones_ref[...]
    y = yf_ref[...] + yb_ref[...]
    mu = _head_sum(y, ones) * (1.0 / A_HEAD_DIM)
    dlt = y - mu
    var = _head_sum(dlt * dlt, ones) * (1.0 / A_HEAD_DIM)
    yn = dlt * lax.rsqrt(var + A_GN_EPS) * lnw_ref[...] + lnb_ref[...]
    o_ref[...] = ((yn + bonus_ref[...]) * g_ref[...]).astype(o_ref.dtype)


def _rwkv_post(yf, yb, bonus, g, row0, P, tm):
    t, w_ = yf.shape
    r0 = row0 // tm
    row = pl.BlockSpec((tm, w_), lambda i: (i, 0))
    off = pl.BlockSpec((tm, w_), lambda i: (r0 + i, 0))
    vec = pl.BlockSpec((1, w_), lambda i: (0, 0))
    return pl.pallas_call(
        _rwkv_post_kernel,
        grid=(t // tm,),
        in_specs=[row, row, off, off, vec, vec, pl.BlockSpec((w_, w_), lambda i: (0, 0))],
        out_specs=row,
        out_shape=jax.ShapeDtypeStruct((t, w_), BF16),
        compiler_params=_cparams(("parallel",)),
        name="rwkv7_post",
    )(yf, yb, bonus, g, P['a_ln_w'].reshape(1, w_), P['a_ln_b'].reshape(1, w_),
      jnp.asarray(_blockdiag_ones(w_), BF16))


def _attn_pre_kernel(q_ref, kv_ref, qn_ref, kn_ref, cos_ref, sin_ref, qo_ref, ko_ref, vo_ref, kf_ref, vf_ref,
                     *, dh, latent):
    def rms(x, g):
        return x * lax.rsqrt(jnp.mean(x * x, axis=-1, keepdims=True) + NORM_EPS) * g

    def rope(x):
        if not latent:
            return x
        lanes = lax.broadcasted_iota(I32, x.shape, 1)
        quarter = dh // 4
        partner = jnp.where(lanes % (2 * quarter) < quarter, pltpu.roll(x, dh - quarter, 1), pltpu.roll(x, quarter, 1))
        return x * cos_ref[...] + partner * sin_ref[...]

    nq = q_ref.shape[1] // dh
    nkv = kv_ref.shape[1] // (2 * dh)
    for h in range(nq):
        q = rope(rms(q_ref[:, h * dh:(h + 1) * dh], qn_ref[...]))
        qo_ref[:, h * dh:(h + 1) * dh] = (q * dh ** -0.5).astype(BF16)
    for h in range(nkv):
        k = rms(kv_ref[:, h * dh:(h + 1) * dh], kn_ref[...])
        v = kv_ref[:, (nkv + h) * dh:(nkv + h + 1) * dh]
        kf_ref[:, h * dh:(h + 1) * dh] = k
        vf_ref[:, h * dh:(h + 1) * dh] = v
        ko_ref[:, h * dh:(h + 1) * dh] = rope(k).astype(BF16)
        vo_ref[:, h * dh:(h + 1) * dh] = v.astype(BF16)


def _rope_tables(seq, dh):
    quarter = dh // 4
    inv = ROPE_THETA ** (-np.arange(quarter, dtype=np.float64) / quarter)
    pos = np.arange(seq)
    ang_r = (pos // GRID_W)[:, None] * inv[None, :]
    ang_c = (pos % GRID_W)[:, None] * inv[None, :]
    cos = np.concatenate([np.cos(ang_r)] * 2 + [np.cos(ang_c)] * 2, axis=1)
    sin = np.concatenate([-np.sin(ang_r), np.sin(ang_r), -np.sin(ang_c), np.sin(ang_c)], axis=1)
    return jnp.asarray(cos, F32), jnp.asarray(sin, F32)


def _attn_pre(z, qnorm, knorm, offs, row0, rows, seq, tm, latent):
    dh = qnorm.shape[-1]
    bw, kvw2 = offs['q_w'], offs['kv_w']
    r0 = row0 // tm
    per_seq = seq // tm
    cos, sin = _rope_tables(seq, dh) if latent else (jnp.zeros((tm, dh), F32), jnp.zeros((tm, dh), F32))
    tab = pl.BlockSpec((tm, dh), (lambda i: (i % per_seq, 0)) if latent else (lambda i: (0, 0)))
    vec = pl.BlockSpec((1, dh), lambda i: (0, 0))
    kvo = pl.BlockSpec((tm, kvw2 // 2), lambda i: (i, 0))
    return pl.pallas_call(
        functools.partial(_attn_pre_kernel, dh=dh, latent=latent),
        grid=(rows // tm,),
        in_specs=[pl.BlockSpec((tm, bw), lambda i: (r0 + i, offs['q'] // bw)),
                  pl.BlockSpec((tm, kvw2), lambda i: (r0 + i, offs['kv'] // kvw2)),
                  vec, vec, tab, tab],
        out_specs=[pl.BlockSpec((tm, bw), lambda i: (i, 0)), kvo, kvo, kvo, kvo],
        out_shape=[jax.ShapeDtypeStruct((rows, bw), BF16), jax.ShapeDtypeStruct((rows, kvw2 // 2), BF16),
                   jax.ShapeDtypeStruct((rows, kvw2 // 2), BF16), jax.ShapeDtypeStruct((rows, kvw2 // 2), F32),
                   jax.ShapeDtypeStruct((rows, kvw2 // 2), F32)],
        compiler_params=_cparams(("parallel",)),
        name="attention_pre",
    )(z, z, qnorm.reshape(1, dh), knorm.reshape(1, dh), cos, sin)


HY_BLK = 256
HY_TILE_ELEMS = 128 * 1024


def _dft_consts():
    n = 2 * HY_BLK
    k = np.arange(HY_BLK, dtype=np.float64)[:, None] + 0.5
    s = np.arange(HY_BLK, dtype=np.float64)[None, :]
    th = 2.0 * np.pi * k * s / n
    fwd = np.concatenate([np.cos(th), -np.sin(th)], axis=0)
    tau = np.arange(n, dtype=np.float64)[:, None]
    ph = 2.0 * np.pi * tau * (np.arange(HY_BLK, dtype=np.float64)[None, :] + 0.5) / n
    inv = np.concatenate([np.cos(ph), -np.sin(ph)], axis=1) * (2.0 / n)
    inv_cat = np.concatenate([inv[:HY_BLK], inv[HY_BLK:]], axis=1)

    return jnp.asarray(fwd, F32).astype(BF16), jnp.asarray(inv_cat, F32).astype(BF16)


def _lag_features(seq, emb):
    bands = (emb - 1) // 2
    t = np.linspace(0.0, 1.0, seq)
    wpos = 2.0 * np.pi * np.arange(seq) / seq
    f = np.linspace(1e-4, bands - 1, bands)
    z = np.concatenate([t[:, None], np.cos(f[None, :] * wpos[:, None]), -np.sin(f[None, :] * wpos[:, None])], axis=1)
    lag = np.concatenate([np.zeros(1, np.int64), np.arange(seq - 1, 0, -1), np.arange(seq)])
    return jnp.asarray(z[lag], F32)


def _hyena_filter_kernel(z_ref, fw1_ref, fb1_ref, freq_ref, fw2_ref, fb2_ref, w3b_ref, w3f_ref, dl_ref,
                         fh_ref, g_ref, hdn_scr, f_scr, *, seq):
    @pl.when((pl.program_id(0) == 0) & (pl.program_id(1) == 0))
    def _():
        h1 = jnp.sin(freq_ref[...] * (_mm(z_ref[...].astype(BF16), fw1_ref[...]) + fb1_ref[...]))
        hdn_scr[...] = jnp.sin(freq_ref[...] * (_mm(h1.astype(BF16), fw2_ref[...]) + fb2_ref[...])).astype(BF16)

    decay = jnp.exp(-z_ref[:, 0:1] * dl_ref[...])
    f_scr[0:seq, :] = _mm(hdn_scr[0:seq, :], w3b_ref[...]) * decay[0:seq, :]
    f_scr[seq:2 * seq, :] = _mm(hdn_scr[seq:2 * seq, :], w3f_ref[...]) * decay[seq:2 * seq, :]
    f = f_scr[...]
    scale = lax.rsqrt(jnp.sum(f * f, axis=0, keepdims=True) + 1e-12)
    rows = lax.broadcasted_iota(I32, f.shape, 0)
    f_scr[...] = jnp.where(rows == 0, 0.0, f * scale)
    for m in range(2 * seq // HY_BLK):
        g_ref[0, m] = _mm(fh_ref[...], f_scr[m * HY_BLK:(m + 1) * HY_BLK, :].astype(BF16))


def _hyena_spectra(seq, P, c_width, tc):
    emb, hid = P['c_fw1'].shape
    z = _lag_features(seq, emb)
    embp = 64
    z = jnp.pad(z, ((0, 0), (0, embp - emb)))
    fw1 = jnp.pad(P['c_fw1'], ((0, embp - emb), (0, 0))).astype(BF16)
    deltas = jnp.asarray(np.abs(np.linspace(math.log(DECAY_TARGET) / SLOW_DECAY_PCT,
                                            math.log(DECAY_TARGET) / FAST_DECAY_PCT, c_width)), F32).reshape(1, c_width)
    fh, _ = _dft_consts()
    nct = c_width // tc
    nseg = 2 * seq // HY_BLK
    full = lambda a: pl.BlockSpec(a.shape, lambda o, j: (0,) * a.ndim)
    w3 = P['c_fw3'].astype(BF16)
    consts = [z, fw1, P['c_fb1'].reshape(1, hid), P['c_freq'].reshape(1, hid), P['c_fw2'].astype(BF16),
              P['c_fb2'].reshape(1, hid)]
    return pl.pallas_call(
        functools.partial(_hyena_filter_kernel, seq=seq),
        grid=(HYENA_ORDER, nct),
        in_specs=[full(a) for a in consts] + [
            pl.BlockSpec((hid, tc), lambda o, j: (0, (o * 2 + 1) * nct + j)),
            pl.BlockSpec((hid, tc), lambda o, j: (0, (o * 2) * nct + j)),
            pl.BlockSpec((1, tc), lambda o, j: (0, j)), full(fh)],
        out_specs=pl.BlockSpec((1, nseg, 2 * HY_BLK, tc), lambda o, j: (o, 0, 0, j)),
        out_shape=jax.ShapeDtypeStruct((HYENA_ORDER, nseg, 2 * HY_BLK, c_width), F32),
        scratch_shapes=[pltpu.VMEM((2 * seq, hid), BF16), pltpu.VMEM((2 * seq, tc), F32)],
        compiler_params=_cparams(("arbitrary", "arbitrary")),
        name="hyena_spectra",
    )(*consts, w3, w3, deltas, fh)


def _hyena_conv_kernel(zin_ref, gate_ref, tz_ref, bz_ref, tg_ref, bg_ref, bias_ref, g_ref, fh_ref, ih_ref,
                       o_ref, z_scr, gate_scr, u_scr, y_scr, *, nb, conv_in):
    half = HY_BLK
    zero = jnp.zeros((1, zin_ref.shape[-1]), F32)
    z = zin_ref[0]
    if conv_in:
        z = _conv3(z, zero, zero, tz_ref[...]) + bz_ref[...]
    z_scr[...] = z
    for j in range(nb):
        u_scr[j] = _mm(fh_ref[...], z_scr[j * half:(j + 1) * half, :].astype(BF16))

    width = zin_ref.shape[-1]
    chunk = max(8, 32 * LANES // width)

    def spectra(pieces):
        for c in range(half // chunk):
            re = pl.ds(c * chunk, chunk)
            im = pl.ds(half + c * chunk, chunk)

            def add_block(j, acc):
                ur, ui = u_scr[j, re, :], u_scr[j, im, :]
                out = []
                for a, ii in enumerate(pieces):
                    m = ii - 1 - j + nb
                    gr, gi = g_ref[0, m, re, :], g_ref[0, m, im, :]
                    out.append((acc[a][0] + (ur * gr - ui * gi), acc[a][1] + (ur * gi + ui * gr)))
                return tuple(out)

            zero = jnp.zeros((chunk, width), F32)
            acc = lax.fori_loop(0, nb, add_block, tuple((zero, zero) for _ in pieces), unroll=min(4, nb))
            for a, ii in enumerate(pieces):
                y_scr[ii, re, :] = acc[a][0]
                y_scr[ii, im, :] = acc[a][1]

    def pair(p, carry):
        spectra([2 * p, 2 * p + 1])
        return carry

    lax.fori_loop(0, (nb + 1) // 2, pair, 0)
    if (nb + 1) % 2:
        spectra([nb])

    gate_scr[...] = _conv3(gate_ref[0], zero, zero, tg_ref[...]) + bg_ref[...]

    def block(i, carry):
        rows = pl.ds(pl.multiple_of(i * half, half), half)
        ycat = jnp.concatenate([y_scr[i + 1], y_scr[i]], axis=0)
        conv = _mm(ih_ref[...], ycat.astype(BF16))
        o_ref[0, rows, :] = gate_scr[rows, :] * (conv + bias_ref[...] * z_scr[rows, :])
        return carry

    lax.fori_loop(0, nb, block, 0)


def _hyena_conv(zin, zin_col0, gate_src, gate_col0, taps, tap_bias, bias, spectra, order, row0, nseq, seq, tc, conv_in):
    c_width = bias.shape[-1]
    nb = seq // HY_BLK
    nct = c_width // tc
    fh, ih = _dft_consts()
    s0 = row0 // seq
    zc, gc = zin_col0 // tc, gate_col0 // tc
    tapc = (zc if conv_in else gc)
    full = lambda a: pl.BlockSpec(a.shape, lambda j, b: (0,) * a.ndim)
    zin3 = zin.reshape(-1, seq, zin.shape[-1])
    gate3 = gate_src.reshape(-1, seq, gate_src.shape[-1])
    zs0 = s0 if conv_in else 0
    return pl.pallas_call(
        functools.partial(_hyena_conv_kernel, nb=nb, conv_in=conv_in),
        grid=(nct, nseq),
        in_specs=[pl.BlockSpec((1, seq, tc), lambda j, b: (zs0 + b, 0, zc + j)),
                  pl.BlockSpec((1, seq, tc), lambda j, b: (s0 + b, 0, gc + j)),
                  pl.BlockSpec((3, tc), lambda j, b: (0, tapc + j)),
                  pl.BlockSpec((1, tc), lambda j, b: (0, tapc + j)),
                  pl.BlockSpec((3, tc), lambda j, b: (0, gc + j)),
                  pl.BlockSpec((1, tc), lambda j, b: (0, gc + j)),
                  pl.BlockSpec((1, tc), lambda j, b: (0, j)),
                  pl.BlockSpec((1, 2 * nb, 2 * HY_BLK, tc), lambda j, b: (order, 0, 0, j)),
                  full(fh), full(ih)],
        out_specs=pl.BlockSpec((1, seq, tc), lambda j, b: (b, 0, j)),
        out_shape=jax.ShapeDtypeStruct((nseq, seq, c_width), F32),
        scratch_shapes=[pltpu.VMEM((seq, tc), F32), pltpu.VMEM((seq, tc), F32),
                        pltpu.VMEM((nb, 2 * HY_BLK, tc), F32), pltpu.VMEM((nb + 1, 2 * HY_BLK, tc), F32)],
        compiler_params=_cparams(("arbitrary", "arbitrary")),
        name="hyena_conv",
    )(zin3, gate3, taps, tap_bias, taps, tap_bias, bias, spectra, fh, ih)


def _rwkv_scan_pass(r, kk, v, w2, b2, kt2, s0, row0, bn, seq):
    w_ = r.shape[-1]
    heads = w_ // A_HEAD_DIM
    n = A_HEAD_DIM
    bh = bn * heads
    kq = max(1, LANES // bh)
    nk = n // kq
    c = kq * bh
    rows = lambda x: _to_chain(x, row0, bn, seq, kq, False)
    if s0 is None:
        s0r = jnp.zeros((2, nk, n, c), F32)
    else:
        s0r = jnp.transpose(s0.reshape(bn, 2, heads, n, kq, nk), (1, 5, 3, 4, 0, 2)).reshape(2, nk, n, c)
    yf, yb, sfin = _scan(rows(r), rows(kk), _to_chain(v, row0, bn, seq, kq, True),
                         (rows(w2[0]), rows(w2[1])), (rows(b2[0]), rows(b2[1])), (rows(kt2[0]), rows(kt2[1])),
                         s0r, kq)
    sfin = jnp.transpose(sfin.reshape(2, nk, n, kq, bn, heads), (4, 0, 5, 2, 3, 1)).reshape(bn, 2, heads, n, n)
    return _from_chain(yf, bn, heads, kq), _from_chain(yb, bn, heads, kq), sfin


def kernel(x_prompt, x_sample, cache_b_k, cache_b_v, state_a, c, c_ctx, mod_w, mod_b, norm1, norm2,
           even_w_in, even_a_conv, even_a_w0, even_a_wu, even_a_a0, even_a_au, even_a_gu, even_a_kk,
           even_a_ka, even_a_rk, even_a_ln_w, even_a_ln_b, even_b_qnorm, even_b_knorm, even_w_out,
           odd_w_in, odd_c_conv, odd_c_conv_b, odd_c_fw1, odd_c_fb1, odd_c_freq, odd_c_fw2, odd_c_fb2,
           odd_c_fw3, odd_c_bias, odd_w_out, peer_wq, peer_keys, peer_u, peer_v):
    bp, sp, d = x_prompt.shape
    bs, ss, _ = x_sample.shape
    depth = mod_w.shape[0]
    tp, ts = bp * sp, bs * ss
    t_all = tp + ts
    a_width = even_a_w0.shape[-1]
    a_cols = even_a_conv.shape[-1]
    dh = even_b_qnorm.shape[-1]
    b_width = d // 2
    kv_width = b_width // B_GROUP
    kvh = kv_width // dh
    c_width = odd_c_bias.shape[-1]
    nkeys = peer_keys.shape[3]
    assert bs + 1 <= 8 and nkeys == LANES and peer_keys.shape[1] * PEER_TOPK == LANES

    def seg_ids(tile):
        return jnp.concatenate([jnp.zeros((tp // tile,), I32),
                                1 + jnp.arange(ts // tile, dtype=I32) // (ss // tile)])

    tm = _row_tile(tp, ss, cap=512)
    segs = {tile: seg_ids(tile) for tile in (tm, 2 * tm) if tp % tile == 0 and ss % tile == 0}
    seg = segs[tm]

    cond8 = jnp.zeros((8, d), F32).at[0].set(c_ctx).at[1:1 + bs].set(c)
    mods = _mod_table(cond8, mod_w, mod_b)

    x = jnp.concatenate([x_prompt.reshape(tp, d), x_sample.reshape(ts, d)], axis=0)
    new_k, new_v, new_s = [], [], []
    for layer in range(depth):
        j = layer // 2
        mod3 = mods[layer].reshape(8 * 6, 1, d)
        if layer % 2 == 0:
            PA = dict(a_conv=even_a_conv[j], a_w0=even_a_w0[j], a_wu=even_a_wu[j], a_a0=even_a_a0[j],
                      a_au=even_a_au[j], a_gu=even_a_gu[j], a_kk=even_a_kk[j], a_ka=even_a_ka[j],
                      a_rk=even_a_rk[j], a_ln_w=even_a_ln_w[j], a_ln_b=even_a_ln_b[j])
            w_in = even_w_in[j]
            w_perm = jnp.concatenate([w_in[:, :3 * a_width], w_in[:, a_cols:], w_in[:, 3 * a_width:a_cols]],
                                     axis=1).astype(BF16)
            offs = dict(q=3 * a_width, q_w=b_width, kv=3 * a_width + b_width, kv_w=2 * kv_width,
                        low=3 * a_width + b_width + 2 * kv_width, low_w=a_cols - 3 * a_width)
            assert a_width == b_width and offs['kv'] % offs['kv_w'] == 0 and offs['low'] % offs['low_w'] == 0
            z, _ = _nm_matmul(x, norm1[layer], mod3, 1, 0, segs, w_perm)
            tmr = _row_tile(sp, ss, cap=256)
            nt = t_all // tmr
            starts = np.concatenate([np.arange(0, tp, sp), tp + np.arange(0, ts, ss), [t_all]])
            tile0 = np.arange(nt) * tmr
            keep_prev = jnp.asarray(~np.isin(tile0, starts), F32)[:, None]
            keep_next = jnp.asarray(~np.isin(tile0 + tmr, starts), F32)[:, None]
            zt = z.reshape(nt, tmr, z.shape[-1])
            zero_row = jnp.zeros((1, z.shape[-1]), F32)
            prev_rows = (jnp.concatenate([zero_row, zt[:-1, tmr - 1]], axis=0) * keep_prev)[:, None, :]
            next_rows = (jnp.concatenate([zt[1:, 0], zero_row], axis=0) * keep_next)[:, None, :]
            r, kk, vv, w0, w1, b0, b1, kt0, kt1, g, bonus = _rwkv_pre(z, prev_rows, next_rows, PA, tmr, offs)
            outs = []
            for (row0, bn, seq, latent) in ((0, bp, sp, False), (tp, bs, ss, True)):
                rows = bn * seq
                q, k, v, k_f32, v_f32 = _attn_pre(z, even_b_qnorm[j], even_b_knorm[j], offs, row0, rows, seq,
                                                  tmr, latent)
                k = k.reshape(bn, seq, kv_width)
                v = v.reshape(bn, seq, kv_width)
                if latent:
                    past = cache_b_k.shape[2]
                    k = jnp.concatenate([k, cache_b_k[:, j].astype(BF16).reshape(bn, past, kv_width)], axis=1)
                    v = jnp.concatenate([v, cache_b_v[:, j].astype(BF16).reshape(bn, past, kv_width)], axis=1)
                    s0 = state_a[:, j]
                else:
                    s0 = None
                    new_k.append(k_f32.reshape(bn, seq, kvh, dh))
                    new_v.append(v_f32.reshape(bn, seq, kvh, dh))
                y_b = _attend(q.reshape(bn, seq, b_width), k, v, dh).reshape(rows, b_width)
                yf, yb, s_fin = _rwkv_scan_pass(r, kk, vv, (w0, w1), (b0, b1), (kt0, kt1), s0, row0, bn, seq)
                if not latent:
                    new_s.append(s_fin)
                y_a = _rwkv_post(yf, yb, bonus, g, row0, PA, tmr)
                outs.append(jnp.concatenate([y_a, y_b], axis=-1))
            mix_in = jnp.concatenate(outs, axis=0)
            x = _res_matmul(mix_in, even_w_out[j].astype(BF16), x, mod3, 2, seg, tm)
        else:
            PC = dict(c_fw1=odd_c_fw1[j], c_fb1=odd_c_fb1[j], c_freq=odd_c_freq[j], c_fw2=odd_c_fw2[j],
                      c_fb2=odd_c_fb2[j], c_fw3=odd_c_fw3[j])
            u_pre, _ = _nm_matmul(x, norm1[layer], mod3, 1, 0, segs, odd_w_in[j].astype(BF16))
            taps = odd_c_conv[j]
            tap_bias = odd_c_conv_b[j].reshape(1, 3 * c_width)
            zs = []
            for (row0, bn, seq) in ((0, bp, sp), (tp, bs, ss)):
                tc = _col_tile(c_width, max(LANES, HY_TILE_ELEMS // seq))
                spectra = _hyena_spectra(seq, PC, c_width, tc)
                z1 = _hyena_conv(u_pre, 2 * c_width, u_pre, 0, taps, tap_bias, odd_c_bias[j, 0:1], spectra, 0,
                                 row0, bn, seq, tc, True)
                z2 = _hyena_conv(z1.reshape(bn * seq, c_width), 0, u_pre, c_width, taps, tap_bias,
                                 odd_c_bias[j, 1:2], spectra, 1, row0, bn, seq, tc, False)
                zs.append(z2.reshape(bn * seq, c_width))
            x = _res_matmul(jnp.concatenate(zs, axis=0).astype(BF16), odd_w_out[j].astype(BF16), x, mod3, 2, seg, tm)
        q, hm = _nm_matmul(x, norm2[layer], mod3, 4, 3, segs, peer_wq[layer].astype(BF16))
        i1, i2, gate = _peer_route(q, peer_keys[layer])
        gmat = _gate_matrix(i1, i2, gate, nkeys)
        u_bf16, v_bf16 = _to_bf16(peer_u, layer), _to_bf16(peer_v, layer)
        if layer < depth - 1:
            x = _peer_experts(hm, gmat, u_bf16, v_bf16, x, mod3, 5, seg, tm)
        else:
            y_prompt = _peer_experts(hm, gmat, u_bf16, v_bf16, x, mod3, 5, seg, tm, 0, tp).reshape(bp, sp, d)
            y_sample = _peer_experts(hm, gmat, u_bf16, v_bf16, x, mod3, 5, seg, tm, tp, ts).reshape(bs, ss, d)
    return (y_prompt, y_sample, jnp.stack(new_k, axis=1), jnp.stack(new_v, axis=1), jnp.stack(new_s, axis=1))
```

```python
import functools
import math

import numpy as np
import jax
import jax.numpy as jnp
from jax import lax
from jax.experimental import pallas as pl
from jax.experimental.pallas import tpu as pltpu

F32 = jnp.float32
BF16 = jnp.bfloat16
I32 = jnp.int32

NORM_EPS = 1e-6
A_HEAD_DIM = 64
A_GN_EPS = 64e-5
B_GROUP = 4
GRID_W = 64
ROPE_THETA = 10000.0
HYENA_ORDER = 2
DECAY_TARGET = 1e-2
FAST_DECAY_PCT = 0.3
SLOW_DECAY_PCT = 1.5
PEER_TOPK = 16
LANES = 128
VMEM_LIMIT = 56 * 1024 * 1024


def _cparams(sem):
    return pltpu.CompilerParams(dimension_semantics=sem, vmem_limit_bytes=VMEM_LIMIT)


def _row_tile(*lengths, cap=512):
    t = cap
    while any(n % t for n in lengths):
        t //= 2
    return t


def _col_tile(n, cap):
    return max(t for t in range(LANES, cap + 1, LANES) if n % t == 0)


def _mod_kernel(c_ref, w_ref, b_ref, o_ref):
    c = c_ref[...]
    s = (c * jax.nn.sigmoid(c)).astype(BF16)
    o_ref[0] = jnp.dot(s, w_ref[0].astype(BF16), preferred_element_type=F32) + b_ref[0]


def _mod_table(cond8, mod_w, mod_b):
    depth, d, n = mod_w.shape
    tn = _row_tile(n, cap=1024)
    return pl.pallas_call(
        _mod_kernel,
        grid=(depth, n // tn),
        in_specs=[pl.BlockSpec((8, d), lambda l, j: (0, 0)),
                  pl.BlockSpec((1, d, tn), lambda l, j: (l, 0, j)),
                  pl.BlockSpec((1, 1, tn), lambda l, j: (l, 0, j))],
        out_specs=pl.BlockSpec((1, 8, tn), lambda l, j: (l, 0, j)),
        out_shape=jax.ShapeDtypeStruct((depth, 8, n), F32),
        compiler_params=_cparams(("parallel", "parallel")),
        name="mod_table",
    )(cond8, mod_w, mod_b.reshape(depth, 1, n))


def _nm_matmul_kernel(seg_ref, x_ref, g_ref, sc_ref, sh_ref, w_ref, o_ref, h_ref, h_scr, *, one_col_tile):
    del seg_ref

    def modulated():
        x = x_ref[...]
        y = x * lax.rsqrt(jnp.mean(x * x, axis=-1, keepdims=True) + NORM_EPS) * g_ref[...]
        return (y * (1.0 + sc_ref[0]) + sh_ref[0]).astype(BF16)

    if one_col_tile:
        h = modulated()
        h_ref[...] = h
        o_ref[...] = jnp.dot(h, w_ref[...], preferred_element_type=F32).astype(o_ref.dtype)
        return

    @pl.when(pl.program_id(1) == 0)
    def _():
        h = modulated()
        h_scr[...] = h
        h_ref[...] = h

    o_ref[...] = jnp.dot(h_scr[...], w_ref[...], preferred_element_type=F32).astype(o_ref.dtype)


W_RESIDENT_BYTES = 8 * 1024 * 1024


def _nm_matmul(x, g, mod3, sc_idx, sh_idx, segs, w_bf16, out_dtype=F32):
    t, d = x.shape
    n = w_bf16.shape[1]
    tm = min(segs)
    if d * n * 2 <= W_RESIDENT_BYTES:
        tn = n
    elif n % 1024 == 0 and len(segs) > 1:
        tm, tn = max(segs), 1024
    else:
        tn = _col_tile(n, 1664)
    seg = segs[tm]
    grid_spec = pltpu.PrefetchScalarGridSpec(
        num_scalar_prefetch=1,
        grid=(t // tm, n // tn),
        in_specs=[pl.BlockSpec((tm, d), lambda i, j, s: (i, 0)),
                  pl.BlockSpec((1, d), lambda i, j, s: (0, 0)),
                  pl.BlockSpec((1, 1, d), lambda i, j, s: (s[i] * 6 + sc_idx, 0, 0)),
                  pl.BlockSpec((1, 1, d), lambda i, j, s: (s[i] * 6 + sh_idx, 0, 0)),
                  pl.BlockSpec((d, tn), lambda i, j, s: (0, j))],
        out_specs=[pl.BlockSpec((tm, tn), lambda i, j, s: (i, j)),
                   pl.BlockSpec((tm, d), lambda i, j, s: (i, 0))],
        scratch_shapes=[pltpu.VMEM((tm, d), BF16)])
    return pl.pallas_call(
        functools.partial(_nm_matmul_kernel, one_col_tile=(tn == n)),
        grid_spec=grid_spec,
        out_shape=[jax.ShapeDtypeStruct((t, n), out_dtype), jax.ShapeDtypeStruct((t, d), BF16)],
        compiler_params=_cparams(("parallel", "arbitrary")),
        name="norm_mod_matmul",
    )(seg, x, g.reshape(1, d), mod3, mod3, w_bf16)


def _res_matmul_kernel(seg_ref, a_ref, w_ref, r_ref, gt_ref, o_ref):
    del seg_ref
    mm = jnp.dot(a_ref[...], w_ref[...], preferred_element_type=F32)
    o_ref[...] = r_ref[...] + gt_ref[0] * mm


def _res_matmul(a_bf16, w_bf16, res, mod3, gt_idx, seg, tm):
    t, k = a_bf16.shape
    n = w_bf16.shape[1]
    tn = n if k * n * 2 <= W_RESIDENT_BYTES else _col_tile(n, 1024)
    grid_spec = pltpu.PrefetchScalarGridSpec(
        num_scalar_prefetch=1,
        grid=(t // tm, n // tn),
        in_specs=[pl.BlockSpec((tm, k), lambda i, j, s: (i, 0)),
                  pl.BlockSpec((k, tn), lambda i, j, s: (0, j)),
                  pl.BlockSpec((tm, tn), lambda i, j, s: (i, j)),
                  pl.BlockSpec((1, 1, tn), lambda i, j, s: (s[i] * 6 + gt_idx, 0, j))],
        out_specs=pl.BlockSpec((tm, tn), lambda i, j, s: (i, j)))
    return pl.pallas_call(
        _res_matmul_kernel,
        grid_spec=grid_spec,
        out_shape=jax.ShapeDtypeStruct((t, n), F32),
        compiler_params=_cparams(("parallel", "parallel")),
        name="res_matmul",
    )(seg, a_bf16, w_bf16, res, mod3)


def _scan_kernel(rf_ref, kkf_ref, vf_ref, wf_ref, bf_ref, ktf_ref, rb_ref, kkb_ref, vb_ref, wb_ref, bb_ref,
                 ktb_ref, s0_ref, yf_ref, yb_ref, sfin_ref, s_scr, *, tb_steps, nk, kq):
    ti = pl.program_id(1)

    @pl.when(ti == 0)
    def _():
        s_scr[...] = s0_ref[...]

    def tree(parts):
        while len(parts) > 1:
            parts = [parts[i] + parts[i + 1] for i in range(0, len(parts), 2)]
        return parts[0]

    def all_parts(p):
        part = LANES // kq
        return tree([p] + [pltpu.roll(p, i * part, 1) for i in range(1, kq)])

    nacc = 4

    def advance(d, t, r_ref, kk_ref, v_ref, w_ref, b_ref, kt_ref, y_ref):
        row = pl.ds(t, 1)
        accs = [None] * nacc
        for k in range(nk):
            p = s_scr[d, k] * kk_ref[k, row, :]
            accs[k % nacc] = p if accs[k % nacc] is None else accs[k % nacc] + p
        sa = all_parts(tree(accs))
        v = v_ref[t]
        yacc = [None] * nacc
        for k in range(nk):
            s_new = s_scr[d, k] * w_ref[k, row, :] - sa * b_ref[k, row, :] + v * kt_ref[k, row, :]
            s_scr[d, k] = s_new
            p = s_new * r_ref[k, row, :]
            yacc[k % nacc] = p if yacc[k % nacc] is None else yacc[k % nacc] + p
        y_ref[t] = all_parts(tree(yacc))

    def step(s, carry):
        advance(0, s, rf_ref, kkf_ref, vf_ref, wf_ref, bf_ref, ktf_ref, yf_ref)
        advance(1, tb_steps - 1 - s, rb_ref, kkb_ref, vb_ref, wb_ref, bb_ref, ktb_ref, yb_ref)
        return carry

    lax.fori_loop(0, tb_steps, step, 0, unroll=2)

    @pl.when(ti == pl.num_programs(1) - 1)
    def _():
        sfin_ref[...] = s_scr[...]


def _scan(r, kk, v, w2, b2, kt2, s0, kq):
    nk, seq, c = r.shape
    nv = v.shape[1]
    tb_steps = _row_tile(seq, cap=16)
    nt = seq // tb_steps
    rowf = pl.BlockSpec((nk, tb_steps, LANES), lambda gi, ti: (0, ti, gi))
    rowb = pl.BlockSpec((nk, tb_steps, LANES), lambda gi, ti: (0, nt - 1 - ti, gi))
    valf = pl.BlockSpec((tb_steps, nv, LANES), lambda gi, ti: (ti, 0, gi))
    valb = pl.BlockSpec((tb_steps, nv, LANES), lambda gi, ti: (nt - 1 - ti, 0, gi))
    st = pl.BlockSpec((2, nk, nv, LANES), lambda gi, ti: (0, 0, 0, gi))
    return pl.pallas_call(
        functools.partial(_scan_kernel, tb_steps=tb_steps, nk=nk, kq=kq),
        grid=(c // LANES, nt),
        in_specs=[rowf, rowf, valf, rowf, rowf, rowf, rowb, rowb, valb, rowb, rowb, rowb, st],
        out_specs=[valf, valb, st],
        out_shape=[jax.ShapeDtypeStruct((seq, nv, c), F32), jax.ShapeDtypeStruct((seq, nv, c), F32),
                   jax.ShapeDtypeStruct((2, nk, nv, c), F32)],
        scratch_shapes=[pltpu.VMEM((2, nk, nv, LANES), F32)],
        compiler_params=_cparams(("parallel", "arbitrary")),
        name="rwkv7_scan",
    )(r, kk, v, w2[0], b2[0], kt2[0], r, kk, v, w2[1], b2[1], kt2[1], s0)


CHAIN_TB = 128


def _to_chain_kernel(x_ref, o_ref, a_scr, *stage, heads, nk, kq, values):
    bg, tb = x_ref.shape[0], x_ref.shape[1]
    n = A_HEAD_DIM
    for b in range(bg):
        xt = x_ref[b].T
        a_scr[b] = jnp.swapaxes(xt.reshape(heads, n, tb), 0, 1)
    for p in range(n if values else nk):
        pieces = [a_scr[b, p if values else q * nk + p] for q in range(kq) for b in range(bg)]
        tile = jnp.concatenate(pieces, axis=0).T
        if values:
            stage[0][p] = tile
        else:
            o_ref[p] = tile
    if values:
        o_ref[...] = jnp.swapaxes(stage[0][...], 0, 1)


def _to_chain(x, row0, bn, seq, kq, values):
    w_ = x.shape[-1]
    heads = w_ // A_HEAD_DIM
    n = A_HEAD_DIM
    nk = n // kq
    bg = LANES // (kq * heads)
    ng = bn // bg
    c = kq * bn * heads
    assert kq * bg * heads == LANES and c == ng * LANES and row0 % (bg * seq) == 0
    tb = min(CHAIN_TB, seq)
    g0 = row0 // (bg * seq)
    scratch = [pltpu.VMEM((bg, n, heads, tb), F32)]
    if values:
        out_spec = pl.BlockSpec((tb, n, LANES), lambda g, t: (t, 0, g))
        out_shape = jax.ShapeDtypeStruct((seq, n, c), F32)
        scratch.append(pltpu.VMEM((n, tb, LANES), F32))
    else:
        out_spec = pl.BlockSpec((nk, tb, LANES), lambda g, t: (0, t, g))
        out_shape = jax.ShapeDtypeStruct((nk, seq, c), F32)
    return pl.pallas_call(
        functools.partial(_to_chain_kernel, heads=heads, nk=nk, kq=kq, values=values),
        grid=(ng, seq // tb),
        in_specs=[pl.BlockSpec((bg, tb, w_), lambda g, t: (g0 + g, t, 0))],
        out_specs=out_spec,
        out_shape=out_shape,
        scratch_shapes=scratch,
        compiler_params=_cparams(("parallel", "parallel")),
        name="to_chain_layout",
    )(x.reshape(-1, seq, w_))


def _from_chain_kernel(y_ref, o_ref, stage, a_scr, *, heads):
    bg = o_ref.shape[0]
    n = A_HEAD_DIM
    tb = y_ref.shape[0]
    stage[...] = jnp.swapaxes(y_ref[...], 0, 1)
    for v in range(n):
        rows = stage[v].T
        for b in range(bg):
            a_scr[b, v] = rows[b * heads:(b + 1) * heads]
    for b in range(bg):
        o_ref[b] = jnp.swapaxes(a_scr[b], 0, 1).reshape(heads * n, tb).T


def _from_chain(y, bn, heads, kq):
    seq, n, c = y.shape
    bg = LANES // (kq * heads)
    tb = min(CHAIN_TB, seq)
    out = pl.pallas_call(
        functools.partial(_from_chain_kernel, heads=heads),
        grid=(c // LANES, seq // tb),
        in_specs=[pl.BlockSpec((tb, n, LANES), lambda g, t: (t, 0, g))],
        out_specs=pl.BlockSpec((bg, tb, heads * n), lambda g, t: (g, t, 0)),
        out_shape=jax.ShapeDtypeStruct((bn, seq, heads * n), F32),
        scratch_shapes=[pltpu.VMEM((n, tb, LANES), F32), pltpu.VMEM((bg, n, heads, tb), F32)],
        compiler_params=_cparams(("parallel", "parallel")),
        name="from_chain_layout",
    )(y)
    return out.reshape(bn * seq, heads * n)


def _attn_kernel(q_ref, k_ref, v_ref, o_ref, *, dh):
    k = k_ref[0]
    v = v_ref[0]
    for g in range(B_GROUP):
        q = q_ref[0, :, g * dh:(g + 1) * dh]
        s = lax.dot_general(q, k, (((1,), (1,)), ((), ())), preferred_element_type=F32)
        m = jnp.max(s, axis=-1, keepdims=True)
        p = jnp.exp(s - m)
        den = jnp.sum(p, axis=-1, keepdims=True)
        o = jnp.dot(p.astype(BF16), v, preferred_element_type=F32) / den
        o_ref[0, :, g * dh:(g + 1) * dh] = o.astype(o_ref.dtype)


def _attend(q, k, v, dh):
    bn, lq, qw = q.shape
    lk = k.shape[1]
    kvh = k.shape[2] // dh
    gw = B_GROUP * dh
    tq = _row_tile(lq, cap=256)
    return pl.pallas_call(
        functools.partial(_attn_kernel, dh=dh),
        grid=(bn, kvh, lq // tq),
        in_specs=[pl.BlockSpec((1, tq, gw), lambda b, h, i: (b, i, h)),
                  pl.BlockSpec((1, lk, dh), lambda b, h, i: (b, 0, h)),
                  pl.BlockSpec((1, lk, dh), lambda b, h, i: (b, 0, h))],
        out_specs=pl.BlockSpec((1, tq, gw), lambda b, h, i: (b, i, h)),
        out_shape=jax.ShapeDtypeStruct((bn, lq, qw), BF16),
        compiler_params=_cparams(("parallel", "parallel", "parallel")),
        name="attention",
    )(q, k, v)


G_GROUP = 16


def _gate_matrix_kernel(i1_ref, i2_ref, g_ref, o_ref, gtmp_scr, *, nkeys):
    iota = lax.broadcasted_iota(I32, (nkeys, nkeys), 0)

    def build(grp, carry):
        base = pl.multiple_of(grp * G_GROUP, G_GROUP)
        def token(tt, c2):
            t = base + tt
            a_t = jnp.where(iota == i1_ref[pl.ds(t, 1), :], 1.0, 0.0).astype(BF16)
            b_t = jnp.where(iota == i2_ref[pl.ds(t, 1), :], g_ref[pl.ds(t, 1), :], 0.0).astype(BF16)
            gtmp_scr[tt] = lax.dot_general(a_t, b_t, (((1,), (1,)), ((), ())), preferred_element_type=F32)
            return c2

        lax.fori_loop(0, G_GROUP, token, 0, unroll=G_GROUP)
        by_n1 = jnp.swapaxes(gtmp_scr[...], 0, 1).astype(BF16)
        for n1 in range(nkeys):
            o_ref[pl.ds(base, G_GROUP), n1 * nkeys:(n1 + 1) * nkeys] = by_n1[n1]
        return carry

    lax.fori_loop(0, o_ref.shape[0] // G_GROUP, build, 0)


def _gate_matrix(i1, i2, gate, nkeys):
    t, nj = i1.shape
    tb = LANES
    sel = pl.BlockSpec((tb, nj), lambda i: (i, 0))
    return pl.pallas_call(
        functools.partial(_gate_matrix_kernel, nkeys=nkeys),
        grid=(t // tb,),
        in_specs=[sel, sel, sel],
        out_specs=pl.BlockSpec((tb, nkeys * nkeys), lambda i: (i, 0)),
        out_shape=jax.ShapeDtypeStruct((t, nkeys * nkeys), BF16),
        scratch_shapes=[pltpu.VMEM((G_GROUP, nkeys, nkeys), F32)],
        compiler_params=_cparams(("parallel",)),
        name="peer_gate_matrix",
    )(i1, i2, gate)


def _to_bf16_kernel(x_ref, o_ref):
    o_ref[...] = x_ref[0].astype(BF16)


def _to_bf16(tables, layer):
    _, e, d = tables.shape
    te = _row_tile(e, cap=1024)
    return pl.pallas_call(
        _to_bf16_kernel,
        grid=(e // te,),
        in_specs=[pl.BlockSpec((1, te, d), lambda i: (layer, i, 0))],
        out_specs=pl.BlockSpec((te, d), lambda i: (i, 0)),
        out_shape=jax.ShapeDtypeStruct((e, d), BF16),
        compiler_params=_cparams(("parallel",)),
        name="to_bf16",
    )(tables)


def _peer_kernel(seg_ref, xb_ref, gm_ref, u_ref, v_ref, r_ref, gt_ref, o_ref, acc_scr):
    del seg_ref
    e = pl.program_id(1)

    @pl.when(e == 0)
    def _():
        acc_scr[...] = jnp.zeros_like(acc_scr)

    h = lax.dot_general(xb_ref[...], u_ref[...], (((1,), (1,)), ((), ())), preferred_element_type=F32)
    act = 0.5 * h * (1.0 + lax.erf(h * (1.0 / math.sqrt(2.0)))) * gm_ref[...].astype(F32)
    acc_scr[...] += jnp.dot(act.astype(BF16), v_ref[...], preferred_element_type=F32)

    @pl.when(e == pl.num_programs(1) - 1)
    def _():
        o_ref[...] = r_ref[...] + gt_ref[0] * acc_scr[...]


def _peer_experts(xb, gmat, u_bf16, v_bf16, res, mod3, gt_idx, seg, tm, row0=0, rows=None):
    t, d = xb.shape
    rows = t if rows is None else rows
    r0 = row0 // tm
    ne = u_bf16.shape[0]
    te = 1024
    grid_spec = pltpu.PrefetchScalarGridSpec(
        num_scalar_prefetch=1,
        grid=(rows // tm, ne // te),
        in_specs=[pl.BlockSpec((tm, d), lambda i, e, s: (r0 + i, 0)),
                  pl.BlockSpec((tm, te), lambda i, e, s: (r0 + i, e)),
                  pl.BlockSpec((te, d), lambda i, e, s: (e, 0)),
                  pl.BlockSpec((te, d), lambda i, e, s: (e, 0)),
                  pl.BlockSpec((tm, d), lambda i, e, s: (r0 + i, 0)),
                  pl.BlockSpec((1, 1, d), lambda i, e, s: (s[r0 + i] * 6 + gt_idx, 0, 0))],
        out_specs=pl.BlockSpec((tm, d), lambda i, e, s: (i, 0)),
        scratch_shapes=[pltpu.VMEM((tm, d), F32)])
    return pl.pallas_call(
        _peer_kernel,
        grid_spec=grid_spec,
        out_shape=jax.ShapeDtypeStruct((rows, d), F32),
        compiler_params=_cparams(("parallel", "arbitrary")),
        name="peer_experts",
    )(seg, xb, gmat, u_bf16, v_bf16, res, mod3)


def _topk_rows(vals, payload, rows_out):
    big = jnp.float32(2 ** 30)
    top_v = jnp.zeros(rows_out.shape, F32)
    top_p = jnp.zeros(rows_out.shape, F32)
    for it in range(PEER_TOPK):
        m = jnp.max(vals, axis=0, keepdims=True)
        sel = jnp.min(jnp.where(vals == m, payload, big), axis=0, keepdims=True)
        top_v = jnp.where(rows_out == it, m, top_v)
        top_p = jnp.where(rows_out == it, sel, top_p)
        vals = jnp.where(payload == sel, -jnp.inf, vals)
    return top_v, top_p


def _top_pair_sums(s1, s2, rows_out):
    k = PEER_TOPK
    half = k // 2
    tt = s1.shape[1]
    big = jnp.float32(2 ** 30)
    lists = [s1[:half] + s2[j:j + 1] for j in range(k)]
    singles = s1[half:] + s2[0:1]
    sub = lax.broadcasted_iota(I32, (half, tt), 0).astype(F32)
    head_id = sub * k
    single_id = (sub + half) * k
    top_v = jnp.zeros(rows_out.shape, F32)
    top_p = jnp.zeros(rows_out.shape, F32)
    for it in range(k):
        m = jnp.max(jnp.maximum(lists[0], singles), axis=0, keepdims=True)
        sel = jnp.min(jnp.minimum(jnp.where(lists[0] == m, head_id, big), jnp.where(singles == m, single_id, big)),
                      axis=0, keepdims=True)
        top_v = jnp.where(rows_out == it, m, top_v)
        top_p = jnp.where(rows_out == it, sel, top_p)
        pop = head_id == sel
        lists = [jnp.where(pop, lists[j + 1], lists[j]) for j in range(k - 1)] + [jnp.where(pop, -jnp.inf, lists[-1])]
        head_id = jnp.where(pop, head_id + 1.0, head_id)
        singles = jnp.where(single_id == sel, -jnp.inf, singles)
    return top_v, top_p


def _gather_rows(table, sel):
    out = jnp.zeros(sel.shape, table.dtype)
    for i in range(PEER_TOPK):
        out = jnp.where(sel == i, table[i:i + 1, :], out)
    return out


def _route_kernel(q_ref, keys_ref, i1_ref, i2_ref, g_ref, n1_scr, n2_scr, gate_scr, *, heads, nkeys, dq):
    tt = q_ref.shape[0]
    k = PEER_TOPK
    n_iota = lax.broadcasted_iota(I32, (nkeys, tt), 0).astype(F32)
    rows_out = lax.broadcasted_iota(I32, (k, tt), 0)

    def head(h, carry):
        tops = []
        for c in range(2):
            col = pl.multiple_of((h * 2 + c) * dq, dq)
            qhc = q_ref[:, pl.ds(col, dq)].astype(BF16)
            khc = keys_ref[h, c].astype(BF16)
            s = lax.dot_general(khc, qhc, (((1,), (1,)), ((), ())), preferred_element_type=F32)
            tops.append(_topk_rows(s, n_iota, rows_out))
        (s1, i1), (s2, i2) = tops
        top, ci = _top_pair_sums(s1, s2, rows_out)
        ci = ci.astype(I32)
        n1 = _gather_rows(i1, lax.shift_right_logical(ci, 4)).astype(I32)
        n2 = _gather_rows(i2, lax.bitwise_and(ci, k - 1)).astype(I32)
        ex = jnp.exp(top - top[0:1, :])
        gate = ex / jnp.sum(ex, axis=0, keepdims=True)
        row = pl.multiple_of(h * k, k)
        n1_scr[pl.ds(row, k), :] = n1
        n2_scr[pl.ds(row, k), :] = n2
        gate_scr[pl.ds(row, k), :] = gate
        return carry

    lax.fori_loop(0, heads, head, 0)
    i1_ref[...] = n1_scr[...].T
    i2_ref[...] = n2_scr[...].T
    g_ref[...] = gate_scr[...].T


def _peer_route(q, sub_keys):
    t, qw = q.shape
    heads, _, nkeys, dq = sub_keys.shape
    hk = heads * PEER_TOPK
    tt = LANES
    out = pl.BlockSpec((tt, hk), lambda i: (i, 0))
    return pl.pallas_call(
        functools.partial(_route_kernel, heads=heads, nkeys=nkeys, dq=dq),
        grid=(t // tt,),
        in_specs=[pl.BlockSpec((tt, qw), lambda i: (i, 0)),
                  pl.BlockSpec((heads, 2, nkeys, dq), lambda i: (0, 0, 0, 0))],
        out_specs=[out, out, out],
        out_shape=[jax.ShapeDtypeStruct((t, hk), I32), jax.ShapeDtypeStruct((t, hk), I32),
                   jax.ShapeDtypeStruct((t, hk), F32)],
        scratch_shapes=[pltpu.VMEM((hk, tt), I32), pltpu.VMEM((hk, tt), I32), pltpu.VMEM((hk, tt), F32)],
        compiler_params=_cparams(("parallel",)),
        name="peer_route",
    )(q, sub_keys)


def _split(x):
    hi = x.astype(BF16)
    return hi, (x - hi.astype(F32)).astype(BF16)


def _mm(a, b):
    return jnp.dot(a, b, preferred_element_type=F32)


def _head_sum(x, ones_blockdiag):
    hi, lo = _split(x)
    return _mm(hi, ones_blockdiag) + _mm(lo, ones_blockdiag)


def _shift_rows(x, first_row, last_row):
    n = x.shape[0]
    rows = lax.broadcasted_iota(I32, x.shape, 0)
    prev = jnp.where(rows == 0, first_row, pltpu.roll(x, 1, 0))
    nxt = jnp.where(rows == n - 1, last_row, pltpu.roll(x, n - 1, 0))
    return prev, nxt


def _conv3(x, first_row, last_row, taps):
    prev, nxt = _shift_rows(x, first_row, last_row)
    return prev * taps[0:1] + x * taps[1:2] + nxt * taps[2:3]


def _rwkv_pre_kernel(r_ref, k_ref, v_ref, low_ref, pr_ref, pk_ref, pv_ref, plow_ref, nr_ref, nk_ref, nv_ref,
                     nlow_ref, cr_ref, ck_ref, cv_ref, clow_ref, w0_ref, wu_ref, a0_ref, au_ref, gu_ref,
                     kkg_ref, ka_ref, rk_ref, ones_ref,
                     ro_ref, kko_ref, vo_ref, w0o_ref, w1o_ref, b0o_ref, b1o_ref, kt0o_ref, kt1o_ref,
                     go_ref, bonus_ref, *, rw, ra):
    r = _conv3(r_ref[...], pr_ref[0], nr_ref[0], cr_ref[...])
    k = _conv3(k_ref[...], pk_ref[0], nk_ref[0], ck_ref[...])
    v = _conv3(v_ref[...], pv_ref[0], nv_ref[0], cv_ref[...])
    low = _conv3(low_ref[...], plow_ref[0], nlow_ref[0], clow_ref[...])
    ones = ones_ref[...]
    kk = k * kkg_ref[...]
    kk = kk * lax.rsqrt(_head_sum(kk * kk, ones) + 1e-12)
    ro_ref[...] = r
    kko_ref[...] = kk
    vo_ref[...] = v
    bonus_ref[...] = _head_sum(r * k * rk_ref[...], ones) * v
    gd = low[:, 2 * rw + 2 * ra:]
    go_ref[...] = _mm(jax.nn.sigmoid(gd).astype(BF16), gu_ref[...])
    for d_, (wo, bo, kto) in enumerate(((w0o_ref, b0o_ref, kt0o_ref), (w1o_ref, b1o_ref, kt1o_ref))):
        wd = low[:, d_ * rw:(d_ + 1) * rw]
        ad = low[:, 2 * rw + d_ * ra:2 * rw + (d_ + 1) * ra]
        lw = w0_ref[d_:d_ + 1, :] + _mm(jnp.tanh(wd).astype(BF16), wu_ref[d_])
        softplus = jnp.maximum(-lw, 0.0) + jnp.log1p(jnp.exp(-jnp.abs(lw)))
        wo[...] = jnp.exp(-jnp.exp(-softplus - 0.5))
        a = jax.nn.sigmoid(a0_ref[d_:d_ + 1, :] + _mm(ad.astype(BF16), au_ref[d_]))
        bo[...] = kk * a
        kto[...] = k * (1.0 + (a - 1.0) * ka_ref[...])


def _rwkv_pre(z, prev_rows, next_rows, P, tm, offs):
    t = z.shape[0]
    w_ = P['a_w0'].shape[-1]
    rw, ra = P['a_wu'].shape[1], P['a_au'].shape[1]
    lw = offs['low_w']
    li = offs['low'] // lw
    row = lambda c, wd: pl.BlockSpec((tm, wd), lambda i: (i, c))
    edge = lambda c, wd: pl.BlockSpec((1, 1, wd), lambda i: (i, 0, c))
    full = lambda a: pl.BlockSpec(a.shape, lambda i: (0,) * a.ndim)
    conv = P['a_conv']
    consts = [conv[:, :w_], conv[:, w_:2 * w_], conv[:, 2 * w_:3 * w_], conv[:, 3 * w_:],
              P['a_w0'], P['a_wu'].astype(BF16), P['a_a0'], P['a_au'].astype(BF16), P['a_gu'].astype(BF16),
              P['a_kk'].reshape(1, w_), P['a_ka'].reshape(1, w_), P['a_rk'].reshape(1, w_),
              jnp.asarray(_blockdiag_ones(w_), BF16)]
    out = pl.BlockSpec((tm, w_), lambda i: (i, 0))
    return pl.pallas_call(
        functools.partial(_rwkv_pre_kernel, rw=rw, ra=ra),
        grid=(t // tm,),
        in_specs=[row(0, w_), row(1, w_), row(2, w_), row(li, lw),
                  edge(0, w_), edge(1, w_), edge(2, w_), edge(li, lw),
                  edge(0, w_), edge(1, w_), edge(2, w_), edge(li, lw)] + [full(a) for a in consts],
        out_specs=[out] * 11,
        out_shape=[jax.ShapeDtypeStruct((t, w_), F32)] * 11,
        compiler_params=_cparams(("parallel",)),
        name="rwkv7_pre",
    )(z, z, z, z, prev_rows, prev_rows, prev_rows, prev_rows, next_rows, next_rows, next_rows, next_rows, *consts)


def _blockdiag_ones(width):
    idx = np.arange(width) // A_HEAD_DIM
    return (idx[:, None] == idx[None, :]).astype(np.float32)


def _rwkv_post_kernel(yf_ref, yb_ref, bonus_ref, g_ref, lnw_ref, lnb_ref, ones_ref, o_ref):
    ones = ones_ref[...]
    y = yf_ref[...] + yb_ref[...]
    mu = _head_sum(y, ones) * (1.0 / A_HEAD_DIM)
    dlt = y - mu
    var = _head_sum(dlt * dlt, ones) * (1.0 / A_HEAD_DIM)
    yn = dlt * lax.rsqrt(var + A_GN_EPS) * lnw_ref[...] + lnb_ref[...]
    o_ref[...] = ((yn + bonus_ref[...]) * g_ref[...]).astype(o_ref.dtype)


def _rwkv_post(yf, yb, bonus, g, row0, P, tm):
    t, w_ = yf.shape
    r0 = row0 // tm
    row = pl.BlockSpec((tm, w_), lambda i: (i, 0))
    off = pl.BlockSpec((tm, w_), lambda i: (r0 + i, 0))
    vec = pl.BlockSpec((1, w_), lambda i: (0, 0))
    return pl.pallas_call(
        _rwkv_post_kernel,
        grid=(t // tm,),
        in_specs=[row, row, off, off, vec, vec, pl.BlockSpec((w_, w_), lambda i: (0, 0))],
        out_specs=row,
        out_shape=jax.ShapeDtypeStruct((t, w_), BF16),
        compiler_params=_cparams(("parallel",)),
        name="rwkv7_post",
    )(yf, yb, bonus, g, P['a_ln_w'].reshape(1, w_), P['a_ln_b'].reshape(1, w_),
      jnp.asarray(_blockdiag_ones(w_), BF16))


def _rwkv_post_chain_kernel(yf_ref, yb_ref, bonus_ref, g_ref, lnw_ref, lnb_ref, ones_ref, o_ref, stage, a_scr,
                            *, heads):
    bg = o_ref.shape[0]
    n = A_HEAD_DIM
    tb = yf_ref.shape[0]
    stage[...] = jnp.swapaxes(yf_ref[...] + yb_ref[...], 0, 1)
    for v in range(n):
        rows = stage[v].T
        for b in range(bg):
            a_scr[b, v] = rows[b * heads:(b + 1) * heads]
    ones = ones_ref[...]
    w_ = heads * n
    y = jnp.concatenate([jnp.swapaxes(a_scr[b], 0, 1).reshape(w_, tb).T for b in range(bg)], axis=0)
    mu = _head_sum(y, ones) * (1.0 / n)
    dlt = y - mu
    var = _head_sum(dlt * dlt, ones) * (1.0 / n)
    yn = dlt * lax.rsqrt(var + A_GN_EPS) * lnw_ref[...] + lnb_ref[...]
    out = (yn + bonus_ref[...].reshape(bg * tb, w_)) * g_ref[...].reshape(bg * tb, w_)
    o_ref[...] = out.astype(o_ref.dtype).reshape(bg, tb, w_)


POST_TB = 64


def _rwkv_post_chain(yf, yb, bonus, g, row0, bn, heads, kq, P):
    seq, n, c = yf.shape
    w_ = heads * n
    bg = LANES // (kq * heads)
    tb = min(POST_TB, seq)
    g0 = row0 // (bg * seq)
    chain = pl.BlockSpec((tb, n, LANES), lambda gi, t: (t, 0, gi))
    tok = pl.BlockSpec((bg, tb, w_), lambda gi, t: (g0 + gi, t, 0))
    vec = pl.BlockSpec((1, w_), lambda gi, t: (0, 0))
    out = pl.pallas_call(
        functools.partial(_rwkv_post_chain_kernel, heads=heads),
        grid=(c // LANES, seq // tb),
        in_specs=[chain, chain, tok, tok, vec, vec, pl.BlockSpec((w_, w_), lambda gi, t: (0, 0))],
        out_specs=pl.BlockSpec((bg, tb, w_), lambda gi, t: (gi, t, 0)),
        out_shape=jax.ShapeDtypeStruct((bn, seq, w_), BF16),
        scratch_shapes=[pltpu.VMEM((n, tb, LANES), F32), pltpu.VMEM((bg, n, heads, tb), F32)],
        compiler_params=_cparams(("parallel", "parallel")),
        name="rwkv7_post_chain",
    )(yf, yb, bonus.reshape(-1, seq, w_), g.reshape(-1, seq, w_), P['a_ln_w'].reshape(1, w_),
      P['a_ln_b'].reshape(1, w_), jnp.asarray(_blockdiag_ones(w_), BF16))
    return out.reshape(bn * seq, w_)


def _attn_pre_kernel(q_ref, kv_ref, qn_ref, kn_ref, cos_ref, sin_ref, qo_ref, ko_ref, vo_ref, kf_ref, vf_ref,
                     *, dh, latent):
    def rms(x, g):
        return x * lax.rsqrt(jnp.mean(x * x, axis=-1, keepdims=True) + NORM_EPS) * g

    def rope(x):
        if not latent:
            return x
        lanes = lax.broadcasted_iota(I32, x.shape, 1)
        quarter = dh // 4
        partner = jnp.where(lanes % (2 * quarter) < quarter, pltpu.roll(x, dh - quarter, 1), pltpu.roll(x, quarter, 1))
        return x * cos_ref[...] + partner * sin_ref[...]

    nq = q_ref.shape[1] // dh
    nkv = kv_ref.shape[1] // (2 * dh)
    for h in range(nq):
        q = rope(rms(q_ref[:, h * dh:(h + 1) * dh], qn_ref[...]))
        qo_ref[:, h * dh:(h + 1) * dh] = (q * dh ** -0.5).astype(BF16)
    for h in range(nkv):
        k = rms(kv_ref[:, h * dh:(h + 1) * dh], kn_ref[...])
        v = kv_ref[:, (nkv + h) * dh:(nkv + h + 1) * dh]
        kf_ref[:, h * dh:(h + 1) * dh] = k
        vf_ref[:, h * dh:(h + 1) * dh] = v
        ko_ref[:, h * dh:(h + 1) * dh] = rope(k).astype(BF16)
        vo_ref[:, h * dh:(h + 1) * dh] = v.astype(BF16)


def _rope_tables(seq, dh):
    quarter = dh // 4
    inv = ROPE_THETA ** (-np.arange(quarter, dtype=np.float64) / quarter)
    pos = np.arange(seq)
    ang_r = (pos // GRID_W)[:, None] * inv[None, :]
    ang_c = (pos % GRID_W)[:, None] * inv[None, :]
    cos = np.concatenate([np.cos(ang_r)] * 2 + [np.cos(ang_c)] * 2, axis=1)
    sin = np.concatenate([-np.sin(ang_r), np.sin(ang_r), -np.sin(ang_c), np.sin(ang_c)], axis=1)
    return jnp.asarray(cos, F32), jnp.asarray(sin, F32)


def _attn_pre(z, qnorm, knorm, offs, row0, rows, seq, tm, latent):
    dh = qnorm.shape[-1]
    bw, kvw2 = offs['q_w'], offs['kv_w']
    r0 = row0 // tm
    per_seq = seq // tm
    cos, sin = _rope_tables(seq, dh) if latent else (jnp.zeros((tm, dh), F32), jnp.zeros((tm, dh), F32))
    tab = pl.BlockSpec((tm, dh), (lambda i: (i % per_seq, 0)) if latent else (lambda i: (0, 0)))
    vec = pl.BlockSpec((1, dh), lambda i: (0, 0))
    kvo = pl.BlockSpec((tm, kvw2 // 2), lambda i: (i, 0))
    return pl.pallas_call(
        functools.partial(_attn_pre_kernel, dh=dh, latent=latent),
        grid=(rows // tm,),
        in_specs=[pl.BlockSpec((tm, bw), lambda i: (r0 + i, offs['q'] // bw)),
                  pl.BlockSpec((tm, kvw2), lambda i: (r0 + i, offs['kv'] // kvw2)),
                  vec, vec, tab, tab],
        out_specs=[pl.BlockSpec((tm, bw), lambda i: (i, 0)), kvo, kvo, kvo, kvo],
        out_shape=[jax.ShapeDtypeStruct((rows, bw), BF16), jax.ShapeDtypeStruct((rows, kvw2 // 2), BF16),
                   jax.ShapeDtypeStruct((rows, kvw2 // 2), BF16), jax.ShapeDtypeStruct((rows, kvw2 // 2), F32),
                   jax.ShapeDtypeStruct((rows, kvw2 // 2), F32)],
        compiler_params=_cparams(("parallel",)),
        name="attention_pre",
    )(z, z, qnorm.reshape(1, dh), knorm.reshape(1, dh), cos, sin)


HY_BLK = 256
HY_TILE_ELEMS = 128 * 1024


def _dft_consts():
    n = 2 * HY_BLK
    k = np.arange(HY_BLK, dtype=np.float64)[:, None] + 0.5
    s = np.arange(HY_BLK, dtype=np.float64)[None, :]
    th = 2.0 * np.pi * k * s / n
    fwd = np.concatenate([np.cos(th), -np.sin(th)], axis=0)
    tau = np.arange(n, dtype=np.float64)[:, None]
    ph = 2.0 * np.pi * tau * (np.arange(HY_BLK, dtype=np.float64)[None, :] + 0.5) / n
    inv = np.concatenate([np.cos(ph), -np.sin(ph)], axis=1) * (2.0 / n)
    inv_cat = np.concatenate([inv[:HY_BLK], inv[HY_BLK:]], axis=1)

    return jnp.asarray(fwd, F32).astype(BF16), jnp.asarray(inv_cat, F32).astype(BF16)


def _lag_features(seq, emb):
    bands = (emb - 1) // 2
    t = np.linspace(0.0, 1.0, seq)
    wpos = 2.0 * np.pi * np.arange(seq) / seq
    f = np.linspace(1e-4, bands - 1, bands)
    z = np.concatenate([t[:, None], np.cos(f[None, :] * wpos[:, None]), -np.sin(f[None, :] * wpos[:, None])], axis=1)
    lag = np.concatenate([np.zeros(1, np.int64), np.arange(seq - 1, 0, -1), np.arange(seq)])
    return jnp.asarray(z[lag], F32)


def _hyena_filter_kernel(z_ref, fw1_ref, fb1_ref, freq_ref, fw2_ref, fb2_ref, w3b_ref, w3f_ref, dl_ref,
                         fh_ref, g_ref, hdn_scr, f_scr, *, seq):
    @pl.when((pl.program_id(0) == 0) & (pl.program_id(1) == 0))
    def _():
        h1 = jnp.sin(freq_ref[...] * (_mm(z_ref[...].astype(BF16), fw1_ref[...]) + fb1_ref[...]))
        hdn_scr[...] = jnp.sin(freq_ref[...] * (_mm(h1.astype(BF16), fw2_ref[...]) + fb2_ref[...])).astype(BF16)

    decay = jnp.exp(-z_ref[:, 0:1] * dl_ref[...])
    f_scr[0:seq, :] = _mm(hdn_scr[0:seq, :], w3b_ref[...]) * decay[0:seq, :]
    f_scr[seq:2 * seq, :] = _mm(hdn_scr[seq:2 * seq, :], w3f_ref[...]) * decay[seq:2 * seq, :]
    f = f_scr[...]
    scale = lax.rsqrt(jnp.sum(f * f, axis=0, keepdims=True) + 1e-12)
    rows = lax.broadcasted_iota(I32, f.shape, 0)
    f_scr[...] = jnp.where(rows == 0, 0.0, f * scale)
    for m in range(2 * seq // HY_BLK):
        g_ref[0, m] = _mm(fh_ref[...], f_scr[m * HY_BLK:(m + 1) * HY_BLK, :].astype(BF16))


def _hyena_spectra(seq, P, c_width, tc):
    emb, hid = P['c_fw1'].shape
    z = _lag_features(seq, emb)
    embp = 64
    z = jnp.pad(z, ((0, 0), (0, embp - emb)))
    fw1 = jnp.pad(P['c_fw1'], ((0, embp - emb), (0, 0))).astype(BF16)
    deltas = jnp.asarray(np.abs(np.linspace(math.log(DECAY_TARGET) / SLOW_DECAY_PCT,
                                            math.log(DECAY_TARGET) / FAST_DECAY_PCT, c_width)), F32).reshape(1, c_width)
    fh, _ = _dft_consts()
    nct = c_width // tc
    nseg = 2 * seq // HY_BLK
    full = lambda a: pl.BlockSpec(a.shape, lambda o, j: (0,) * a.ndim)
    w3 = P['c_fw3'].astype(BF16)
    consts = [z, fw1, P['c_fb1'].reshape(1, hid), P['c_freq'].reshape(1, hid), P['c_fw2'].astype(BF16),
              P['c_fb2'].reshape(1, hid)]
    return pl.pallas_call(
        functools.partial(_hyena_filter_kernel, seq=seq),
        grid=(HYENA_ORDER, nct),
        in_specs=[full(a) for a in consts] + [
            pl.BlockSpec((hid, tc), lambda o, j: (0, (o * 2 + 1) * nct + j)),
            pl.BlockSpec((hid, tc), lambda o, j: (0, (o * 2) * nct + j)),
            pl.BlockSpec((1, tc), lambda o, j: (0, j)), full(fh)],
        out_specs=pl.BlockSpec((1, nseg, 2 * HY_BLK, tc), lambda o, j: (o, 0, 0, j)),
        out_shape=jax.ShapeDtypeStruct((HYENA_ORDER, nseg, 2 * HY_BLK, c_width), F32),
        scratch_shapes=[pltpu.VMEM((2 * seq, hid), BF16), pltpu.VMEM((2 * seq, tc), F32)],
        compiler_params=_cparams(("arbitrary", "arbitrary")),
        name="hyena_spectra",
    )(*consts, w3, w3, deltas, fh)


def _hyena_conv_kernel(zin_ref, gate_ref, tz_ref, bz_ref, tg_ref, bg_ref, bias_ref, g_ref, fh_ref, ih_ref,
                       o_ref, z_scr, gate_scr, u_scr, y_scr, *, nb, conv_in):
    half = HY_BLK
    zero = jnp.zeros((1, zin_ref.shape[-1]), F32)
    z = zin_ref[0]
    if conv_in:
        z = _conv3(z, zero, zero, tz_ref[...]) + bz_ref[...]
    z_scr[...] = z
    for j in range(nb):
        u_scr[j] = _mm(fh_ref[...], z_scr[j * half:(j + 1) * half, :].astype(BF16))

    width = zin_ref.shape[-1]
    chunk = max(8, 32 * LANES // width)

    def spectra(pieces):
        for c in range(half // chunk):
            re = pl.ds(c * chunk, chunk)
            im = pl.ds(half + c * chunk, chunk)

            def add_block(j, acc):
                ur, ui = u_scr[j, re, :], u_scr[j, im, :]
                out = []
                for a, ii in enumerate(pieces):
                    m = ii - 1 - j + nb
                    gr, gi = g_ref[0, m, re, :], g_ref[0, m, im, :]
                    out.append((acc[a][0] + (ur * gr - ui * gi), acc[a][1] + (ur * gi + ui * gr)))
                return tuple(out)

            zero = jnp.zeros((chunk, width), F32)
            acc = lax.fori_loop(0, nb, add_block, tuple((zero, zero) for _ in pieces), unroll=min(4, nb))
            for a, ii in enumerate(pieces):
                y_scr[ii, re, :] = acc[a][0]
                y_scr[ii, im, :] = acc[a][1]

    def pair(p, carry):
        spectra([2 * p, 2 * p + 1])
        return carry

    lax.fori_loop(0, (nb + 1) // 2, pair, 0)
    if (nb + 1) % 2:
        spectra([nb])

    gate_scr[...] = _conv3(gate_ref[0], zero, zero, tg_ref[...]) + bg_ref[...]

    def block(i, carry):
        rows = pl.ds(pl.multiple_of(i * half, half), half)
        ycat = jnp.concatenate([y_scr[i + 1], y_scr[i]], axis=0)
        conv = _mm(ih_ref[...], ycat.astype(BF16))
        o_ref[0, rows, :] = gate_scr[rows, :] * (conv + bias_ref[...] * z_scr[rows, :])
        return carry

    lax.fori_loop(0, nb, block, 0)


def _hyena_conv(zin, zin_col0, gate_src, gate_col0, taps, tap_bias, bias, spectra, order, row0, nseq, seq, tc, conv_in):
    c_width = bias.shape[-1]
    nb = seq // HY_BLK
    nct = c_width // tc
    fh, ih = _dft_consts()
    s0 = row0 // seq
    zc, gc = zin_col0 // tc, gate_col0 // tc
    tapc = (zc if conv_in else gc)
    full = lambda a: pl.BlockSpec(a.shape, lambda j, b: (0,) * a.ndim)
    zin3 = zin.reshape(-1, seq, zin.shape[-1])
    gate3 = gate_src.reshape(-1, seq, gate_src.shape[-1])
    zs0 = s0 if conv_in else 0
    return pl.pallas_call(
        functools.partial(_hyena_conv_kernel, nb=nb, conv_in=conv_in),
        grid=(nct, nseq),
        in_specs=[pl.BlockSpec((1, seq, tc), lambda j, b: (zs0 + b, 0, zc + j)),
                  pl.BlockSpec((1, seq, tc), lambda j, b: (s0 + b, 0, gc + j)),
                  pl.BlockSpec((3, tc), lambda j, b: (0, tapc + j)),
                  pl.BlockSpec((1, tc), lambda j, b: (0, tapc + j)),
                  pl.BlockSpec((3, tc), lambda j, b: (0, gc + j)),
                  pl.BlockSpec((1, tc), lambda j, b: (0, gc + j)),
                  pl.BlockSpec((1, tc), lambda j, b: (0, j)),
                  pl.BlockSpec((1, 2 * nb, 2 * HY_BLK, tc), lambda j, b: (order, 0, 0, j)),
                  full(fh), full(ih)],
        out_specs=pl.BlockSpec((1, seq, tc), lambda j, b: (b, 0, j)),
        out_shape=jax.ShapeDtypeStruct((nseq, seq, c_width), F32),
        scratch_shapes=[pltpu.VMEM((seq, tc), F32), pltpu.VMEM((seq, tc), F32),
                        pltpu.VMEM((nb, 2 * HY_BLK, tc), F32), pltpu.VMEM((nb + 1, 2 * HY_BLK, tc), F32)],
        compiler_params=_cparams(("arbitrary", "arbitrary")),
        name="hyena_conv",
    )(zin3, gate3, taps, tap_bias, taps, tap_bias, bias, spectra, fh, ih)


def _rwkv_scan_pass(r, kk, v, w2, b2, kt2, s0, row0, bn, seq):
    w_ = r.shape[-1]
    heads = w_ // A_HEAD_DIM
    n = A_HEAD_DIM
    bh = bn * heads
    kq = max(1, LANES // bh)
    nk = n // kq
    c = kq * bh
    rows = lambda x: _to_chain(x, row0, bn, seq, kq, False)
    if s0 is None:
        s0r = jnp.zeros((2, nk, n, c), F32)
    else:
        s0r = jnp.transpose(s0.reshape(bn, 2, heads, n, kq, nk), (1, 5, 3, 4, 0, 2)).reshape(2, nk, n, c)
    yf, yb, sfin = _scan(rows(r), rows(kk), _to_chain(v, row0, bn, seq, kq, True),
                         (rows(w2[0]), rows(w2[1])), (rows(b2[0]), rows(b2[1])), (rows(kt2[0]), rows(kt2[1])),
                         s0r, kq)
    sfin = jnp.transpose(sfin.reshape(2, nk, n, kq, bn, heads), (4, 0, 5, 2, 3, 1)).reshape(bn, 2, heads, n, n)
    return yf, yb, sfin, kq


def kernel(x_prompt, x_sample, cache_b_k, cache_b_v, state_a, c, c_ctx, mod_w, mod_b, norm1, norm2,
           even_w_in, even_a_conv, even_a_w0, even_a_wu, even_a_a0, even_a_au, even_a_gu, even_a_kk,
           even_a_ka, even_a_rk, even_a_ln_w, even_a_ln_b, even_b_qnorm, even_b_knorm, even_w_out,
           odd_w_in, odd_c_conv, odd_c_conv_b, odd_c_fw1, odd_c_fb1, odd_c_freq, odd_c_fw2, odd_c_fb2,
           odd_c_fw3, odd_c_bias, odd_w_out, peer_wq, peer_keys, peer_u, peer_v):
    bp, sp, d = x_prompt.shape
    bs, ss, _ = x_sample.shape
    depth = mod_w.shape[0]
    tp, ts = bp * sp, bs * ss
    t_all = tp + ts
    a_width = even_a_w0.shape[-1]
    a_cols = even_a_conv.shape[-1]
    dh = even_b_qnorm.shape[-1]
    b_width = d // 2
    kv_width = b_width // B_GROUP
    kvh = kv_width // dh
    c_width = odd_c_bias.shape[-1]
    nkeys = peer_keys.shape[3]
    assert bs + 1 <= 8 and nkeys == LANES and peer_keys.shape[1] * PEER_TOPK == LANES

    def seg_ids(tile):
        return jnp.concatenate([jnp.zeros((tp // tile,), I32),
                                1 + jnp.arange(ts // tile, dtype=I32) // (ss // tile)])

    tm = _row_tile(tp, ss, cap=512)
    segs = {tile: seg_ids(tile) for tile in (tm, 2 * tm) if tp % tile == 0 and ss % tile == 0}
    seg = segs[tm]

    cond8 = jnp.zeros((8, d), F32).at[0].set(c_ctx).at[1:1 + bs].set(c)
    mods = _mod_table(cond8, mod_w, mod_b)

    x = jnp.concatenate([x_prompt.reshape(tp, d), x_sample.reshape(ts, d)], axis=0)
    new_k, new_v, new_s = [], [], []
    for layer in range(depth):
        j = layer // 2
        mod3 = mods[layer].reshape(8 * 6, 1, d)
        if layer % 2 == 0:
            PA = dict(a_conv=even_a_conv[j], a_w0=even_a_w0[j], a_wu=even_a_wu[j], a_a0=even_a_a0[j],
                      a_au=even_a_au[j], a_gu=even_a_gu[j], a_kk=even_a_kk[j], a_ka=even_a_ka[j],
                      a_rk=even_a_rk[j], a_ln_w=even_a_ln_w[j], a_ln_b=even_a_ln_b[j])
            w_in = even_w_in[j]
            w_perm = jnp.concatenate([w_in[:, :3 * a_width], w_in[:, a_cols:], w_in[:, 3 * a_width:a_cols]],
                                     axis=1).astype(BF16)
            offs = dict(q=3 * a_width, q_w=b_width, kv=3 * a_width + b_width, kv_w=2 * kv_width,
                        low=3 * a_width + b_width + 2 * kv_width, low_w=a_cols - 3 * a_width)
            assert a_width == b_width and offs['kv'] % offs['kv_w'] == 0 and offs['low'] % offs['low_w'] == 0
            z, _ = _nm_matmul(x, norm1[layer], mod3, 1, 0, segs, w_perm)
            tmr = _row_tile(sp, ss, cap=256)
            nt = t_all // tmr
            starts = np.concatenate([np.arange(0, tp, sp), tp + np.arange(0, ts, ss), [t_all]])
            tile0 = np.arange(nt) * tmr
            keep_prev = jnp.asarray(~np.isin(tile0, starts), F32)[:, None]
            keep_next = jnp.asarray(~np.isin(tile0 + tmr, starts), F32)[:, None]
            zt = z.reshape(nt, tmr, z.shape[-1])
            zero_row = jnp.zeros((1, z.shape[-1]), F32)
            prev_rows = (jnp.concatenate([zero_row, zt[:-1, tmr - 1]], axis=0) * keep_prev)[:, None, :]
            next_rows = (jnp.concatenate([zt[1:, 0], zero_row], axis=0) * keep_next)[:, None, :]
            r, kk, vv, w0, w1, b0, b1, kt0, kt1, g, bonus = _rwkv_pre(z, prev_rows, next_rows, PA, tmr, offs)
            outs = []
            for (row0, bn, seq, latent) in ((0, bp, sp, False), (tp, bs, ss, True)):
                rows = bn * seq
                q, k, v, k_f32, v_f32 = _attn_pre(z, even_b_qnorm[j], even_b_knorm[j], offs, row0, rows, seq,
                                                  tmr, latent)
                k = k.reshape(bn, seq, kv_width)
                v = v.reshape(bn, seq, kv_width)
                if latent:
                    past = cache_b_k.shape[2]
                    k = jnp.concatenate([k, cache_b_k[:, j].astype(BF16).reshape(bn, past, kv_width)], axis=1)
                    v = jnp.concatenate([v, cache_b_v[:, j].astype(BF16).reshape(bn, past, kv_width)], axis=1)
                    s0 = state_a[:, j]
                else:
                    s0 = None
                    new_k.append(k_f32.reshape(bn, seq, kvh, dh))
                    new_v.append(v_f32.reshape(bn, seq, kvh, dh))
                y_b = _attend(q.reshape(bn, seq, b_width), k, v, dh).reshape(rows, b_width)
                yf, yb, s_fin, kq = _rwkv_scan_pass(r, kk, vv, (w0, w1), (b0, b1), (kt0, kt1), s0, row0, bn, seq)
                if not latent:
                    new_s.append(s_fin)
                y_a = _rwkv_post_chain(yf, yb, bonus, g, row0, bn, a_width // A_HEAD_DIM, kq, PA)
                outs.append(jnp.concatenate([y_a, y_b], axis=-1))
            mix_in = jnp.concatenate(outs, axis=0)
            x = _res_matmul(mix_in, even_w_out[j].astype(BF16), x, mod3, 2, seg, tm)
        else:
            PC = dict(c_fw1=odd_c_fw1[j], c_fb1=odd_c_fb1[j], c_freq=odd_c_freq[j], c_fw2=odd_c_fw2[j],
                      c_fb2=odd_c_fb2[j], c_fw3=odd_c_fw3[j])
            u_pre, _ = _nm_matmul(x, norm1[layer], mod3, 1, 0, segs, odd_w_in[j].astype(BF16))
            taps = odd_c_conv[j]
            tap_bias = odd_c_conv_b[j].reshape(1, 3 * c_width)
            zs = []
            for (row0, bn, seq) in ((0, bp, sp), (tp, bs, ss)):
                tc = _col_tile(c_width, max(LANES, HY_TILE_ELEMS // seq))
                spectra = _hyena_spectra(seq, PC, c_width, tc)
                z1 = _hyena_conv(u_pre, 2 * c_width, u_pre, 0, taps, tap_bias, odd_c_bias[j, 0:1], spectra, 0,
                                 row0, bn, seq, tc, True)
                z2 = _hyena_conv(z1.reshape(bn * seq, c_width), 0, u_pre, c_width, taps, tap_bias,
                                 odd_c_bias[j, 1:2], spectra, 1, row0, bn, seq, tc, False)
                zs.append(z2.reshape(bn * seq, c_width))
            x = _res_matmul(jnp.concatenate(zs, axis=0).astype(BF16), odd_w_out[j].astype(BF16), x, mod3, 2, seg, tm)
        q, hm = _nm_matmul(x, norm2[layer], mod3, 4, 3, segs, peer_wq[layer].astype(BF16))
        i1, i2, gate = _peer_route(q, peer_keys[layer])
        gmat = _gate_matrix(i1, i2, gate, nkeys)
        u_bf16, v_bf16 = _to_bf16(peer_u, layer), _to_bf16(peer_v, layer)
        if layer < depth - 1:
            x = _peer_experts(hm, gmat, u_bf16, v_bf16, x, mod3, 5, seg, tm)
        else:
            y_prompt = _peer_experts(hm, gmat, u_bf16, v_bf16, x, mod3, 5, seg, tm, 0, tp).reshape(bp, sp, d)
            y_sample = _peer_experts(hm, gmat, u_bf16, v_bf16, x, mod3, 5, seg, tm, tp, ts).reshape(bs, ss, d)
    return (y_prompt, y_sample, jnp.stack(new_k, axis=1), jnp.stack(new_v, axis=1), jnp.stack(new_s, axis=1))
```

```python
import functools
import math

import numpy as np
import jax
import jax.numpy as jnp
from jax import lax
from jax.experimental import pallas as pl
from jax.experimental.pallas import tpu as pltpu

F32 = jnp.float32
BF16 = jnp.bfloat16
I32 = jnp.int32

NORM_EPS = 1e-6
A_HEAD_DIM = 64
A_GN_EPS = 64e-5
B_GROUP = 4
GRID_W = 64
ROPE_THETA = 10000.0
HYENA_ORDER = 2
DECAY_TARGET = 1e-2
FAST_DECAY_PCT = 0.3
SLOW_DECAY_PCT = 1.5
PEER_TOPK = 16
LANES = 128
VMEM_LIMIT = 56 * 1024 * 1024


def _cparams(sem):
    return pltpu.CompilerParams(dimension_semantics=sem, vmem_limit_bytes=VMEM_LIMIT)


def _row_tile(*lengths, cap=512):
    t = cap
    while any(n % t for n in lengths):
        t //= 2
    return t


def _col_tile(n, cap):
    return max(t for t in range(LANES, cap + 1, LANES) if n % t == 0)


def _mod_kernel(c_ref, w_ref, b_ref, o_ref):
    c = c_ref[...]
    s = (c * jax.nn.sigmoid(c)).astype(BF16)
    o_ref[0] = jnp.dot(s, w_ref[0].astype(BF16), preferred_element_type=F32) + b_ref[0]


def _mod_table(cond8, mod_w, mod_b):
    depth, d, n = mod_w.shape
    tn = _row_tile(n, cap=1024)
    return pl.pallas_call(
        _mod_kernel,
        grid=(depth, n // tn),
        in_specs=[pl.BlockSpec((8, d), lambda l, j: (0, 0)),
                  pl.BlockSpec((1, d, tn), lambda l, j: (l, 0, j)),
                  pl.BlockSpec((1, 1, tn), lambda l, j: (l, 0, j))],
        out_specs=pl.BlockSpec((1, 8, tn), lambda l, j: (l, 0, j)),
        out_shape=jax.ShapeDtypeStruct((depth, 8, n), F32),
        compiler_params=_cparams(("parallel", "parallel")),
        name="mod_table",
    )(cond8, mod_w, mod_b.reshape(depth, 1, n))


def _nm_matmul_kernel(seg_ref, x_ref, g_ref, sc_ref, sh_ref, w_ref, o_ref, h_ref, h_scr, *, one_col_tile):
    del seg_ref

    def modulated():
        x = x_ref[...]
        y = x * lax.rsqrt(jnp.mean(x * x, axis=-1, keepdims=True) + NORM_EPS) * g_ref[...]
        return (y * (1.0 + sc_ref[0]) + sh_ref[0]).astype(BF16)

    if one_col_tile:
        h = modulated()
        h_ref[...] = h
        o_ref[...] = jnp.dot(h, w_ref[...], preferred_element_type=F32).astype(o_ref.dtype)
        return

    @pl.when(pl.program_id(1) == 0)
    def _():
        h = modulated()
        h_scr[...] = h
        h_ref[...] = h

    o_ref[...] = jnp.dot(h_scr[...], w_ref[...], preferred_element_type=F32).astype(o_ref.dtype)


W_RESIDENT_BYTES = 8 * 1024 * 1024


def _nm_matmul(x, g, mod3, sc_idx, sh_idx, segs, w_bf16, out_dtype=F32):
    t, d = x.shape
    n = w_bf16.shape[1]
    tm = min(segs)
    if d * n * 2 <= W_RESIDENT_BYTES:
        tn = n
    elif n % 1024 == 0 and len(segs) > 1:
        tm, tn = max(segs), 1024
    else:
        tn = _col_tile(n, 1664)
    seg = segs[tm]
    grid_spec = pltpu.PrefetchScalarGridSpec(
        num_scalar_prefetch=1,
        grid=(t // tm, n // tn),
        in_specs=[pl.BlockSpec((tm, d), lambda i, j, s: (i, 0)),
                  pl.BlockSpec((1, d), lambda i, j, s: (0, 0)),
                  pl.BlockSpec((1, 1, d), lambda i, j, s: (s[i] * 6 + sc_idx, 0, 0)),
                  pl.BlockSpec((1, 1, d), lambda i, j, s: (s[i] * 6 + sh_idx, 0, 0)),
                  pl.BlockSpec((d, tn), lambda i, j, s: (0, j))],
        out_specs=[pl.BlockSpec((tm, tn), lambda i, j, s: (i, j)),
                   pl.BlockSpec((tm, d), lambda i, j, s: (i, 0))],
        scratch_shapes=[pltpu.VMEM((tm, d), BF16)])
    return pl.pallas_call(
        functools.partial(_nm_matmul_kernel, one_col_tile=(tn == n)),
        grid_spec=grid_spec,
        out_shape=[jax.ShapeDtypeStruct((t, n), out_dtype), jax.ShapeDtypeStruct((t, d), BF16)],
        compiler_params=_cparams(("parallel", "arbitrary")),
        name="norm_mod_matmul",
    )(seg, x, g.reshape(1, d), mod3, mod3, w_bf16)


def _res_matmul_kernel(seg_ref, a_ref, w_ref, r_ref, gt_ref, o_ref):
    del seg_ref
    mm = jnp.dot(a_ref[...], w_ref[...], preferred_element_type=F32)
    o_ref[...] = r_ref[...] + gt_ref[0] * mm


def _res_matmul(a_bf16, w_bf16, res, mod3, gt_idx, seg, tm):
    t, k = a_bf16.shape
    n = w_bf16.shape[1]
    tn = n if k * n * 2 <= W_RESIDENT_BYTES else _col_tile(n, 1024)
    grid_spec = pltpu.PrefetchScalarGridSpec(
        num_scalar_prefetch=1,
        grid=(t // tm, n // tn),
        in_specs=[pl.BlockSpec((tm, k), lambda i, j, s: (i, 0)),
                  pl.BlockSpec((k, tn), lambda i, j, s: (0, j)),
                  pl.BlockSpec((tm, tn), lambda i, j, s: (i, j)),
                  pl.BlockSpec((1, 1, tn), lambda i, j, s: (s[i] * 6 + gt_idx, 0, j))],
        out_specs=pl.BlockSpec((tm, tn), lambda i, j, s: (i, j)))
    return pl.pallas_call(
        _res_matmul_kernel,
        grid_spec=grid_spec,
        out_shape=jax.ShapeDtypeStruct((t, n), F32),
        compiler_params=_cparams(("parallel", "parallel")),
        name="res_matmul",
    )(seg, a_bf16, w_bf16, res, mod3)


def _scan_kernel(rf_ref, kkf_ref, vf_ref, wf_ref, bf_ref, ktf_ref, rb_ref, kkb_ref, vb_ref, wb_ref, bb_ref,
                 ktb_ref, s0_ref, yf_ref, yb_ref, sfin_ref, s_scr, *, tb_steps, nk, kq):
    ti = pl.program_id(1)

    @pl.when(ti == 0)
    def _():
        s_scr[...] = s0_ref[...]

    def tree(parts):
        while len(parts) > 1:
            parts = [parts[i] + parts[i + 1] for i in range(0, len(parts), 2)]
        return parts[0]

    def all_parts(p):
        part = LANES // kq
        return tree([p] + [pltpu.roll(p, i * part, 1) for i in range(1, kq)])

    nacc = 4

    def advance(d, t, r_ref, kk_ref, v_ref, w_ref, b_ref, kt_ref, y_ref):
        row = pl.ds(t, 1)
        accs = [None] * nacc
        for k in range(nk):
            p = s_scr[d, k] * kk_ref[k, row, :]
            accs[k % nacc] = p if accs[k % nacc] is None else accs[k % nacc] + p
        sa = all_parts(tree(accs))
        v = v_ref[t]
        yacc = [None] * nacc
        for k in range(nk):
            s_new = s_scr[d, k] * w_ref[k, row, :] - sa * b_ref[k, row, :] + v * kt_ref[k, row, :]
            s_scr[d, k] = s_new
            p = s_new * r_ref[k, row, :]
            yacc[k % nacc] = p if yacc[k % nacc] is None else yacc[k % nacc] + p
        y_ref[t] = all_parts(tree(yacc))

    def step(s, carry):
        advance(0, s, rf_ref, kkf_ref, vf_ref, wf_ref, bf_ref, ktf_ref, yf_ref)
        advance(1, tb_steps - 1 - s, rb_ref, kkb_ref, vb_ref, wb_ref, bb_ref, ktb_ref, yb_ref)
        return carry

    lax.fori_loop(0, tb_steps, step, 0, unroll=2)

    @pl.when(ti == pl.num_programs(1) - 1)
    def _():
        sfin_ref[...] = s_scr[...]


def _scan(r, kk, v, w2, b2, kt2, s0, kq):
    nk, seq, c = r.shape
    nv = v.shape[1]
    tb_steps = _row_tile(seq, cap=16)
    nt = seq // tb_steps
    rowf = pl.BlockSpec((nk, tb_steps, LANES), lambda gi, ti: (0, ti, gi))
    rowb = pl.BlockSpec((nk, tb_steps, LANES), lambda gi, ti: (0, nt - 1 - ti, gi))
    valf = pl.BlockSpec((tb_steps, nv, LANES), lambda gi, ti: (ti, 0, gi))
    valb = pl.BlockSpec((tb_steps, nv, LANES), lambda gi, ti: (nt - 1 - ti, 0, gi))
    st = pl.BlockSpec((2, nk, nv, LANES), lambda gi, ti: (0, 0, 0, gi))
    return pl.pallas_call(
        functools.partial(_scan_kernel, tb_steps=tb_steps, nk=nk, kq=kq),
        grid=(c // LANES, nt),
        in_specs=[rowf, rowf, valf, rowf, rowf, rowf, rowb, rowb, valb, rowb, rowb, rowb, st],
        out_specs=[valf, valb, st],
        out_shape=[jax.ShapeDtypeStruct((seq, nv, c), F32), jax.ShapeDtypeStruct((seq, nv, c), F32),
                   jax.ShapeDtypeStruct((2, nk, nv, c), F32)],
        scratch_shapes=[pltpu.VMEM((2, nk, nv, LANES), F32)],
        compiler_params=_cparams(("parallel", "arbitrary")),
        name="rwkv7_scan",
    )(r, kk, v, w2[0], b2[0], kt2[0], r, kk, v, w2[1], b2[1], kt2[1], s0)


CHAIN_TB = 128


def _to_chain_kernel(x_ref, o_ref, a_scr, *stage, heads, nk, kq, values):
    bg, tb = x_ref.shape[0], x_ref.shape[1]
    n = A_HEAD_DIM
    for b in range(bg):
        xt = x_ref[b].T
        a_scr[b] = jnp.swapaxes(xt.reshape(heads, n, tb), 0, 1)
    for p in range(n if values else nk):
        pieces = [a_scr[b, p if values else q * nk + p] for q in range(kq) for b in range(bg)]
        tile = jnp.concatenate(pieces, axis=0).T
        if values:
            stage[0][p] = tile
        else:
            o_ref[p] = tile
    if values:
        o_ref[...] = jnp.swapaxes(stage[0][...], 0, 1)


def _to_chain(x, row0, bn, seq, kq, values):
    w_ = x.shape[-1]
    heads = w_ // A_HEAD_DIM
    n = A_HEAD_DIM
    nk = n // kq
    bg = LANES // (kq * heads)
    ng = bn // bg
    c = kq * bn * heads
    assert kq * bg * heads == LANES and c == ng * LANES and row0 % (bg * seq) == 0
    tb = min(CHAIN_TB, seq)
    g0 = row0 // (bg * seq)
    scratch = [pltpu.VMEM((bg, n, heads, tb), F32)]
    if values:
        out_spec = pl.BlockSpec((tb, n, LANES), lambda g, t: (t, 0, g))
        out_shape = jax.ShapeDtypeStruct((seq, n, c), F32)
        scratch.append(pltpu.VMEM((n, tb, LANES), F32))
    else:
        out_spec = pl.BlockSpec((nk, tb, LANES), lambda g, t: (0, t, g))
        out_shape = jax.ShapeDtypeStruct((nk, seq, c), F32)
    return pl.pallas_call(
        functools.partial(_to_chain_kernel, heads=heads, nk=nk, kq=kq, values=values),
        grid=(ng, seq // tb),
        in_specs=[pl.BlockSpec((bg, tb, w_), lambda g, t: (g0 + g, t, 0))],
        out_specs=out_spec,
        out_shape=out_shape,
        scratch_shapes=scratch,
        compiler_params=_cparams(("parallel", "parallel")),
        name="to_chain_layout",
    )(x.reshape(-1, seq, w_))


def _from_chain_kernel(y_ref, o_ref, stage, a_scr, *, heads):
    bg = o_ref.shape[0]
    n = A_HEAD_DIM
    tb = y_ref.shape[0]
    stage[...] = jnp.swapaxes(y_ref[...], 0, 1)
    for v in range(n):
        rows = stage[v].T
        for b in range(bg):
            a_scr[b, v] = rows[b * heads:(b + 1) * heads]
    for b in range(bg):
        o_ref[b] = jnp.swapaxes(a_scr[b], 0, 1).reshape(heads * n, tb).T


def _from_chain(y, bn, heads, kq):
    seq, n, c = y.shape
    bg = LANES // (kq * heads)
    tb = min(CHAIN_TB, seq)
    out = pl.pallas_call(
        functools.partial(_from_chain_kernel, heads=heads),
        grid=(c // LANES, seq // tb),
        in_specs=[pl.BlockSpec((tb, n, LANES), lambda g, t: (t, 0, g))],
        out_specs=pl.BlockSpec((bg, tb, heads * n), lambda g, t: (g, t, 0)),
        out_shape=jax.ShapeDtypeStruct((bn, seq, heads * n), F32),
        scratch_shapes=[pltpu.VMEM((n, tb, LANES), F32), pltpu.VMEM((bg, n, heads, tb), F32)],
        compiler_params=_cparams(("parallel", "parallel")),
        name="from_chain_layout",
    )(y)
    return out.reshape(bn * seq, heads * n)


def _attn_kernel(q_ref, k_ref, v_ref, o_ref, *, dh):
    k = k_ref[0]
    v = v_ref[0]
    for g in range(B_GROUP):
        q = q_ref[0, :, g * dh:(g + 1) * dh]
        s = lax.dot_general(q, k, (((1,), (1,)), ((), ())), preferred_element_type=F32)
        m = jnp.max(s, axis=-1, keepdims=True)
        p = jnp.exp(s - m)
        den = jnp.sum(p, axis=-1, keepdims=True)
        o = jnp.dot(p.astype(BF16), v, preferred_element_type=F32) / den
        o_ref[0, :, g * dh:(g + 1) * dh] = o.astype(o_ref.dtype)


def _attend(q, k, v, dh):
    bn, lq, qw = q.shape
    lk = k.shape[1]
    kvh = k.shape[2] // dh
    gw = B_GROUP * dh
    tq = _row_tile(lq, cap=256)
    return pl.pallas_call(
        functools.partial(_attn_kernel, dh=dh),
        grid=(bn, kvh, lq // tq),
        in_specs=[pl.BlockSpec((1, tq, gw), lambda b, h, i: (b, i, h)),
                  pl.BlockSpec((1, lk, dh), lambda b, h, i: (b, 0, h)),
                  pl.BlockSpec((1, lk, dh), lambda b, h, i: (b, 0, h))],
        out_specs=pl.BlockSpec((1, tq, gw), lambda b, h, i: (b, i, h)),
        out_shape=jax.ShapeDtypeStruct((bn, lq, qw), BF16),
        compiler_params=_cparams(("parallel", "parallel", "parallel")),
        name="attention",
    )(q, k, v)


G_GROUP = 16


def _gate_matrix_kernel(i1_ref, i2_ref, g_ref, o_ref, gtmp_scr, *, nkeys):
    iota = lax.broadcasted_iota(I32, (nkeys, nkeys), 0)

    def build(grp, carry):
        base = pl.multiple_of(grp * G_GROUP, G_GROUP)
        def token(tt, c2):
            t = base + tt
            a_t = jnp.where(iota == i1_ref[pl.ds(t, 1), :], 1.0, 0.0).astype(BF16)
            b_t = jnp.where(iota == i2_ref[pl.ds(t, 1), :], g_ref[pl.ds(t, 1), :], 0.0).astype(BF16)
            gtmp_scr[tt] = lax.dot_general(a_t, b_t, (((1,), (1,)), ((), ())), preferred_element_type=F32)
            return c2

        lax.fori_loop(0, G_GROUP, token, 0, unroll=G_GROUP)
        by_n1 = jnp.swapaxes(gtmp_scr[...], 0, 1).astype(BF16)
        for n1 in range(nkeys):
            o_ref[pl.ds(base, G_GROUP), n1 * nkeys:(n1 + 1) * nkeys] = by_n1[n1]
        return carry

    lax.fori_loop(0, o_ref.shape[0] // G_GROUP, build, 0)


def _gate_matrix(i1, i2, gate, nkeys):
    t, nj = i1.shape
    tb = LANES
    sel = pl.BlockSpec((tb, nj), lambda i: (i, 0))
    return pl.pallas_call(
        functools.partial(_gate_matrix_kernel, nkeys=nkeys),
        grid=(t // tb,),
        in_specs=[sel, sel, sel],
        out_specs=pl.BlockSpec((tb, nkeys * nkeys), lambda i: (i, 0)),
        out_shape=jax.ShapeDtypeStruct((t, nkeys * nkeys), BF16),
        scratch_shapes=[pltpu.VMEM((G_GROUP, nkeys, nkeys), F32)],
        compiler_params=_cparams(("parallel",)),
        name="peer_gate_matrix",
    )(i1, i2, gate)


def _to_bf16_kernel(x_ref, o_ref):
    o_ref[...] = x_ref[0].astype(BF16)


def _to_bf16(tables, layer):
    _, e, d = tables.shape
    te = _row_tile(e, cap=1024)
    return pl.pallas_call(
        _to_bf16_kernel,
        grid=(e // te,),
        in_specs=[pl.BlockSpec((1, te, d), lambda i: (layer, i, 0))],
        out_specs=pl.BlockSpec((te, d), lambda i: (i, 0)),
        out_shape=jax.ShapeDtypeStruct((e, d), BF16),
        compiler_params=_cparams(("parallel",)),
        name="to_bf16",
    )(tables)


def _peer_kernel(seg_ref, xb_ref, gm_ref, u_ref, v_ref, r_ref, gt_ref, o_ref, acc_scr):
    del seg_ref
    e = pl.program_id(1)

    @pl.when(e == 0)
    def _():
        acc_scr[...] = jnp.zeros_like(acc_scr)

    h = lax.dot_general(xb_ref[...], u_ref[...], (((1,), (1,)), ((), ())), preferred_element_type=F32)
    act = 0.5 * h * (1.0 + lax.erf(h * (1.0 / math.sqrt(2.0)))) * gm_ref[...].astype(F32)
    acc_scr[...] += jnp.dot(act.astype(BF16), v_ref[...], preferred_element_type=F32)

    @pl.when(e == pl.num_programs(1) - 1)
    def _():
        o_ref[...] = r_ref[...] + gt_ref[0] * acc_scr[...]


def _peer_experts(xb, gmat, u_bf16, v_bf16, res, mod3, gt_idx, seg, tm, row0=0, rows=None):
    t, d = xb.shape
    rows = t if rows is None else rows
    r0 = row0 // tm
    ne = u_bf16.shape[0]
    te = 1024
    grid_spec = pltpu.PrefetchScalarGridSpec(
        num_scalar_prefetch=1,
        grid=(rows // tm, ne // te),
        in_specs=[pl.BlockSpec((tm, d), lambda i, e, s: (r0 + i, 0)),
                  pl.BlockSpec((tm, te), lambda i, e, s: (r0 + i, e)),
                  pl.BlockSpec((te, d), lambda i, e, s: (e, 0)),
                  pl.BlockSpec((te, d), lambda i, e, s: (e, 0)),
                  pl.BlockSpec((tm, d), lambda i, e, s: (r0 + i, 0)),
                  pl.BlockSpec((1, 1, d), lambda i, e, s: (s[r0 + i] * 6 + gt_idx, 0, 0))],
        out_specs=pl.BlockSpec((tm, d), lambda i, e, s: (i, 0)),
        scratch_shapes=[pltpu.VMEM((tm, d), F32)])
    return pl.pallas_call(
        _peer_kernel,
        grid_spec=grid_spec,
        out_shape=jax.ShapeDtypeStruct((rows, d), F32),
        compiler_params=_cparams(("parallel", "arbitrary")),
        name="peer_experts",
    )(seg, xb, gmat, u_bf16, v_bf16, res, mod3)


def _topk_rows(vals, payload, rows_out):
    big = jnp.float32(2 ** 30)
    top_v = jnp.zeros(rows_out.shape, F32)
    top_p = jnp.zeros(rows_out.shape, F32)
    for it in range(PEER_TOPK):
        m = jnp.max(vals, axis=0, keepdims=True)
        sel = jnp.min(jnp.where(vals == m, payload, big), axis=0, keepdims=True)
        top_v = jnp.where(rows_out == it, m, top_v)
        top_p = jnp.where(rows_out == it, sel, top_p)
        vals = jnp.where(payload == sel, -jnp.inf, vals)
    return top_v, top_p


def _top_pair_sums(s1, s2, rows_out):
    k = PEER_TOPK
    half = k // 2
    tt = s1.shape[1]
    big = jnp.float32(2 ** 30)
    lists = [s1[:half] + s2[j:j + 1] for j in range(k)]
    singles = s1[half:] + s2[0:1]
    sub = lax.broadcasted_iota(I32, (half, tt), 0).astype(F32)
    head_id = sub * k
    single_id = (sub + half) * k
    top_v = jnp.zeros(rows_out.shape, F32)
    top_p = jnp.zeros(rows_out.shape, F32)
    for it in range(k):
        m = jnp.max(jnp.maximum(lists[0], singles), axis=0, keepdims=True)
        sel = jnp.min(jnp.minimum(jnp.where(lists[0] == m, head_id, big), jnp.where(singles == m, single_id, big)),
                      axis=0, keepdims=True)
        top_v = jnp.where(rows_out == it, m, top_v)
        top_p = jnp.where(rows_out == it, sel, top_p)
        pop = head_id == sel
        lists = [jnp.where(pop, lists[j + 1], lists[j]) for j in range(k - 1)] + [jnp.where(pop, -jnp.inf, lists[-1])]
        head_id = jnp.where(pop, head_id + 1.0, head_id)
        singles = jnp.where(single_id == sel, -jnp.inf, singles)
    return top_v, top_p


def _gather_rows(table, sel):
    out = jnp.zeros(sel.shape, table.dtype)
    for i in range(PEER_TOPK):
        out = jnp.where(sel == i, table[i:i + 1, :], out)
    return out


def _route_kernel(q_ref, keys_ref, i1_ref, i2_ref, g_ref, n1_scr, n2_scr, gate_scr, *, heads, nkeys, dq):
    tt = q_ref.shape[0]
    k = PEER_TOPK
    n_iota = lax.broadcasted_iota(I32, (nkeys, tt), 0).astype(F32)
    rows_out = lax.broadcasted_iota(I32, (k, tt), 0)

    def head(h, carry):
        tops = []
        for c in range(2):
            col = pl.multiple_of((h * 2 + c) * dq, dq)
            qhc = q_ref[:, pl.ds(col, dq)].astype(BF16)
            khc = keys_ref[h, c].astype(BF16)
            s = lax.dot_general(khc, qhc, (((1,), (1,)), ((), ())), preferred_element_type=F32)
            tops.append(_topk_rows(s, n_iota, rows_out))
        (s1, i1), (s2, i2) = tops
        top, ci = _top_pair_sums(s1, s2, rows_out)
        ci = ci.astype(I32)
        n1 = _gather_rows(i1, lax.shift_right_logical(ci, 4)).astype(I32)
        n2 = _gather_rows(i2, lax.bitwise_and(ci, k - 1)).astype(I32)
        ex = jnp.exp(top - top[0:1, :])
        gate = ex / jnp.sum(ex, axis=0, keepdims=True)
        row = pl.multiple_of(h * k, k)
        n1_scr[pl.ds(row, k), :] = n1
        n2_scr[pl.ds(row, k), :] = n2
        gate_scr[pl.ds(row, k), :] = gate
        return carry

    lax.fori_loop(0, heads, head, 0)
    i1_ref[...] = n1_scr[...].T
    i2_ref[...] = n2_scr[...].T
    g_ref[...] = gate_scr[...].T


def _peer_route(q, sub_keys):
    t, qw = q.shape
    heads, _, nkeys, dq = sub_keys.shape
    hk = heads * PEER_TOPK
    tt = LANES
    out = pl.BlockSpec((tt, hk), lambda i: (i, 0))
    return pl.pallas_call(
        functools.partial(_route_kernel, heads=heads, nkeys=nkeys, dq=dq),
        grid=(t // tt,),
        in_specs=[pl.BlockSpec((tt, qw), lambda i: (i, 0)),
                  pl.BlockSpec((heads, 2, nkeys, dq), lambda i: (0, 0, 0, 0))],
        out_specs=[out, out, out],
        out_shape=[jax.ShapeDtypeStruct((t, hk), I32), jax.ShapeDtypeStruct((t, hk), I32),
                   jax.ShapeDtypeStruct((t, hk), F32)],
        scratch_shapes=[pltpu.VMEM((hk, tt), I32), pltpu.VMEM((hk, tt), I32), pltpu.VMEM((hk, tt), F32)],
        compiler_params=_cparams(("parallel",)),
        name="peer_route",
    )(q, sub_keys)


def _split(x):
    hi = x.astype(BF16)
    return hi, (x - hi.astype(F32)).astype(BF16)


def _mm(a, b):
    return jnp.dot(a, b, preferred_element_type=F32)


def _head_sum(x, ones_blockdiag):
    hi, lo = _split(x)
    return _mm(hi, ones_blockdiag) + _mm(lo, ones_blockdiag)


def _shift_rows(x, first_row, last_row):
    n = x.shape[0]
    rows = lax.broadcasted_iota(I32, x.shape, 0)
    prev = jnp.where(rows == 0, first_row, pltpu.roll(x, 1, 0))
    nxt = jnp.where(rows == n - 1, last_row, pltpu.roll(x, n - 1, 0))
    return prev, nxt


def _conv3(x, first_row, last_row, taps):
    prev, nxt = _shift_rows(x, first_row, last_row)
    return prev * taps[0:1] + x * taps[1:2] + nxt * taps[2:3]


def _rwkv_pre_kernel(r_ref, k_ref, v_ref, low_ref, pr_ref, pk_ref, pv_ref, plow_ref, nr_ref, nk_ref, nv_ref,
                     nlow_ref, cr_ref, ck_ref, cv_ref, clow_ref, w0_ref, wu_ref, a0_ref, au_ref, gu_ref,
                     kkg_ref, ka_ref, rk_ref, ones_ref,
                     ro_ref, kko_ref, vo_ref, w0o_ref, w1o_ref, b0o_ref, b1o_ref, kt0o_ref, kt1o_ref,
                     go_ref, bonus_ref, *, rw, ra):
    r = _conv3(r_ref[...], pr_ref[0], nr_ref[0], cr_ref[...])
    k = _conv3(k_ref[...], pk_ref[0], nk_ref[0], ck_ref[...])
    v = _conv3(v_ref[...], pv_ref[0], nv_ref[0], cv_ref[...])
    low = _conv3(low_ref[...], plow_ref[0], nlow_ref[0], clow_ref[...])
    ones = ones_ref[...]
    kk = k * kkg_ref[...]
    kk = kk * lax.rsqrt(_head_sum(kk * kk, ones) + 1e-12)
    ro_ref[...] = r
    kko_ref[...] = kk
    vo_ref[...] = v
    bonus_ref[...] = _head_sum(r * k * rk_ref[...], ones) * v
    gd = low[:, 2 * rw + 2 * ra:]
    go_ref[...] = _mm(jax.nn.sigmoid(gd).astype(BF16), gu_ref[...])
    for d_, (wo, bo, kto) in enumerate(((w0o_ref, b0o_ref, kt0o_ref), (w1o_ref, b1o_ref, kt1o_ref))):
        wd = low[:, d_ * rw:(d_ + 1) * rw]
        ad = low[:, 2 * rw + d_ * ra:2 * rw + (d_ + 1) * ra]
        lw = w0_ref[d_:d_ + 1, :] + _mm(jnp.tanh(wd).astype(BF16), wu_ref[d_])
        softplus = jnp.maximum(-lw, 0.0) + jnp.log1p(jnp.exp(-jnp.abs(lw)))
        wo[...] = jnp.exp(-jnp.exp(-softplus - 0.5))
        a = jax.nn.sigmoid(a0_ref[d_:d_ + 1, :] + _mm(ad.astype(BF16), au_ref[d_]))
        bo[...] = kk * a
        kto[...] = k * (1.0 + (a - 1.0) * ka_ref[...])


def _rwkv_pre(z, prev_rows, next_rows, P, tm, offs):
    t = z.shape[0]
    w_ = P['a_w0'].shape[-1]
    rw, ra = P['a_wu'].shape[1], P['a_au'].shape[1]
    lw = offs['low_w']
    li = offs['low'] // lw
    row = lambda c, wd: pl.BlockSpec((tm, wd), lambda i: (i, c))
    edge = lambda c, wd: pl.BlockSpec((1, 1, wd), lambda i: (i, 0, c))
    full = lambda a: pl.BlockSpec(a.shape, lambda i: (0,) * a.ndim)
    conv = P['a_conv']
    consts = [conv[:, :w_], conv[:, w_:2 * w_], conv[:, 2 * w_:3 * w_], conv[:, 3 * w_:],
              P['a_w0'], P['a_wu'].astype(BF16), P['a_a0'], P['a_au'].astype(BF16), P['a_gu'].astype(BF16),
              P['a_kk'].reshape(1, w_), P['a_ka'].reshape(1, w_), P['a_rk'].reshape(1, w_),
              jnp.asarray(_blockdiag_ones(w_), BF16)]
    out = pl.BlockSpec((tm, w_), lambda i: (i, 0))
    return pl.pallas_call(
        functools.partial(_rwkv_pre_kernel, rw=rw, ra=ra),
        grid=(t // tm,),
        in_specs=[row(0, w_), row(1, w_), row(2, w_), row(li, lw),
                  edge(0, w_), edge(1, w_), edge(2, w_), edge(li, lw),
                  edge(0, w_), edge(1, w_), edge(2, w_), edge(li, lw)] + [full(a) for a in consts],
        out_specs=[out] * 11,
        out_shape=[jax.ShapeDtypeStruct((t, w_), F32)] * 11,
        compiler_params=_cparams(("parallel",)),
        name="rwkv7_pre",
    )(z, z, z, z, prev_rows, prev_rows, prev_rows, prev_rows, next_rows, next_rows, next_rows, next_rows, *consts)


def _blockdiag_ones(width):
    idx = np.arange(width) // A_HEAD_DIM
    return (idx[:, None] == idx[None, :]).astype(np.float32)


def _rwkv_post_kernel(yf_ref, yb_ref, bonus_ref, g_ref, lnw_ref, lnb_ref, ones_ref, o_ref):
    ones = ones_ref[...]
    y = yf_ref[...] + yb_ref[...]
    mu = _head_sum(y, ones) * (1.0 / A_HEAD_DIM)
    dlt = y - mu
    var = _head_sum(dlt * dlt, ones) * (1.0 / A_HEAD_DIM)
    yn = dlt * lax.rsqrt(var + A_GN_EPS) * lnw_ref[...] + lnb_ref[...]
    o_ref[...] = ((yn + bonus_ref[...]) * g_ref[...]).astype(o_ref.dtype)


def _rwkv_post(yf, yb, bonus, g, row0, P, tm):
    t, w_ = yf.shape
    r0 = row0 // tm
    row = pl.BlockSpec((tm, w_), lambda i: (i, 0))
    off = pl.BlockSpec((tm, w_), lambda i: (r0 + i, 0))
    vec = pl.BlockSpec((1, w_), lambda i: (0, 0))
    return pl.pallas_call(
        _rwkv_post_kernel,
        grid=(t // tm,),
        in_specs=[row, row, off, off, vec, vec, pl.BlockSpec((w_, w_), lambda i: (0, 0))],
        out_specs=row,
        out_shape=jax.ShapeDtypeStruct((t, w_), BF16),
        compiler_params=_cparams(("parallel",)),
        name="rwkv7_post",
    )(yf, yb, bonus, g, P['a_ln_w'].reshape(1, w_), P['a_ln_b'].reshape(1, w_),
      jnp.asarray(_blockdiag_ones(w_), BF16))


def _rwkv_post_chain_kernel(yf_ref, yb_ref, bonus_ref, g_ref, lnw_ref, lnb_ref, ones_ref, o_ref, stage, a_scr,
                            *, heads):
    bg = o_ref.shape[0]
    n = A_HEAD_DIM
    tb = yf_ref.shape[0]
    stage[...] = jnp.swapaxes(yf_ref[...] + yb_ref[...], 0, 1)
    for v in range(n):
        rows = stage[v].T
        for b in range(bg):
            a_scr[b, v] = rows[b * heads:(b + 1) * heads]
    ones = ones_ref[...]
    w_ = heads * n
    y = jnp.concatenate([jnp.swapaxes(a_scr[b], 0, 1).reshape(w_, tb).T for b in range(bg)], axis=0)
    mu = _head_sum(y, ones) * (1.0 / n)
    dlt = y - mu
    var = _head_sum(dlt * dlt, ones) * (1.0 / n)
    yn = dlt * lax.rsqrt(var + A_GN_EPS) * lnw_ref[...] + lnb_ref[...]
    out = (yn + bonus_ref[...].reshape(bg * tb, w_)) * g_ref[...].reshape(bg * tb, w_)
    o_ref[...] = out.astype(o_ref.dtype).reshape(bg, tb, w_)


POST_TB = 64


def _rwkv_post_chain(yf, yb, bonus, g, row0, bn, heads, kq, P):
    seq, n, c = yf.shape
    w_ = heads * n
    bg = LANES // (kq * heads)
    tb = min(POST_TB, seq)
    g0 = row0 // (bg * seq)
    chain = pl.BlockSpec((tb, n, LANES), lambda gi, t: (t, 0, gi))
    tok = pl.BlockSpec((bg, tb, w_), lambda gi, t: (g0 + gi, t, 0))
    vec = pl.BlockSpec((1, w_), lambda gi, t: (0, 0))
    out = pl.pallas_call(
        functools.partial(_rwkv_post_chain_kernel, heads=heads),
        grid=(c // LANES, seq // tb),
        in_specs=[chain, chain, tok, tok, vec, vec, pl.BlockSpec((w_, w_), lambda gi, t: (0, 0))],
        out_specs=pl.BlockSpec((bg, tb, w_), lambda gi, t: (gi, t, 0)),
        out_shape=jax.ShapeDtypeStruct((bn, seq, w_), BF16),
        scratch_shapes=[pltpu.VMEM((n, tb, LANES), F32), pltpu.VMEM((bg, n, heads, tb), F32)],
        compiler_params=_cparams(("parallel", "parallel")),
        name="rwkv7_post_chain",
    )(yf, yb, bonus.reshape(-1, seq, w_), g.reshape(-1, seq, w_), P['a_ln_w'].reshape(1, w_),
      P['a_ln_b'].reshape(1, w_), jnp.asarray(_blockdiag_ones(w_), BF16))
    return out.reshape(bn * seq, w_)


def _attn_pre_kernel(q_ref, kv_ref, qn_ref, kn_ref, cos_ref, sin_ref, qo_ref, ko_ref, vo_ref, kf_ref, vf_ref,
                     *, dh, latent):
    def rms(x, g):
        return x * lax.rsqrt(jnp.mean(x * x, axis=-1, keepdims=True) + NORM_EPS) * g

    def rope(x):
        if not latent:
            return x
        lanes = lax.broadcasted_iota(I32, x.shape, 1)
        quarter = dh // 4
        partner = jnp.where(lanes % (2 * quarter) < quarter, pltpu.roll(x, dh - quarter, 1), pltpu.roll(x, quarter, 1))
        return x * cos_ref[...] + partner * sin_ref[...]

    nq = q_ref.shape[1] // dh
    nkv = kv_ref.shape[1] // (2 * dh)
    for h in range(nq):
        q = rope(rms(q_ref[:, h * dh:(h + 1) * dh], qn_ref[...]))
        qo_ref[:, h * dh:(h + 1) * dh] = (q * dh ** -0.5).astype(BF16)
    for h in range(nkv):
        k = rms(kv_ref[:, h * dh:(h + 1) * dh], kn_ref[...])
        v = kv_ref[:, (nkv + h) * dh:(nkv + h + 1) * dh]
        kf_ref[:, h * dh:(h + 1) * dh] = k
        vf_ref[:, h * dh:(h + 1) * dh] = v
        ko_ref[:, h * dh:(h + 1) * dh] = rope(k).astype(BF16)
        vo_ref[:, h * dh:(h + 1) * dh] = v.astype(BF16)


def _rope_tables(seq, dh):
    quarter = dh // 4
    inv = ROPE_THETA ** (-np.arange(quarter, dtype=np.float64) / quarter)
    pos = np.arange(seq)
    ang_r = (pos // GRID_W)[:, None] * inv[None, :]
    ang_c = (pos % GRID_W)[:, None] * inv[None, :]
    cos = np.concatenate([np.cos(ang_r)] * 2 + [np.cos(ang_c)] * 2, axis=1)
    sin = np.concatenate([-np.sin(ang_r), np.sin(ang_r), -np.sin(ang_c), np.sin(ang_c)], axis=1)
    return jnp.asarray(cos, F32), jnp.asarray(sin, F32)


def _attn_pre(z, qnorm, knorm, offs, row0, rows, seq, tm, latent):
    dh = qnorm.shape[-1]
    bw, kvw2 = offs['q_w'], offs['kv_w']
    r0 = row0 // tm
    per_seq = seq // tm
    cos, sin = _rope_tables(seq, dh) if latent else (jnp.zeros((tm, dh), F32), jnp.zeros((tm, dh), F32))
    tab = pl.BlockSpec((tm, dh), (lambda i: (i % per_seq, 0)) if latent else (lambda i: (0, 0)))
    vec = pl.BlockSpec((1, dh), lambda i: (0, 0))
    kvo = pl.BlockSpec((tm, kvw2 // 2), lambda i: (i, 0))
    return pl.pallas_call(
        functools.partial(_attn_pre_kernel, dh=dh, latent=latent),
        grid=(rows // tm,),
        in_specs=[pl.BlockSpec((tm, bw), lambda i: (r0 + i, offs['q'] // bw)),
                  pl.BlockSpec((tm, kvw2), lambda i: (r0 + i, offs['kv'] // kvw2)),
                  vec, vec, tab, tab],
        out_specs=[pl.BlockSpec((tm, bw), lambda i: (i, 0)), kvo, kvo, kvo, kvo],
        out_shape=[jax.ShapeDtypeStruct((rows, bw), BF16), jax.ShapeDtypeStruct((rows, kvw2 // 2), BF16),
                   jax.ShapeDtypeStruct((rows, kvw2 // 2), BF16), jax.ShapeDtypeStruct((rows, kvw2 // 2), F32),
                   jax.ShapeDtypeStruct((rows, kvw2 // 2), F32)],
        compiler_params=_cparams(("parallel",)),
        name="attention_pre",
    )(z, z, qnorm.reshape(1, dh), knorm.reshape(1, dh), cos, sin)


HY_BLK = 256
HY_TILE_ELEMS = 128 * 1024


def _dft_consts():
    n = 2 * HY_BLK
    k = np.arange(HY_BLK, dtype=np.float64)[:, None] + 0.5
    s = np.arange(HY_BLK, dtype=np.float64)[None, :]
    th = 2.0 * np.pi * k * s / n
    fwd = np.concatenate([np.cos(th), -np.sin(th)], axis=0)
    tau = np.arange(n, dtype=np.float64)[:, None]
    ph = 2.0 * np.pi * tau * (np.arange(HY_BLK, dtype=np.float64)[None, :] + 0.5) / n
    inv = np.concatenate([np.cos(ph), -np.sin(ph)], axis=1) * (2.0 / n)
    inv_cat = np.concatenate([inv[:HY_BLK], inv[HY_BLK:]], axis=1)

    return jnp.asarray(fwd, F32).astype(BF16), jnp.asarray(inv_cat, F32).astype(BF16)


def _lag_features(seq, emb):
    bands = (emb - 1) // 2
    t = np.linspace(0.0, 1.0, seq)
    wpos = 2.0 * np.pi * np.arange(seq) / seq
    f = np.linspace(1e-4, bands - 1, bands)
    z = np.concatenate([t[:, None], np.cos(f[None, :] * wpos[:, None]), -np.sin(f[None, :] * wpos[:, None])], axis=1)
    lag = np.concatenate([np.zeros(1, np.int64), np.arange(seq - 1, 0, -1), np.arange(seq)])
    return jnp.asarray(z[lag], F32)


def _hyena_filter_kernel(z_ref, fw1_ref, fb1_ref, freq_ref, fw2_ref, fb2_ref, w3b_ref, w3f_ref, dl_ref,
                         fh_ref, g_ref, hdn_scr, f_scr, *, seq):
    @pl.when((pl.program_id(0) == 0) & (pl.program_id(1) == 0))
    def _():
        h1 = jnp.sin(freq_ref[...] * (_mm(z_ref[...].astype(BF16), fw1_ref[...]) + fb1_ref[...]))
        hdn_scr[...] = jnp.sin(freq_ref[...] * (_mm(h1.astype(BF16), fw2_ref[...]) + fb2_ref[...])).astype(BF16)

    decay = jnp.exp(-z_ref[:, 0:1] * dl_ref[...])
    f_scr[0:seq, :] = _mm(hdn_scr[0:seq, :], w3b_ref[...]) * decay[0:seq, :]
    f_scr[seq:2 * seq, :] = _mm(hdn_scr[seq:2 * seq, :], w3f_ref[...]) * decay[seq:2 * seq, :]
    f = f_scr[...]
    scale = lax.rsqrt(jnp.sum(f * f, axis=0, keepdims=True) + 1e-12)
    rows = lax.broadcasted_iota(I32, f.shape, 0)
    f_scr[...] = jnp.where(rows == 0, 0.0, f * scale)
    for m in range(2 * seq // HY_BLK):
        g_ref[0, m] = _mm(fh_ref[...], f_scr[m * HY_BLK:(m + 1) * HY_BLK, :].astype(BF16))


def _hyena_spectra(seq, P, c_width, tc):
    emb, hid = P['c_fw1'].shape
    z = _lag_features(seq, emb)
    embp = 64
    z = jnp.pad(z, ((0, 0), (0, embp - emb)))
    fw1 = jnp.pad(P['c_fw1'], ((0, embp - emb), (0, 0))).astype(BF16)
    deltas = jnp.asarray(np.abs(np.linspace(math.log(DECAY_TARGET) / SLOW_DECAY_PCT,
                                            math.log(DECAY_TARGET) / FAST_DECAY_PCT, c_width)), F32).reshape(1, c_width)
    fh, _ = _dft_consts()
    nct = c_width // tc
    nseg = 2 * seq // HY_BLK
    full = lambda a: pl.BlockSpec(a.shape, lambda o, j: (0,) * a.ndim)
    w3 = P['c_fw3'].astype(BF16)
    consts = [z, fw1, P['c_fb1'].reshape(1, hid), P['c_freq'].reshape(1, hid), P['c_fw2'].astype(BF16),
              P['c_fb2'].reshape(1, hid)]
    return pl.pallas_call(
        functools.partial(_hyena_filter_kernel, seq=seq),
        grid=(HYENA_ORDER, nct),
        in_specs=[full(a) for a in consts] + [
            pl.BlockSpec((hid, tc), lambda o, j: (0, (o * 2 + 1) * nct + j)),
            pl.BlockSpec((hid, tc), lambda o, j: (0, (o * 2) * nct + j)),
            pl.BlockSpec((1, tc), lambda o, j: (0, j)), full(fh)],
        out_specs=pl.BlockSpec((1, nseg, 2 * HY_BLK, tc), lambda o, j: (o, 0, 0, j)),
        out_shape=jax.ShapeDtypeStruct((HYENA_ORDER, nseg, 2 * HY_BLK, c_width), F32),
        scratch_shapes=[pltpu.VMEM((2 * seq, hid), BF16), pltpu.VMEM((2 * seq, tc), F32)],
        compiler_params=_cparams(("arbitrary", "arbitrary")),
        name="hyena_spectra",
    )(*consts, w3, w3, deltas, fh)


def _hyena_conv_kernel(zin_ref, gate_ref, tz_ref, bz_ref, tg_ref, bg_ref, bias_ref, g_ref, fh_ref, ih_ref,
                       o_ref, z_scr, gate_scr, u_scr, y_scr, *, nb, conv_in):
    half = HY_BLK
    zero = jnp.zeros((1, zin_ref.shape[-1]), F32)
    z = zin_ref[0]
    if conv_in:
        z = _conv3(z, zero, zero, tz_ref[...]) + bz_ref[...]
    z_scr[...] = z
    for j in range(nb):
        u_scr[j] = _mm(fh_ref[...], z_scr[j * half:(j + 1) * half, :].astype(BF16))

    width = zin_ref.shape[-1]
    chunk = max(8, 32 * LANES // width)

    def spectra(pieces):
        for c in range(half // chunk):
            re = pl.ds(c * chunk, chunk)
            im = pl.ds(half + c * chunk, chunk)

            def add_block(j, acc):
                ur, ui = u_scr[j, re, :], u_scr[j, im, :]
                out = []
                for a, ii in enumerate(pieces):
                    m = ii - 1 - j + nb
                    gr, gi = g_ref[0, m, re, :], g_ref[0, m, im, :]
                    out.append((acc[a][0] + (ur * gr - ui * gi), acc[a][1] + (ur * gi + ui * gr)))
                return tuple(out)

            zero = jnp.zeros((chunk, width), F32)
            acc = lax.fori_loop(0, nb, add_block, tuple((zero, zero) for _ in pieces), unroll=min(4, nb))
            for a, ii in enumerate(pieces):
                y_scr[ii, re, :] = acc[a][0]
                y_scr[ii, im, :] = acc[a][1]

    def pair(p, carry):
        spectra([2 * p, 2 * p + 1])
        return carry

    lax.fori_loop(0, (nb + 1) // 2, pair, 0)
    if (nb + 1) % 2:
        spectra([nb])

    gate_scr[...] = _conv3(gate_ref[0], zero, zero, tg_ref[...]) + bg_ref[...]

    def block(i, carry):
        rows = pl.ds(pl.multiple_of(i * half, half), half)
        ycat = jnp.concatenate([y_scr[i + 1], y_scr[i]], axis=0)
        conv = _mm(ih_ref[...], ycat.astype(BF16))
        o_ref[0, rows, :] = gate_scr[rows, :] * (conv + bias_ref[...] * z_scr[rows, :])
        return carry

    lax.fori_loop(0, nb, block, 0)


def _hyena_conv(zin, zin_col0, gate_src, gate_col0, taps, tap_bias, bias, spectra, order, row0, nseq, seq, tc, conv_in):
    c_width = bias.shape[-1]
    nb = seq // HY_BLK
    nct = c_width // tc
    fh, ih = _dft_consts()
    s0 = row0 // seq
    zc, gc = zin_col0 // tc, gate_col0 // tc
    tapc = (zc if conv_in else gc)
    full = lambda a: pl.BlockSpec(a.shape, lambda j, b: (0,) * a.ndim)
    zin3 = zin.reshape(-1, seq, zin.shape[-1])
    gate3 = gate_src.reshape(-1, seq, gate_src.shape[-1])
    zs0 = s0 if conv_in else 0
    return pl.pallas_call(
        functools.partial(_hyena_conv_kernel, nb=nb, conv_in=conv_in),
        grid=(nct, nseq),
        in_specs=[pl.BlockSpec((1, seq, tc), lambda j, b: (zs0 + b, 0, zc + j)),
                  pl.BlockSpec((1, seq, tc), lambda j, b: (s0 + b, 0, gc + j)),
                  pl.BlockSpec((3, tc), lambda j, b: (0, tapc + j)),
                  pl.BlockSpec((1, tc), lambda j, b: (0, tapc + j)),
                  pl.BlockSpec((3, tc), lambda j, b: (0, gc + j)),
                  pl.BlockSpec((1, tc), lambda j, b: (0, gc + j)),
                  pl.BlockSpec((1, tc), lambda j, b: (0, j)),
                  pl.BlockSpec((1, 2 * nb, 2 * HY_BLK, tc), lambda j, b: (order, 0, 0, j)),
                  full(fh), full(ih)],
        out_specs=pl.BlockSpec((1, seq, tc), lambda j, b: (b, 0, j)),
        out_shape=jax.ShapeDtypeStruct((nseq, seq, c_width), F32),
        scratch_shapes=[pltpu.VMEM((seq, tc), F32), pltpu.VMEM((seq, tc), F32),
                        pltpu.VMEM((nb, 2 * HY_BLK, tc), F32), pltpu.VMEM((nb + 1, 2 * HY_BLK, tc), F32)],
        compiler_params=_cparams(("arbitrary", "arbitrary")),
        name="hyena_conv",
    )(zin3, gate3, taps, tap_bias, taps, tap_bias, bias, spectra, fh, ih)


def _rwkv_scan_pass(r, kk, v, w2, b2, kt2, s0, row0, bn, seq):
    w_ = r.shape[-1]
    heads = w_ // A_HEAD_DIM
    n = A_HEAD_DIM
    bh = bn * heads
    kq = max(1, LANES // bh)
    nk = n // kq
    c = kq * bh
    rows = lambda x: _to_chain(x, row0, bn, seq, kq, False)
    if s0 is None:
        s0r = jnp.zeros((2, nk, n, c), F32)
    else:
        s0r = jnp.transpose(s0.reshape(bn, 2, heads, n, kq, nk), (1, 5, 3, 4, 0, 2)).reshape(2, nk, n, c)
    yf, yb, sfin = _scan(rows(r), rows(kk), _to_chain(v, row0, bn, seq, kq, True),
                         (rows(w2[0]), rows(w2[1])), (rows(b2[0]), rows(b2[1])), (rows(kt2[0]), rows(kt2[1])),
                         s0r, kq)
    sfin = jnp.transpose(sfin.reshape(2, nk, n, kq, bn, heads), (4, 0, 5, 2, 3, 1)).reshape(bn, 2, heads, n, n)
    return yf, yb, sfin, kq


def kernel(x_prompt, x_sample, cache_b_k, cache_b_v, state_a, c, c_ctx, mod_w, mod_b, norm1, norm2,
           even_w_in, even_a_conv, even_a_w0, even_a_wu, even_a_a0, even_a_au, even_a_gu, even_a_kk,
           even_a_ka, even_a_rk, even_a_ln_w, even_a_ln_b, even_b_qnorm, even_b_knorm, even_w_out,
           odd_w_in, odd_c_conv, odd_c_conv_b, odd_c_fw1, odd_c_fb1, odd_c_freq, odd_c_fw2, odd_c_fb2,
           odd_c_fw3, odd_c_bias, odd_w_out, peer_wq, peer_keys, peer_u, peer_v):
    bp, sp, d = x_prompt.shape
    bs, ss, _ = x_sample.shape
    depth = mod_w.shape[0]
    tp, ts = bp * sp, bs * ss
    t_all = tp + ts
    a_width = even_a_w0.shape[-1]
    a_cols = even_a_conv.shape[-1]
    dh = even_b_qnorm.shape[-1]
    b_width = d // 2
    kv_width = b_width // B_GROUP
    kvh = kv_width // dh
    c_width = odd_c_bias.shape[-1]
    nkeys = peer_keys.shape[3]
    assert bs + 1 <= 8 and nkeys == LANES and peer_keys.shape[1] * PEER_TOPK == LANES

    def seg_ids(tile):
        return jnp.concatenate([jnp.zeros((tp // tile,), I32),
                                1 + jnp.arange(ts // tile, dtype=I32) // (ss // tile)])

    tm = _row_tile(tp, ss, cap=512)
    segs = {tile: seg_ids(tile) for tile in (tm, 2 * tm) if tp % tile == 0 and ss % tile == 0}
    seg = segs[tm]

    cond8 = jnp.zeros((8, d), F32).at[0].set(c_ctx).at[1:1 + bs].set(c)
    mods = _mod_table(cond8, mod_w, mod_b)

    x = jnp.concatenate([x_prompt.reshape(tp, d), x_sample.reshape(ts, d)], axis=0)
    new_k, new_v, new_s = [], [], []
    for layer in range(depth):
        j = layer // 2
        mod3 = mods[layer].reshape(8 * 6, 1, d)
        if layer % 2 == 0:
            PA = dict(a_conv=even_a_conv[j], a_w0=even_a_w0[j], a_wu=even_a_wu[j], a_a0=even_a_a0[j],
                      a_au=even_a_au[j], a_gu=even_a_gu[j], a_kk=even_a_kk[j], a_ka=even_a_ka[j],
                      a_rk=even_a_rk[j], a_ln_w=even_a_ln_w[j], a_ln_b=even_a_ln_b[j])
            w_in = even_w_in[j]
            w_perm = jnp.concatenate([w_in[:, :3 * a_width], w_in[:, a_cols:], w_in[:, 3 * a_width:a_cols]],
                                     axis=1).astype(BF16)
            offs = dict(q=3 * a_width, q_w=b_width, kv=3 * a_width + b_width, kv_w=2 * kv_width,
                        low=3 * a_width + b_width + 2 * kv_width, low_w=a_cols - 3 * a_width)
            assert a_width == b_width and offs['kv'] % offs['kv_w'] == 0 and offs['low'] % offs['low_w'] == 0
            pad = (-w_perm.shape[1]) % 1024
            if pad <= LANES:
                w_perm = jnp.pad(w_perm, ((0, 0), (0, pad)))
            z, _ = _nm_matmul(x, norm1[layer], mod3, 1, 0, segs, w_perm)
            tmr = _row_tile(sp, ss, cap=256)
            nt = t_all // tmr
            starts = np.concatenate([np.arange(0, tp, sp), tp + np.arange(0, ts, ss), [t_all]])
            tile0 = np.arange(nt) * tmr
            keep_prev = jnp.asarray(~np.isin(tile0, starts), F32)[:, None]
            keep_next = jnp.asarray(~np.isin(tile0 + tmr, starts), F32)[:, None]
            zt = z.reshape(nt, tmr, z.shape[-1])
            zero_row = jnp.zeros((1, z.shape[-1]), F32)
            prev_rows = (jnp.concatenate([zero_row, zt[:-1, tmr - 1]], axis=0) * keep_prev)[:, None, :]
            next_rows = (jnp.concatenate([zt[1:, 0], zero_row], axis=0) * keep_next)[:, None, :]
            r, kk, vv, w0, w1, b0, b1, kt0, kt1, g, bonus = _rwkv_pre(z, prev_rows, next_rows, PA, tmr, offs)
            outs = []
            for (row0, bn, seq, latent) in ((0, bp, sp, False), (tp, bs, ss, True)):
                rows = bn * seq
                q, k, v, k_f32, v_f32 = _attn_pre(z, even_b_qnorm[j], even_b_knorm[j], offs, row0, rows, seq,
                                                  tmr, latent)
                k = k.reshape(bn, seq, kv_width)
                v = v.reshape(bn, seq, kv_width)
                if latent:
                    past = cache_b_k.shape[2]
                    k = jnp.concatenate([k, cache_b_k[:, j].astype(BF16).reshape(bn, past, kv_width)], axis=1)
                    v = jnp.concatenate([v, cache_b_v[:, j].astype(BF16).reshape(bn, past, kv_width)], axis=1)
                    s0 = state_a[:, j]
                else:
                    s0 = None
                    new_k.append(k_f32.reshape(bn, seq, kvh, dh))
                    new_v.append(v_f32.reshape(bn, seq, kvh, dh))
                y_b = _attend(q.reshape(bn, seq, b_width), k, v, dh).reshape(rows, b_width)
                yf, yb, s_fin, kq = _rwkv_scan_pass(r, kk, vv, (w0, w1), (b0, b1), (kt0, kt1), s0, row0, bn, seq)
                if not latent:
                    new_s.append(s_fin)
                y_a = _rwkv_post_chain(yf, yb, bonus, g, row0, bn, a_width // A_HEAD_DIM, kq, PA)
                outs.append(jnp.concatenate([y_a, y_b], axis=-1))
            mix_in = jnp.concatenate(outs, axis=0)
            x = _res_matmul(mix_in, even_w_out[j].astype(BF16), x, mod3, 2, seg, tm)
        else:
            PC = dict(c_fw1=odd_c_fw1[j], c_fb1=odd_c_fb1[j], c_freq=odd_c_freq[j], c_fw2=odd_c_fw2[j],
                      c_fb2=odd_c_fb2[j], c_fw3=odd_c_fw3[j])
            u_pre, _ = _nm_matmul(x, norm1[layer], mod3, 1, 0, segs, odd_w_in[j].astype(BF16))
            taps = odd_c_conv[j]
            tap_bias = odd_c_conv_b[j].reshape(1, 3 * c_width)
            zs = []
            for (row0, bn, seq) in ((0, bp, sp), (tp, bs, ss)):
                tc = _col_tile(c_width, max(LANES, HY_TILE_ELEMS // seq))
                spectra = _hyena_spectra(seq, PC, c_width, tc)
                z1 = _hyena_conv(u_pre, 2 * c_width, u_pre, 0, taps, tap_bias, odd_c_bias[j, 0:1], spectra, 0,
                                 row0, bn, seq, tc, True)
                z2 = _hyena_conv(z1.reshape(bn * seq, c_width), 0, u_pre, c_width, taps, tap_bias,
                                 odd_c_bias[j, 1:2], spectra, 1, row0, bn, seq, tc, False)
                zs.append(z2.reshape(bn * seq, c_width))
            x = _res_matmul(jnp.concatenate(zs, axis=0).astype(BF16), odd_w_out[j].astype(BF16), x, mod3, 2, seg, tm)
        q, hm = _nm_matmul(x, norm2[layer], mod3, 4, 3, segs, peer_wq[layer].astype(BF16))
        i1, i2, gate = _peer_route(q, peer_keys[layer])
        gmat = _gate_matrix(i1, i2, gate, nkeys)
        u_bf16, v_bf16 = _to_bf16(peer_u, layer), _to_bf16(peer_v, layer)
        if layer < depth - 1:
            x = _peer_experts(hm, gmat, u_bf16, v_bf16, x, mod3, 5, seg, tm)
        else:
            y_prompt = _peer_experts(hm, gmat, u_bf16, v_bf16, x, mod3, 5, seg, tm, 0, tp).reshape(bp, sp, d)
            y_sample = _peer_experts(hm, gmat, u_bf16, v_bf16, x, mod3, 5, seg, tm, tp, ts).reshape(bs, ss, d)
    return (y_prompt, y_sample, jnp.stack(new_k, axis=1), jnp.stack(new_v, axis=1), jnp.stack(new_s, axis=1))
```
